```python
import jax, jax.numpy as jnp
from jax import lax
import numpy as np

D_MODEL = 1024
BATCH = 2
SEQ = 16384
DEPTH = 4

N_MIXERS = 2
N_CONV_LAYERS = (DEPTH + N_MIXERS - 1) // N_MIXERS
N_MLA_LAYERS = DEPTH // N_MIXERS
CONV_WIDTH = 3
N_HEADS = 8
Q_LORA_RANK = 384
KV_LORA_RANK = 256
QK_NOPE_DIM = 128
QK_ROPE_DIM = 64
QK_HEAD_DIM = QK_NOPE_DIM + QK_ROPE_DIM
V_HEAD_DIM = 128
LATENT_DIM = Q_LORA_RANK + KV_LORA_RANK + QK_ROPE_DIM
ROPE_THETA = 10000.0
Q_BLOCK = 128
N_GROUPS = 8
EXPERTS_PER_GROUP = 8
N_EXPERTS = N_GROUPS * EXPERTS_PER_GROUP
TOP_K = 2
D_FF_EXPERT = 384
MOE_BLOCK = 128
RMS_EPS = 1e-6
MAX_POS_OFFSET = 1024

kernel_name = 'hybrid_conv_mla_hier_moe_encoder'


def rms_norm(x, w):
    xf = x.astype(jnp.float32)
    y = xf * lax.rsqrt(jnp.mean(xf * xf, axis=-1, keepdims=True) + RMS_EPS)
    return (y * w.astype(jnp.float32)).astype(x.dtype)


def rope(x, cos, sin):
    half = x.shape[-1] // 2
    x1, x2 = x[..., :half], x[..., half:]
    return jnp.concatenate([x1 * cos - x2 * sin, x2 * cos + x1 * sin], axis=-1)


def short_conv_mixer(h, w_in, conv_k, w_out):
    bcu = h @ w_in
    gb, gc, u = jnp.split(bcu, 3, axis=-1)
    z = gc * u
    zp = jnp.pad(z, ((0, 0), (1, 1), (0, 0)))
    conv = zp[:, :-2] * conv_k[0] + zp[:, 1:-1] * conv_k[1] + zp[:, 2:] * conv_k[2]
    return (gb * conv) @ w_out


def mla_mixer(h, cos, sin, w_down, q_lat_norm, kv_lat_norm, w_q_up, w_kv_up,
              q_head_norm, k_head_norm, w_out):
    b, s, _ = h.shape
    lat = h @ w_down
    q_lat = rms_norm(lat[..., :Q_LORA_RANK], q_lat_norm)
    kv_lat = rms_norm(lat[..., Q_LORA_RANK:Q_LORA_RANK + KV_LORA_RANK], kv_lat_norm)
    k_rope = lat[..., Q_LORA_RANK + KV_LORA_RANK:]
    q = (q_lat @ w_q_up).reshape(b, s, N_HEADS, QK_HEAD_DIM)
    kv = (kv_lat @ w_kv_up).reshape(b, s, N_HEADS, QK_NOPE_DIM + V_HEAD_DIM)
    k_nope, v = kv[..., :QK_NOPE_DIM], kv[..., QK_NOPE_DIM:]
    k = jnp.concatenate(
        [k_nope, jnp.broadcast_to(k_rope[:, :, None, :], (b, s, N_HEADS, QK_ROPE_DIM))], axis=-1)
    q = rms_norm(q, q_head_norm)
    k = rms_norm(k, k_head_norm)
    q = jnp.concatenate([q[..., :QK_NOPE_DIM], rope(q[..., QK_NOPE_DIM:], cos, sin)], axis=-1)
    k = jnp.concatenate([k[..., :QK_NOPE_DIM], rope(k[..., QK_NOPE_DIM:], cos, sin)], axis=-1)
    scale = QK_HEAD_DIM ** -0.5
    n_qb = s // Q_BLOCK
    q_blocks = q.reshape(b, n_qb, Q_BLOCK, N_HEADS, QK_HEAD_DIM).transpose(1, 0, 2, 3, 4)

    def attend(qb):
        scores = jnp.einsum('bqhd,bkhd->bhqk', qb, k).astype(jnp.float32) * scale
        probs = jax.nn.softmax(scores, axis=-1).astype(v.dtype)
        return jnp.einsum('bhqk,bkhd->bqhd', probs, v)

    o = lax.map(attend, q_blocks)
    o = o.transpose(1, 0, 2, 3, 4).reshape(b, s, N_HEADS * V_HEAD_DIM)
    return o @ w_out


def hier_moe(h, w_group, b_group, w_expert, b_expert, w_gate, w_up, w_down):
    b, s, d = h.shape
    t = h.reshape(b * s, d)
    n_tok = b * s
    tok_ids = jnp.arange(n_tok)
    g_logits = (t @ w_group).astype(jnp.float32)
    g_prob = jax.nn.softmax(g_logits, axis=-1)
    g_sel = jnp.argmax(g_logits + b_group.astype(jnp.float32), axis=-1)
    g_gate = g_prob[tok_ids, g_sel]
    e_logits = (t @ w_expert).astype(jnp.float32).reshape(n_tok, N_GROUPS, EXPERTS_PER_GROUP)
    e_logits = e_logits[tok_ids, g_sel]
    e_bias = b_expert.astype(jnp.float32).reshape(N_GROUPS, EXPERTS_PER_GROUP)[g_sel]
    _, e_top = lax.top_k(e_logits + e_bias, TOP_K)
    e_prob = jnp.take_along_axis(jax.nn.softmax(e_logits, axis=-1), e_top, axis=-1)
    e_prob = e_prob / jnp.sum(e_prob, axis=-1, keepdims=True)
    gate = g_gate[:, None] * e_prob
    expert_id = g_sel[:, None] * EXPERTS_PER_GROUP + e_top
    n_assign = n_tok * TOP_K
    flat_e = expert_id.reshape(n_assign).astype(jnp.int32)
    flat_tok = jnp.repeat(tok_ids.astype(jnp.int32), TOP_K)
    flat_gate = gate.reshape(n_assign)
    order = jnp.argsort(flat_e)
    se, stok, sgate = flat_e[order], flat_tok[order], flat_gate[order]
    counts = jnp.bincount(flat_e, length=N_EXPERTS)
    starts = jnp.cumsum(counts) - counts
    padded = (counts + MOE_BLOCK - 1) // MOE_BLOCK * MOE_BLOCK
    pends = jnp.cumsum(padded)
    pstarts = pends - padded
    dest = pstarts[se] + (jnp.arange(n_assign) - starts[se])
    n_blocks = -(-n_assign // MOE_BLOCK) + N_EXPERTS
    n_rows = n_blocks * MOE_BLOCK
    buf_tok = jnp.zeros((n_rows,), jnp.int32).at[dest].set(stok)
    buf_gate = jnp.zeros((n_rows,), jnp.float32).at[dest].set(sgate)
    block_expert = jnp.minimum(
        jnp.searchsorted(pends, jnp.arange(n_blocks) * MOE_BLOCK, side='right'), N_EXPERTS - 1)
    xb = t[buf_tok].reshape(n_blocks, MOE_BLOCK, d)

    def run_block(args):
        xblk, e = args
        hid = jax.nn.silu(xblk @ w_gate[e]) * (xblk @ w_up[e])
        return hid @ w_down[e]

    yb = lax.map(run_block, (xb, block_expert)).reshape(n_rows, d)
    out = jnp.zeros((n_tok, d), h.dtype).at[buf_tok].add(yb * buf_gate[:, None].astype(yb.dtype))
    return out.reshape(b, s, d)


def setup_inputs(seed: int = 0) -> dict:
    key = jax.random.key(seed)
    ks = iter(jax.random.split(key, 32))
    f32 = jnp.float32

    def w(shape, fan_in):
        return jax.random.normal(next(ks), shape, f32) * (fan_in ** -0.5)

    def gain(shape):
        return 1.0 + 0.02 * jax.random.normal(next(ks), shape, f32)

    x = jax.random.normal(next(ks), (BATCH, SEQ, D_MODEL), f32)
    offsets = jax.random.randint(next(ks), (BATCH, 1), 0, MAX_POS_OFFSET, dtype=jnp.int32)
    positions = offsets + jnp.arange(SEQ, dtype=jnp.int32)[None, :]
    nc, nm, L = N_CONV_LAYERS, N_MLA_LAYERS, DEPTH
    return {
        'x': x,
        'positions': positions,
        'conv_norm_w': gain((nc, D_MODEL)),
        'conv_w_in': w((nc, D_MODEL, 3 * D_MODEL), D_MODEL),
        'conv_k': w((nc, CONV_WIDTH, D_MODEL), CONV_WIDTH),
        'conv_w_out': w((nc, D_MODEL, D_MODEL), D_MODEL),
        'mla_norm_w': gain((nm, D_MODEL)),
        'mla_w_down': w((nm, D_MODEL, LATENT_DIM), D_MODEL),
        'mla_q_lat_norm': gain((nm, Q_LORA_RANK)),
        'mla_kv_lat_norm': gain((nm, KV_LORA_RANK)),
        'mla_w_q_up': w((nm, Q_LORA_RANK, N_HEADS * QK_HEAD_DIM), Q_LORA_RANK),
        'mla_w_kv_up': w((nm, KV_LORA_RANK, N_HEADS * (QK_NOPE_DIM + V_HEAD_DIM)), KV_LORA_RANK),
        'mla_q_head_norm': gain((nm, QK_HEAD_DIM)),
        'mla_k_head_norm': gain((nm, QK_HEAD_DIM)),
        'mla_w_out': w((nm, N_HEADS * V_HEAD_DIM, D_MODEL), N_HEADS * V_HEAD_DIM),
        'ffn_norm_w': gain((L, D_MODEL)),
        'router_w_group': w((L, D_MODEL, N_GROUPS), D_MODEL),
        'router_b_group': 0.01 * jax.random.normal(next(ks), (L, N_GROUPS), f32),
        'router_w_expert': w((L, D_MODEL, N_EXPERTS), D_MODEL),
        'router_b_expert': 0.01 * jax.random.normal(next(ks), (L, N_EXPERTS), f32),
        'exp_w_gate': w((L, N_EXPERTS, D_MODEL, D_FF_EXPERT), D_MODEL),
        'exp_w_up': w((L, N_EXPERTS, D_MODEL, D_FF_EXPERT), D_MODEL),
        'exp_w_down': w((L, N_EXPERTS, D_FF_EXPERT, D_MODEL), D_FF_EXPERT),
    }


def reference(x, positions, conv_norm_w, conv_w_in, conv_k, conv_w_out,
              mla_norm_w, mla_w_down, mla_q_lat_norm, mla_kv_lat_norm, mla_w_q_up,
              mla_w_kv_up, mla_q_head_norm, mla_k_head_norm, mla_w_out,
              ffn_norm_w, router_w_group, router_b_group, router_w_expert, router_b_expert,
              exp_w_gate, exp_w_up, exp_w_down):
    inv_freq = ROPE_THETA ** (-jnp.arange(0, QK_ROPE_DIM, 2, dtype=jnp.float32) / QK_ROPE_DIM)
    ang = positions.astype(jnp.float32)[..., None] * inv_freq
    cos = jnp.cos(ang)[:, :, None, :].astype(x.dtype)
    sin = jnp.sin(ang)[:, :, None, :].astype(x.dtype)
    for i in range(DEPTH):
        j = i // N_MIXERS
        if i % N_MIXERS == 0:
            h = rms_norm(x, conv_norm_w[j])
            x = x + short_conv_mixer(h, conv_w_in[j], conv_k[j], conv_w_out[j])
        else:
            h = rms_norm(x, mla_norm_w[j])
            x = x + mla_mixer(h, cos, sin, mla_w_down[j], mla_q_lat_norm[j], mla_kv_lat_norm[j],
                              mla_w_q_up[j], mla_w_kv_up[j], mla_q_head_norm[j],
                              mla_k_head_norm[j], mla_w_out[j])
        h = rms_norm(x, ffn_norm_w[i])
        x = x + hier_moe(h, router_w_group[i], router_b_group[i], router_w_expert[i],
                         router_b_expert[i], exp_w_gate[i], exp_w_up[i], exp_w_down[i])
    return x
```

```python
import functools
import math

import jax
import jax.numpy as jnp
from jax import lax
from jax.experimental import pallas as pl
from jax.experimental.pallas import tpu as pltpu

D_MODEL = 1024
BATCH = 2
SEQ = 16384
DEPTH = 4
N_TOK = BATCH * SEQ
N_MIXERS = 2
N_HEADS = 8
Q_LORA_RANK = 384
KV_LORA_RANK = 256
QK_NOPE_DIM = 128
QK_ROPE_DIM = 64
QK_HEAD_DIM = QK_NOPE_DIM + QK_ROPE_DIM
V_HEAD_DIM = 128
LATENT_DIM = Q_LORA_RANK + KV_LORA_RANK + QK_ROPE_DIM
LATENT_AUG = LATENT_DIM + QK_ROPE_DIM
ROPE_THETA = 10000.0
N_GROUPS = 8
EXPERTS_PER_GROUP = 8
N_EXPERTS = N_GROUPS * EXPERTS_PER_GROUP
TOP_K = 2
D_FF_EXPERT = 384
RMS_EPS = 1e-6

LANES = 128
VMEM_LIMIT = 56 * 1024 * 1024

TM = 512
TM_ROW = 256
MOE_ROWS = 256
N_MOE_BLOCKS = N_TOK * TOP_K // MOE_ROWS + N_EXPERTS
N_MOE_ROWS = N_MOE_BLOCKS * MOE_ROWS
TQ = 512
TK = 512
QK_SCALE_LOG2E = (QK_HEAD_DIM ** -0.5) * math.log2(math.e)

bf16 = jnp.bfloat16
f32 = jnp.float32


def _cparams(sem):
    return pltpu.CompilerParams(dimension_semantics=sem, vmem_limit_bytes=VMEM_LIMIT)


def _rms(x, w):
    return x * lax.rsqrt(jnp.mean(x * x, axis=-1, keepdims=True) + RMS_EPS) * w


def _dot(a, b):
    return jnp.dot(a, b, preferred_element_type=f32)


def _dot_nt(a, b):
    return lax.dot_general(a, b, (((1,), (1,)), ((), ())), preferred_element_type=f32)


def _conv_in_kernel(x_ref, nw_ref, w_ref, z_ref, gb_ref):
    h = _rms(x_ref[...], nw_ref[...]).astype(bf16)
    gb = _dot(h, w_ref[:, 0:D_MODEL])
    gc = _dot(h, w_ref[:, D_MODEL:2 * D_MODEL])
    u = _dot(h, w_ref[:, 2 * D_MODEL:3 * D_MODEL])
    z_ref[...] = (gc * u).astype(bf16)
    gb_ref[...] = gb.astype(bf16)


def _conv_in(x, nw, w_in):
    return pl.pallas_call(
        _conv_in_kernel,
        grid=(N_TOK // TM,),
        in_specs=[pl.BlockSpec((TM, D_MODEL), lambda i: (i, 0)),
                  pl.BlockSpec((1, D_MODEL), lambda i: (0, 0)),
                  pl.BlockSpec((D_MODEL, 3 * D_MODEL), lambda i: (0, 0))],
        out_specs=[pl.BlockSpec((TM, D_MODEL), lambda i: (i, 0)),
                   pl.BlockSpec((TM, D_MODEL), lambda i: (i, 0))],
        out_shape=[jax.ShapeDtypeStruct((N_TOK, D_MODEL), bf16),
                   jax.ShapeDtypeStruct((N_TOK, D_MODEL), bf16)],
        compiler_params=_cparams(("arbitrary",)),
        name="conv_in",
    )(x, nw, w_in)


def _conv_out_kernel(x_ref, z_ref, zp_ref, zn_ref, gb_ref, k_ref, w_ref, o_ref):
    i = pl.program_id(0)
    tiles_per_seq = SEQ // TM
    first = (i % tiles_per_seq) == 0
    last = (i % tiles_per_seq) == tiles_per_seq - 1
    z = z_ref[...].astype(f32)
    prev_row = jnp.where(first, 0.0, zp_ref[7:8, :].astype(f32))
    next_row = jnp.where(last, 0.0, zn_ref[0:1, :].astype(f32))
    row = lax.broadcasted_iota(jnp.int32, z.shape, 0)
    z_m1 = jnp.where(row == 0, prev_row, pltpu.roll(z, 1, 0))
    z_p1 = jnp.where(row == TM - 1, next_row, pltpu.roll(z, TM - 1, 0))
    conv = z_m1 * k_ref[0:1, :] + z * k_ref[1:2, :] + z_p1 * k_ref[2:3, :]
    y = _dot((gb_ref[...].astype(f32) * conv).astype(bf16), w_ref[...])
    o_ref[...] = x_ref[...] + y


def _conv_out(x, z, gb, conv_k, w_out):
    n_halo = N_TOK // 8
    per = TM // 8
    return pl.pallas_call(
        _conv_out_kernel,
        grid=(N_TOK // TM,),
        in_specs=[pl.BlockSpec((TM, D_MODEL), lambda i: (i, 0)),
                  pl.BlockSpec((TM, D_MODEL), lambda i: (i, 0)),
                  pl.BlockSpec((8, D_MODEL), lambda i: (jnp.maximum(i * per - 1, 0), 0)),
                  pl.BlockSpec((8, D_MODEL), lambda i: (jnp.minimum((i + 1) * per, n_halo - 1), 0)),
                  pl.BlockSpec((TM, D_MODEL), lambda i: (i, 0)),
                  pl.BlockSpec((3, D_MODEL), lambda i: (0, 0)),
                  pl.BlockSpec((D_MODEL, D_MODEL), lambda i: (0, 0))],
        out_specs=pl.BlockSpec((TM, D_MODEL), lambda i: (i, 0)),
        out_shape=jax.ShapeDtypeStruct((N_TOK, D_MODEL), f32),
        compiler_params=_cparams(("arbitrary",)),
        name="conv_out",
    )(x, z, z, z, gb, conv_k, w_out)


def _mla_proj_kernel(x_ref, nw_ref, wd_ref, qln_ref, kvln_ref, wqt_ref, wk_ref, wvt_ref,
                     qhn_ref, khn_ref, g128_ref, cs_ref, cost_ref, sint_ref,
                     qt_ref, k_ref, vt_ref):
    h = _rms(x_ref[...], nw_ref[...]).astype(bf16)
    lat = _dot(h, wd_ref[...])
    q_lat = _rms(lat[:, 0:Q_LORA_RANK], qln_ref[...]).astype(bf16)
    kv_lat = _rms(lat[:, Q_LORA_RANK:Q_LORA_RANK + KV_LORA_RANK], kvln_ref[...]).astype(bf16)
    kr_blk = lat[:, Q_LORA_RANK + KV_LORA_RANK:LATENT_AUG]
    lane = lax.broadcasted_iota(jnp.int32, kr_blk.shape, 1)
    kr_ss = jnp.sum(jnp.where(lane < QK_ROPE_DIM, kr_blk * kr_blk, 0.0), axis=-1, keepdims=True)
    t = kr_blk * g128_ref[...] * cs_ref[...]
    kr_roped = t + pltpu.roll(t, QK_ROPE_DIM, 1)

    k_nope = _dot(kv_lat, wk_ref[...])
    cos_t = cost_ref[0]
    sin_t = sint_ref[0]
    for hd in range(N_HEADS):
        kn = k_nope[:, hd * QK_NOPE_DIM:(hd + 1) * QK_NOPE_DIM]
        ss = jnp.sum(kn * kn, axis=-1, keepdims=True) + kr_ss
        r = lax.rsqrt(ss * (1.0 / QK_HEAD_DIM) + RMS_EPS)
        k_ref[0, hd, :, 0:QK_NOPE_DIM] = (kn * r * khn_ref[:, 0:QK_NOPE_DIM]).astype(bf16)
        k_ref[0, hd, :, QK_NOPE_DIM:QK_HEAD_DIM] = (kr_roped[:, 0:QK_ROPE_DIM] * r).astype(bf16)

        qt = _dot_nt(wqt_ref[hd * QK_HEAD_DIM:(hd + 1) * QK_HEAD_DIM, :], q_lat)
        rq = lax.rsqrt(jnp.mean(qt * qt, axis=0, keepdims=True) + RMS_EPS)
        qn = qt * rq * qhn_ref[...] * QK_SCALE_LOG2E
        half = QK_ROPE_DIM // 2
        x1 = qn[QK_NOPE_DIM:QK_NOPE_DIM + half, :]
        x2 = qn[QK_NOPE_DIM + half:QK_HEAD_DIM, :]
        qt_ref[0, hd, 0:QK_NOPE_DIM, :] = qn[0:QK_NOPE_DIM, :].astype(bf16)
        qt_ref[0, hd, QK_NOPE_DIM:QK_NOPE_DIM + half, :] = (x1 * cos_t - x2 * sin_t).astype(bf16)
        qt_ref[0, hd, QK_NOPE_DIM + half:QK_HEAD_DIM, :] = (x2 * cos_t + x1 * sin_t).astype(bf16)

        vt = _dot_nt(wvt_ref[hd * V_HEAD_DIM:(hd + 1) * V_HEAD_DIM, :], kv_lat)
        vt_ref[0, hd, :, :] = vt.astype(bf16)


def _mla_proj(x, nw, wd_aug, qln, kvln, wqt, wk, wvt, qhn_col, khn_row, g128, cs, cos_t, sin_t):
    tps = SEQ // TM
    const = lambda i: (0, 0)
    return pl.pallas_call(
        _mla_proj_kernel,
        grid=(N_TOK // TM,),
        in_specs=[pl.BlockSpec((TM, D_MODEL), lambda i: (i, 0)),
                  pl.BlockSpec((1, D_MODEL), const),
                  pl.BlockSpec((D_MODEL, LATENT_AUG), const),
                  pl.BlockSpec((1, Q_LORA_RANK), const),
                  pl.BlockSpec((1, KV_LORA_RANK), const),
                  pl.BlockSpec((N_HEADS * QK_HEAD_DIM, Q_LORA_RANK), const),
                  pl.BlockSpec((KV_LORA_RANK, N_HEADS * QK_NOPE_DIM), const),
                  pl.BlockSpec((N_HEADS * V_HEAD_DIM, KV_LORA_RANK), const),
                  pl.BlockSpec((QK_HEAD_DIM, 1), const),
                  pl.BlockSpec((1, QK_HEAD_DIM), const),
                  pl.BlockSpec((1, LANES), const),
                  pl.BlockSpec((TM, LANES), lambda i: (i, 0)),
                  pl.BlockSpec((1, QK_ROPE_DIM // 2, TM), lambda i: (i // tps, 0, i % tps)),
                  pl.BlockSpec((1, QK_ROPE_DIM // 2, TM), lambda i: (i // tps, 0, i % tps))],
        out_specs=[pl.BlockSpec((1, N_HEADS, QK_HEAD_DIM, TM), lambda i: (i // tps, 0, 0, i % tps)),
                   pl.BlockSpec((1, N_HEADS, TM, QK_HEAD_DIM), lambda i: (i // tps, 0, i % tps, 0)),
                   pl.BlockSpec((1, N_HEADS, V_HEAD_DIM, TM), lambda i: (i // tps, 0, 0, i % tps))],
        out_shape=[jax.ShapeDtypeStruct((BATCH, N_HEADS, QK_HEAD_DIM, SEQ), bf16),
                   jax.ShapeDtypeStruct((BATCH, N_HEADS, SEQ, QK_HEAD_DIM), bf16),
                   jax.ShapeDtypeStruct((BATCH, N_HEADS, V_HEAD_DIM, SEQ), bf16)],
        compiler_params=_cparams(("arbitrary",)),
        name="mla_proj",
    )(x, nw, wd_aug, qln, kvln, wqt, wk, wvt, qhn_col, khn_row, g128, cs, cos_t, sin_t)


def _attn_kernel(qt_ref, k_ref, vt_ref, o_ref, acc_ref):
    qt = qt_ref[0, 0]
    acc_ref[...] = jnp.zeros_like(acc_ref)

    def body(j, carry):
        m, l = carry
        off = pl.multiple_of(j * TK, TK)
        s = _dot(k_ref[0, 0, pl.ds(off, TK), :], qt)
        m_new = jnp.maximum(m, jnp.max(s, axis=0, keepdims=True))
        alpha = jnp.exp2(m - m_new)
        p = jnp.exp2(s - m_new)
        l = alpha * l + jnp.sum(p, axis=0, keepdims=True)
        pv = _dot(vt_ref[0, 0, :, pl.ds(off, TK)], p.astype(bf16))
        acc_ref[...] = alpha * acc_ref[...] + pv
        return m_new, l

    m0 = jnp.full((1, TQ), -jnp.inf, f32)
    l0 = jnp.zeros((1, TQ), f32)
    _, l = lax.fori_loop(0, SEQ // TK, body, (m0, l0))
    o_ref[0, 0] = (acc_ref[...] / l).astype(bf16)


def _attention(qt, k, vt):
    return pl.pallas_call(
        _attn_kernel,
        grid=(BATCH, N_HEADS, SEQ // TQ),
        in_specs=[pl.BlockSpec((1, 1, QK_HEAD_DIM, TQ), lambda b, h, i: (b, h, 0, i)),
                  pl.BlockSpec((1, 1, SEQ, QK_HEAD_DIM), lambda b, h, i: (b, h, 0, 0)),
                  pl.BlockSpec((1, 1, V_HEAD_DIM, SEQ), lambda b, h, i: (b, h, 0, 0))],
        out_specs=pl.BlockSpec((1, 1, V_HEAD_DIM, TQ), lambda b, h, i: (b, h, 0, i)),
        out_shape=jax.ShapeDtypeStruct((BATCH, N_HEADS, V_HEAD_DIM, SEQ), bf16),
        scratch_shapes=[pltpu.VMEM((V_HEAD_DIM, TQ), f32)],
        compiler_params=_cparams(("arbitrary", "arbitrary", "arbitrary")),
        name="mla_attention",
    )(qt, k, vt)


def _mla_out_kernel(x_ref, ot_ref, w_ref, o_ref):
    yt = _dot(w_ref[...], ot_ref[0])
    o_ref[...] = x_ref[...] + yt.T


def _mla_out(x, ot, w_out_t):
    tps = SEQ // TM
    return pl.pallas_call(
        _mla_out_kernel,
        grid=(N_TOK // TM,),
        in_specs=[pl.BlockSpec((TM, D_MODEL), lambda i: (i, 0)),
                  pl.BlockSpec((1, N_HEADS * V_HEAD_DIM, TM), lambda i: (i // tps, 0, i % tps)),
                  pl.BlockSpec((D_MODEL, N_HEADS * V_HEAD_DIM), lambda i: (0, 0))],
        out_specs=pl.BlockSpec((TM, D_MODEL), lambda i: (i, 0)),
        out_shape=jax.ShapeDtypeStruct((N_TOK, D_MODEL), f32),
        compiler_params=_cparams(("arbitrary",)),
        name="mla_out",
    )(x, ot, w_out_t)


def _first_argmax(v, n):
    idx = lax.broadcasted_iota(jnp.int32, v.shape, 0)
    mx = jnp.max(v, axis=0, keepdims=True)
    return jnp.min(jnp.where(v == mx, idx, n), axis=0, keepdims=True)


def _router_kernel(x_ref, nw_ref, wt_hi_ref, wt_lo_ref, bias_ref, tri_ref,
                   idx_ref, gate_ref, cnt_ref, base_ref):
    i = pl.program_id(0)

    @pl.when(i == 0)
    def _():
        base_ref[...] = jnp.zeros_like(base_ref)

    h = _rms(x_ref[...], nw_ref[...])
    h_hi = h.astype(bf16)
    h_lo = (h - h_hi.astype(f32)).astype(bf16)
    logits = (_dot_nt(wt_hi_ref[...], h_hi) + _dot_nt(wt_lo_ref[...], h_hi)
              + _dot_nt(wt_hi_ref[...], h_lo))
    g_logits = logits[0:N_GROUPS, :]
    g_exp = jnp.exp(g_logits - jnp.max(g_logits, axis=0, keepdims=True))
    g_prob = g_exp / jnp.sum(g_exp, axis=0, keepdims=True)
    g_sel = _first_argmax(g_logits + bias_ref[0:N_GROUPS, :], N_GROUPS)
    gidx = lax.broadcasted_iota(jnp.int32, g_logits.shape, 0)
    g_gate = jnp.sum(jnp.where(gidx == g_sel, g_prob, 0.0), axis=0, keepdims=True)

    e_logits = jnp.zeros((EXPERTS_PER_GROUP, TM), f32)
    e_bias = jnp.zeros((EXPERTS_PER_GROUP, TM), f32)
    for g in range(N_GROUPS):
        lo = N_GROUPS + g * EXPERTS_PER_GROUP
        e_logits = jnp.where(g_sel == g, logits[lo:lo + EXPERTS_PER_GROUP, :], e_logits)
        e_bias = jnp.where(g_sel == g, bias_ref[lo:lo + EXPERTS_PER_GROUP, :], e_bias)
    scored = e_logits + e_bias
    eidx = lax.broadcasted_iota(jnp.int32, scored.shape, 0)
    top0 = _first_argmax(scored, EXPERTS_PER_GROUP)
    top1 = _first_argmax(jnp.where(eidx == top0, -jnp.inf, scored), EXPERTS_PER_GROUP)
    e_exp = jnp.exp(e_logits - jnp.max(e_logits, axis=0, keepdims=True))
    e_prob = e_exp / jnp.sum(e_exp, axis=0, keepdims=True)
    p0 = jnp.sum(jnp.where(eidx == top0, e_prob, 0.0), axis=0, keepdims=True)
    p1 = jnp.sum(jnp.where(eidx == top1, e_prob, 0.0), axis=0, keepdims=True)
    psum = p0 + p1
    gate0 = g_gate * (p0 / psum)
    gate1 = g_gate * (p1 / psum)
    eid0 = g_sel * EXPERTS_PER_GROUP + top0
    eid1 = g_sel * EXPERTS_PER_GROUP + top1

    xidx = lax.broadcasted_iota(jnp.int32, (N_EXPERTS, TM), 0)
    oh0 = xidx == eid0
    oh1 = xidx == eid1
    pre0 = _dot(jnp.where(oh0, 1.0, 0.0).astype(bf16), tri_ref[...])
    pre1 = _dot(jnp.where(oh1, 1.0, 0.0).astype(bf16), tri_ref[...])
    tot0 = jnp.sum(jnp.where(oh0, 1.0, 0.0), axis=1, keepdims=True)
    tot1 = jnp.sum(jnp.where(oh1, 1.0, 0.0), axis=1, keepdims=True)
    base = base_ref[:, 0:1]
    rank0 = jnp.sum(jnp.where(oh0, base + pre0, 0.0), axis=0, keepdims=True)
    rank1 = jnp.sum(jnp.where(oh1, base + tot0 + pre1, 0.0), axis=0, keepdims=True)
    new_base = base + tot0 + tot1
    base_ref[...] = jnp.broadcast_to(new_base, base_ref.shape)
    cnt_ref[...] = jnp.broadcast_to(new_base, cnt_ref.shape)

    zi = jnp.zeros((4, TM), jnp.int32)
    idx_ref[...] = jnp.concatenate(
        [eid0, eid1, rank0.astype(jnp.int32), rank1.astype(jnp.int32), zi], axis=0)
    gate_ref[...] = jnp.concatenate([gate0, gate1, jnp.zeros((6, TM), f32)], axis=0)


def _router(x, nw, wt_hi, wt_lo, bias_col, tri):
    const = lambda i: (0, 0)
    return pl.pallas_call(
        _router_kernel,
        grid=(N_TOK // TM,),
        in_specs=[pl.BlockSpec((TM, D_MODEL), lambda i: (i, 0)),
                  pl.BlockSpec((1, D_MODEL), const),
                  pl.BlockSpec((LANES, D_MODEL), const),
                  pl.BlockSpec((LANES, D_MODEL), const),
                  pl.BlockSpec((LANES, 1), const),
                  pl.BlockSpec((TM, TM), const)],
        out_specs=[pl.BlockSpec((8, TM), lambda i: (0, i)),
                   pl.BlockSpec((8, TM), lambda i: (0, i)),
                   pl.BlockSpec((N_EXPERTS, LANES), const)],
        out_shape=[jax.ShapeDtypeStruct((8, N_TOK), jnp.int32),
                   jax.ShapeDtypeStruct((8, N_TOK), f32),
                   jax.ShapeDtypeStruct((N_EXPERTS, LANES), f32)],
        scratch_shapes=[pltpu.VMEM((N_EXPERTS, LANES), f32)],
        compiler_params=_cparams(("arbitrary",)),
        name="moe_router",
    )(x, nw, wt_hi, wt_lo, bias_col, tri)


N_ROW_TILES = N_TOK // TM_ROW


def _dispatch_kernel(x_ref, nw_ref, dest_hbm, xb_in, xb_hbm, hbuf, idx_smem, idx_sem, row_sem):
    del xb_in
    i = pl.program_id(0)
    slot = i % 2

    def idx_copy(tile, s):
        return pltpu.make_async_copy(dest_hbm.at[tile], idx_smem.at[s], idx_sem.at[s])

    def row_copy(s, k, r):
        d = idx_smem[s, k, r]
        return pltpu.make_async_copy(hbuf.at[s, pl.ds(r, 1), :], xb_hbm.at[pl.ds(d, 1), :],
                                     row_sem.at[s])

    def drain(s):
        def w(r, c):
            row_copy(s, 0, 0).wait()
            row_copy(s, 0, 0).wait()
            return c
        lax.fori_loop(0, TM_ROW, w, 0)

    @pl.when(i == 0)
    def _():
        idx_copy(0, 0).start()

    idx_copy(i, slot).wait()

    @pl.when(i + 1 < N_ROW_TILES)
    def _():
        idx_copy(i + 1, 1 - slot).start()

    @pl.when(i >= 2)
    def _():
        drain(slot)

    hbuf[slot] = _rms(x_ref[...], nw_ref[...])

    def issue(r, c):
        row_copy(slot, 0, r).start()
        row_copy(slot, 1, r).start()
        return c
    lax.fori_loop(0, TM_ROW, issue, 0)

    @pl.when(i == N_ROW_TILES - 1)
    def _():
        drain(1 - slot)
        drain(slot)


def _dispatch(x, nw, dest_tiles, xb_zero):
    return pl.pallas_call(
        _dispatch_kernel,
        grid=(N_ROW_TILES,),
        in_specs=[pl.BlockSpec((TM_ROW, D_MODEL), lambda i: (i, 0)),
                  pl.BlockSpec((1, D_MODEL), lambda i: (0, 0)),
                  pl.BlockSpec(memory_space=pl.ANY),
                  pl.BlockSpec(memory_space=pl.ANY)],
        out_specs=pl.BlockSpec(memory_space=pl.ANY),
        out_shape=jax.ShapeDtypeStruct((N_MOE_ROWS, D_MODEL), f32),
        scratch_shapes=[pltpu.VMEM((2, TM_ROW, D_MODEL), f32),
                        pltpu.SMEM((2, TOP_K, TM_ROW), jnp.int32),
                        pltpu.SemaphoreType.DMA((2,)),
                        pltpu.SemaphoreType.DMA((2,))],
        input_output_aliases={3: 0},
        compiler_params=_cparams(("arbitrary",)),
        name="moe_dispatch",
    )(x, nw, dest_tiles, xb_zero)


def _expert_kernel(be_ref, nused_ref, xb_ref, wg_ref, wu_ref, wd_ref, yb_ref, wg_b, wu_b, wd_b):
    j = pl.program_id(0)
    used = j < nused_ref[0]
    changed = jnp.logical_or(j == 0, be_ref[j] != be_ref[jnp.maximum(j - 1, 0)])

    @pl.when(jnp.logical_and(used, changed))
    def _():
        wg_b[...] = wg_ref[...].astype(bf16)
        wu_b[...] = wu_ref[...].astype(bf16)
        wd_b[...] = wd_ref[...].astype(bf16)

    @pl.when(used)
    def _():
        xb = xb_ref[...].astype(bf16)
        hid = jax.nn.silu(_dot(xb, wg_b[...])) * _dot(xb, wu_b[...])
        yb_ref[...] = _dot(hid.astype(bf16), wd_b[...])

    @pl.when(jnp.logical_not(used))
    def _():
        yb_ref[...] = jnp.zeros_like(yb_ref)


def _experts(block_expert, n_used, xb, w_gate, w_up, w_down):
    grid_spec = pltpu.PrefetchScalarGridSpec(
        num_scalar_prefetch=2,
        grid=(N_MOE_BLOCKS,),
        in_specs=[pl.BlockSpec((MOE_ROWS, D_MODEL), lambda j, be, nu: (j, 0)),
                  pl.BlockSpec((None, D_MODEL, D_FF_EXPERT), lambda j, be, nu: (be[j], 0, 0)),
                  pl.BlockSpec((None, D_MODEL, D_FF_EXPERT), lambda j, be, nu: (be[j], 0, 0)),
                  pl.BlockSpec((None, D_FF_EXPERT, D_MODEL), lambda j, be, nu: (be[j], 0, 0))],
        out_specs=pl.BlockSpec((MOE_ROWS, D_MODEL), lambda j, be, nu: (j, 0)),
        scratch_shapes=[pltpu.VMEM((D_MODEL, D_FF_EXPERT), bf16),
                        pltpu.VMEM((D_MODEL, D_FF_EXPERT), bf16),
                        pltpu.VMEM((D_FF_EXPERT, D_MODEL), bf16)])
    return pl.pallas_call(
        _expert_kernel,
        grid_spec=grid_spec,
        out_shape=jax.ShapeDtypeStruct((N_MOE_ROWS, D_MODEL), f32),
        compiler_params=_cparams(("arbitrary",)),
        name="moe_experts",
    )(block_expert, n_used, xb, w_gate, w_up, w_down)


def _combine_kernel(x_ref, gate_ref, dest_hbm, yb_hbm, o_ref, ybuf, idx_smem, idx_sem, row_sem):
    i = pl.program_id(0)
    slot = i % 2

    def idx_copy(tile, s):
        return pltpu.make_async_copy(dest_hbm.at[tile], idx_smem.at[s], idx_sem.at[s])

    def row_copy(s, k, r):
        d = idx_smem[s, k, r]
        return pltpu.make_async_copy(yb_hbm.at[pl.ds(d, 1), :], ybuf.at[s, k, pl.ds(r, 1), :],
                                     row_sem.at[s])

    def gather(s):
        def g(r, c):
            row_copy(s, 0, r).start()
            row_copy(s, 1, r).start()
            return c
        lax.fori_loop(0, TM_ROW, g, 0)

    @pl.when(i == 0)
    def _():
        idx_copy(0, 0).start()
        idx_copy(0, 0).wait()
        gather(0)

    @pl.when(i + 1 < N_ROW_TILES)
    def _():
        idx_copy(i + 1, 1 - slot).start()
        idx_copy(i + 1, 1 - slot).wait()
        gather(1 - slot)

    def w(r, c):
        row_copy(slot, 0, 0).wait()
        row_copy(slot, 0, 0).wait()
        return c
    lax.fori_loop(0, TM_ROW, w, 0)

    g0 = gate_ref[:, 0:1]
    g1 = gate_ref[:, 1:2]
    o_ref[...] = x_ref[...] + (ybuf[slot, 0] * g0 + ybuf[slot, 1] * g1)


def _combine(x, gate_cols, dest_tiles, yb):
    return pl.pallas_call(
        _combine_kernel,
        grid=(N_ROW_TILES,),
        in_specs=[pl.BlockSpec((TM_ROW, D_MODEL), lambda i: (i, 0)),
                  pl.BlockSpec((TM_ROW, TOP_K), lambda i: (i, 0)),
                  pl.BlockSpec(memory_space=pl.ANY),
                  pl.BlockSpec(memory_space=pl.ANY)],
        out_specs=pl.BlockSpec((TM_ROW, D_MODEL), lambda i: (i, 0)),
        out_shape=jax.ShapeDtypeStruct((N_TOK, D_MODEL), f32),
        scratch_shapes=[pltpu.VMEM((2, TOP_K, TM_ROW, D_MODEL), f32),
                        pltpu.SMEM((2, TOP_K, TM_ROW), jnp.int32),
                        pltpu.SemaphoreType.DMA((2,)),
                        pltpu.SemaphoreType.DMA((2,))],
        compiler_params=_cparams(("arbitrary",)),
        name="moe_combine",
    )(x, gate_cols, dest_tiles, yb)


def _hier_moe(x, nw, w_group, b_group, w_expert, b_expert, w_gate, w_up, w_down, tri):
    wt = jnp.concatenate([w_group, w_expert], axis=1).T
    wt = jnp.pad(wt, ((0, LANES - wt.shape[0]), (0, 0)))
    wt_hi = wt.astype(bf16)
    wt_lo = (wt - wt_hi.astype(f32)).astype(bf16)
    bias_col = jnp.pad(jnp.concatenate([b_group, b_expert]), (0, LANES - N_GROUPS - N_EXPERTS))[:, None]

    idx, gates, cnt = _router(x, nw, wt_hi, wt_lo, bias_col, tri)
    counts = cnt[:, 0].astype(jnp.int32)
    padded = (counts + MOE_ROWS - 1) // MOE_ROWS * MOE_ROWS
    pends = jnp.cumsum(padded)
    pstarts = pends - padded
    dest = jnp.take(pstarts, idx[0:2], axis=0) + idx[2:4]
    dest_tiles = dest.reshape(TOP_K, N_ROW_TILES, TM_ROW).transpose(1, 0, 2)
    block_expert = jnp.minimum(
        jnp.searchsorted(pends, jnp.arange(N_MOE_BLOCKS, dtype=jnp.int32) * MOE_ROWS, side='right'),
        N_EXPERTS - 1).astype(jnp.int32)
    n_used = (pends[-1:] // MOE_ROWS).astype(jnp.int32)

    xb = _dispatch(x, nw, dest_tiles, jnp.zeros((N_MOE_ROWS, D_MODEL), f32))
    yb = _experts(block_expert, n_used, xb, w_gate, w_up, w_down)
    return _combine(x, gates[0:2].T, dest_tiles, yb)


def kernel(x, positions, conv_norm_w, conv_w_in, conv_k, conv_w_out, mla_norm_w, mla_w_down, mla_q_lat_norm, mla_kv_lat_norm, mla_w_q_up, mla_w_kv_up, mla_q_head_norm, mla_k_head_norm, mla_w_out, ffn_norm_w, router_w_group, router_b_group, router_w_expert, router_b_expert, exp_w_gate, exp_w_up, exp_w_down):
    half = QK_ROPE_DIM // 2
    inv_freq = ROPE_THETA ** (-jnp.arange(0, QK_ROPE_DIM, 2, dtype=f32) / QK_ROPE_DIM)
    ang = positions.astype(f32)[..., None] * inv_freq
    cos = jnp.cos(ang)
    sin = jnp.sin(ang)
    cos_t = cos.transpose(0, 2, 1)
    sin_t = sin.transpose(0, 2, 1)
    cs = jnp.concatenate([cos, cos, -sin, sin], axis=-1).reshape(N_TOK, LANES)
    row = lax.broadcasted_iota(jnp.int32, (TM, TM), 0)
    col = lax.broadcasted_iota(jnp.int32, (TM, TM), 1)
    tri = (row < col).astype(bf16)

    xt = x.reshape(N_TOK, D_MODEL)
    for i in range(DEPTH):
        j = i // N_MIXERS
        if i % N_MIXERS == 0:
            z, gb = _conv_in(xt, conv_norm_w[j][None, :], conv_w_in[j].astype(bf16))
            xt = _conv_out(xt, z, gb, conv_k[j], conv_w_out[j].astype(bf16))
        else:
            wd = mla_w_down[j]
            ro = Q_LORA_RANK + KV_LORA_RANK
            wd_aug = jnp.concatenate([wd, wd[:, ro + half:ro + 2 * half], wd[:, ro:ro + half]],
                                     axis=1).astype(bf16)
            wkv = mla_w_kv_up[j].reshape(KV_LORA_RANK, N_HEADS, QK_NOPE_DIM + V_HEAD_DIM)
            wk = wkv[:, :, :QK_NOPE_DIM].reshape(KV_LORA_RANK, N_HEADS * QK_NOPE_DIM).astype(bf16)
            wvt = wkv[:, :, QK_NOPE_DIM:].reshape(KV_LORA_RANK, N_HEADS * V_HEAD_DIM).T.astype(bf16)
            wqt = mla_w_q_up[j].T.astype(bf16)
            khn = mla_k_head_norm[j]
            kw1 = khn[QK_NOPE_DIM:QK_NOPE_DIM + half]
            kw2 = khn[QK_NOPE_DIM + half:]
            g128 = jnp.concatenate([kw1, kw2, kw2, kw1])[None, :]
            qt, k, vt = _mla_proj(xt, mla_norm_w[j][None, :], wd_aug, mla_q_lat_norm[j][None, :],
                                  mla_kv_lat_norm[j][None, :], wqt, wk, wvt,
                                  mla_q_head_norm[j][:, None], khn[None, :], g128, cs, cos_t, sin_t)
            ot = _attention(qt, k, vt)
            xt = _mla_out(xt, ot.reshape(BATCH, N_HEADS * V_HEAD_DIM, SEQ), mla_w_out[j].T.astype(bf16))
        xt = _hier_moe(xt, ffn_norm_w[i][None, :], router_w_group[i], router_b_group[i],
                       router_w_expert[i], router_b_expert[i], exp_w_gate[i], exp_w_up[i],
                       exp_w_down[i], tri)
    return xt.reshape(BATCH, SEQ, D_MODEL)
```

```python
import math

import jax
import jax.numpy as jnp
from jax import lax
from jax.experimental import pallas as pl
from jax.experimental.pallas import tpu as pltpu

D_MODEL = 1024
BATCH = 2
SEQ = 16384
DEPTH = 4
N_TOK = BATCH * SEQ
N_MIXERS = 2
N_HEADS = 8
Q_LORA_RANK = 384
KV_LORA_RANK = 256
QK_NOPE_DIM = 128
QK_ROPE_DIM = 64
QK_HEAD_DIM = QK_NOPE_DIM + QK_ROPE_DIM
V_HEAD_DIM = 128
LATENT_DIM = Q_LORA_RANK + KV_LORA_RANK + QK_ROPE_DIM
LATENT_AUG = LATENT_DIM + QK_ROPE_DIM
ROPE_THETA = 10000.0
N_GROUPS = 8
EXPERTS_PER_GROUP = 8
N_EXPERTS = N_GROUPS * EXPERTS_PER_GROUP
TOP_K = 2
D_FF_EXPERT = 384
RMS_EPS = 1e-6

LANES = 128
VMEM_LIMIT = 56 * 1024 * 1024

TM = 512
TM_ROW = 256
ROW_UNROLL = 8
MOE_ROWS = 256
N_MOE_BLOCKS = N_TOK * TOP_K // MOE_ROWS + N_EXPERTS
N_MOE_ROWS = N_MOE_BLOCKS * MOE_ROWS
TQ = 512
TK = 512
QK_SCALE_LOG2E = (QK_HEAD_DIM ** -0.5) * math.log2(math.e)

bf16 = jnp.bfloat16
f32 = jnp.float32


def _cparams(sem):
    return pltpu.CompilerParams(dimension_semantics=sem, vmem_limit_bytes=VMEM_LIMIT)


def _rms(x, w):
    return x * lax.rsqrt(jnp.mean(x * x, axis=-1, keepdims=True) + RMS_EPS) * w


def _dot(a, b):
    return jnp.dot(a, b, preferred_element_type=f32)


def _dot_nt(a, b):
    return lax.dot_general(a, b, (((1,), (1,)), ((), ())), preferred_element_type=f32)


def _conv_in_kernel(x_ref, nw_ref, w_ref, z_ref, gb_ref):
    h = _rms(x_ref[...], nw_ref[...]).astype(bf16)
    gb = _dot(h, w_ref[:, 0:D_MODEL])
    gc = _dot(h, w_ref[:, D_MODEL:2 * D_MODEL])
    u = _dot(h, w_ref[:, 2 * D_MODEL:3 * D_MODEL])
    z_ref[...] = (gc * u).astype(bf16)
    gb_ref[...] = gb.astype(bf16)


def _conv_in(x, nw, w_in):
    return pl.pallas_call(
        _conv_in_kernel,
        grid=(N_TOK // TM,),
        in_specs=[pl.BlockSpec((TM, D_MODEL), lambda i: (i, 0)),
                  pl.BlockSpec((1, D_MODEL), lambda i: (0, 0)),
                  pl.BlockSpec((D_MODEL, 3 * D_MODEL), lambda i: (0, 0))],
        out_specs=[pl.BlockSpec((TM, D_MODEL), lambda i: (i, 0)),
                   pl.BlockSpec((TM, D_MODEL), lambda i: (i, 0))],
        out_shape=[jax.ShapeDtypeStruct((N_TOK, D_MODEL), bf16),
                   jax.ShapeDtypeStruct((N_TOK, D_MODEL), bf16)],
        compiler_params=_cparams(("arbitrary",)),
        name="conv_in",
    )(x, nw, w_in)


def _conv_out_kernel(x_ref, z_ref, zp_ref, zn_ref, gb_ref, k_ref, w_ref, o_ref):
    i = pl.program_id(0)
    tiles_per_seq = SEQ // TM
    first = (i % tiles_per_seq) == 0
    last = (i % tiles_per_seq) == tiles_per_seq - 1
    z = z_ref[...].astype(f32)
    prev_row = jnp.where(first, 0.0, zp_ref[7:8, :].astype(f32))
    next_row = jnp.where(last, 0.0, zn_ref[0:1, :].astype(f32))
    row = lax.broadcasted_iota(jnp.int32, z.shape, 0)
    z_m1 = jnp.where(row == 0, prev_row, pltpu.roll(z, 1, 0))
    z_p1 = jnp.where(row == TM - 1, next_row, pltpu.roll(z, TM - 1, 0))
    conv = z_m1 * k_ref[0:1, :] + z * k_ref[1:2, :] + z_p1 * k_ref[2:3, :]
    y = _dot((gb_ref[...].astype(f32) * conv).astype(bf16), w_ref[...])
    o_ref[...] = x_ref[...] + y


def _conv_out(x, z, gb, conv_k, w_out):
    n_halo = N_TOK // 8
    per = TM // 8
    return pl.pallas_call(
        _conv_out_kernel,
        grid=(N_TOK // TM,),
        in_specs=[pl.BlockSpec((TM, D_MODEL), lambda i: (i, 0)),
                  pl.BlockSpec((TM, D_MODEL), lambda i: (i, 0)),
                  pl.BlockSpec((8, D_MODEL), lambda i: (jnp.maximum(i * per - 1, 0), 0)),
                  pl.BlockSpec((8, D_MODEL), lambda i: (jnp.minimum((i + 1) * per, n_halo - 1), 0)),
                  pl.BlockSpec((TM, D_MODEL), lambda i: (i, 0)),
                  pl.BlockSpec((3, D_MODEL), lambda i: (0, 0)),
                  pl.BlockSpec((D_MODEL, D_MODEL), lambda i: (0, 0))],
        out_specs=pl.BlockSpec((TM, D_MODEL), lambda i: (i, 0)),
        out_shape=jax.ShapeDtypeStruct((N_TOK, D_MODEL), f32),
        compiler_params=_cparams(("arbitrary",)),
        name="conv_out",
    )(x, z, z, z, gb, conv_k, w_out)


def _mla_proj_kernel(x_ref, nw_ref, wd_ref, qln_ref, kvln_ref, wqt_ref, wk_ref, wvt_ref,
                     qhn_ref, khn_ref, g128_ref, cs_ref, cost_ref, sint_ref,
                     qt_ref, k_ref, vt_ref):
    h = _rms(x_ref[...], nw_ref[...]).astype(bf16)
    lat = _dot(h, wd_ref[...])
    q_lat = _rms(lat[:, 0:Q_LORA_RANK], qln_ref[...]).astype(bf16)
    kv_lat = _rms(lat[:, Q_LORA_RANK:Q_LORA_RANK + KV_LORA_RANK], kvln_ref[...]).astype(bf16)
    kr_blk = lat[:, Q_LORA_RANK + KV_LORA_RANK:LATENT_AUG]
    lane = lax.broadcasted_iota(jnp.int32, kr_blk.shape, 1)
    kr_ss = jnp.sum(jnp.where(lane < QK_ROPE_DIM, kr_blk * kr_blk, 0.0), axis=-1, keepdims=True)
    t = kr_blk * g128_ref[...] * cs_ref[...]
    kr_roped = t + pltpu.roll(t, QK_ROPE_DIM, 1)

    k_nope = _dot(kv_lat, wk_ref[...])
    cos_t = cost_ref[0]
    sin_t = sint_ref[0]
    for hd in range(N_HEADS):
        kn = k_nope[:, hd * QK_NOPE_DIM:(hd + 1) * QK_NOPE_DIM]
        ss = jnp.sum(kn * kn, axis=-1, keepdims=True) + kr_ss
        r = lax.rsqrt(ss * (1.0 / QK_HEAD_DIM) + RMS_EPS)
        k_ref[0, hd, :, 0:QK_NOPE_DIM] = (kn * r * khn_ref[:, 0:QK_NOPE_DIM]).astype(bf16)
        k_ref[0, hd, :, QK_NOPE_DIM:QK_HEAD_DIM] = (kr_roped[:, 0:QK_ROPE_DIM] * r).astype(bf16)

        qt = _dot_nt(wqt_ref[hd * QK_HEAD_DIM:(hd + 1) * QK_HEAD_DIM, :], q_lat)
        rq = lax.rsqrt(jnp.mean(qt * qt, axis=0, keepdims=True) + RMS_EPS)
        qn = qt * rq * qhn_ref[...] * QK_SCALE_LOG2E
        half = QK_ROPE_DIM // 2
        x1 = qn[QK_NOPE_DIM:QK_NOPE_DIM + half, :]
        x2 = qn[QK_NOPE_DIM + half:QK_HEAD_DIM, :]
        qt_ref[0, hd, 0:QK_NOPE_DIM, :] = qn[0:QK_NOPE_DIM, :].astype(bf16)
        qt_ref[0, hd, QK_NOPE_DIM:QK_NOPE_DIM + half, :] = (x1 * cos_t - x2 * sin_t).astype(bf16)
        qt_ref[0, hd, QK_NOPE_DIM + half:QK_HEAD_DIM, :] = (x2 * cos_t + x1 * sin_t).astype(bf16)

        vt = _dot_nt(wvt_ref[hd * V_HEAD_DIM:(hd + 1) * V_HEAD_DIM, :], kv_lat)
        vt_ref[0, hd, :, :] = vt.astype(bf16)


def _mla_proj(x, nw, wd_aug, qln, kvln, wqt, wk, wvt, qhn_col, khn_row, g128, cs, cos_t, sin_t):
    tps = SEQ // TM
    const = lambda i: (0, 0)
    return pl.pallas_call(
        _mla_proj_kernel,
        grid=(N_TOK // TM,),
        in_specs=[pl.BlockSpec((TM, D_MODEL), lambda i: (i, 0)),
                  pl.BlockSpec((1, D_MODEL), const),
                  pl.BlockSpec((D_MODEL, LATENT_AUG), const),
                  pl.BlockSpec((1, Q_LORA_RANK), const),
                  pl.BlockSpec((1, KV_LORA_RANK), const),
                  pl.BlockSpec((N_HEADS * QK_HEAD_DIM, Q_LORA_RANK), const),
                  pl.BlockSpec((KV_LORA_RANK, N_HEADS * QK_NOPE_DIM), const),
                  pl.BlockSpec((N_HEADS * V_HEAD_DIM, KV_LORA_RANK), const),
                  pl.BlockSpec((QK_HEAD_DIM, 1), const),
                  pl.BlockSpec((1, QK_HEAD_DIM), const),
                  pl.BlockSpec((1, LANES), const),
                  pl.BlockSpec((TM, LANES), lambda i: (i, 0)),
                  pl.BlockSpec((1, QK_ROPE_DIM // 2, TM), lambda i: (i // tps, 0, i % tps)),
                  pl.BlockSpec((1, QK_ROPE_DIM // 2, TM), lambda i: (i // tps, 0, i % tps))],
        out_specs=[pl.BlockSpec((1, N_HEADS, QK_HEAD_DIM, TM), lambda i: (i // tps, 0, 0, i % tps)),
                   pl.BlockSpec((1, N_HEADS, TM, QK_HEAD_DIM), lambda i: (i // tps, 0, i % tps, 0)),
                   pl.BlockSpec((1, N_HEADS, V_HEAD_DIM, TM), lambda i: (i // tps, 0, 0, i % tps))],
        out_shape=[jax.ShapeDtypeStruct((BATCH, N_HEADS, QK_HEAD_DIM, SEQ), bf16),
                   jax.ShapeDtypeStruct((BATCH, N_HEADS, SEQ, QK_HEAD_DIM), bf16),
                   jax.ShapeDtypeStruct((BATCH, N_HEADS, V_HEAD_DIM, SEQ), bf16)],
        compiler_params=_cparams(("arbitrary",)),
        name="mla_proj",
    )(x, nw, wd_aug, qln, kvln, wqt, wk, wvt, qhn_col, khn_row, g128, cs, cos_t, sin_t)


def _attn_kernel(qt_ref, k_ref, vt_ref, o_ref, acc_ref, s0_ref, s1_ref, p0_ref, p1_ref):
    qt = qt_ref[0, 0]
    n = SEQ // TK
    acc_ref[...] = jnp.zeros_like(acc_ref)

    def scores(j, s_ref):
        off = pl.multiple_of(j * TK, TK)
        s = _dot(k_ref[0, 0, pl.ds(off, TK), :], qt)
        s_ref[...] = s
        return jnp.max(s, axis=0, keepdims=True)

    def softmax(s_ref, p_ref, m, l, cmax):
        m_new = jnp.maximum(m, cmax)
        alpha = jnp.exp2(m - m_new)
        p = jnp.exp2(s_ref[...] - m_new)
        p_ref[...] = p.astype(bf16)
        return m_new, alpha * l + jnp.sum(p, axis=0, keepdims=True), alpha

    def values(j, p_ref, alpha):
        off = pl.multiple_of(j * TK, TK)
        pv = _dot(vt_ref[0, 0, :, pl.ds(off, TK)], p_ref[...])
        acc_ref[...] = alpha * acc_ref[...] + pv

    m = jnp.full((1, TQ), -jnp.inf, f32)
    l = jnp.zeros((1, TQ), f32)
    c0 = scores(0, s0_ref)
    c1 = scores(1, s1_ref)
    m, l, a0 = softmax(s0_ref, p0_ref, m, l, c0)
    c0 = scores(2, s0_ref)
    m, l, a1 = softmax(s1_ref, p1_ref, m, l, c1)
    values(0, p0_ref, a0)

    def pair(jj, carry):
        m, l, a_prev, c0 = carry
        j = 2 * jj
        c1 = scores(j + 1, s1_ref)
        m, l, a0 = softmax(s0_ref, p0_ref, m, l, c0)
        values(j - 1, p1_ref, a_prev)
        c0 = scores(j + 2, s0_ref)
        m, l, a1 = softmax(s1_ref, p1_ref, m, l, c1)
        values(j, p0_ref, a0)
        return m, l, a1, c0

    m, l, a_prev, c0 = lax.fori_loop(1, n // 2 - 1, pair, (m, l, a1, c0))
    c1 = scores(n - 1, s1_ref)
    m, l, a0 = softmax(s0_ref, p0_ref, m, l, c0)
    values(n - 3, p1_ref, a_prev)
    m, l, a1 = softmax(s1_ref, p1_ref, m, l, c1)
    values(n - 2, p0_ref, a0)
    values(n - 1, p1_ref, a1)
    o_ref[0, 0] = (acc_ref[...] / l).astype(bf16)


def _attention(qt, k, vt):
    return pl.pallas_call(
        _attn_kernel,
        grid=(BATCH, N_HEADS, SEQ // TQ),
        in_specs=[pl.BlockSpec((1, 1, QK_HEAD_DIM, TQ), lambda b, h, i: (b, h, 0, i)),
                  pl.BlockSpec((1, 1, SEQ, QK_HEAD_DIM), lambda b, h, i: (b, h, 0, 0)),
                  pl.BlockSpec((1, 1, V_HEAD_DIM, SEQ), lambda b, h, i: (b, h, 0, 0))],
        out_specs=pl.BlockSpec((1, 1, V_HEAD_DIM, TQ), lambda b, h, i: (b, h, 0, i)),
        out_shape=jax.ShapeDtypeStruct((BATCH, N_HEADS, V_HEAD_DIM, SEQ), bf16),
        scratch_shapes=[pltpu.VMEM((V_HEAD_DIM, TQ), f32),
                        pltpu.VMEM((TK, TQ), f32), pltpu.VMEM((TK, TQ), f32),
                        pltpu.VMEM((TK, TQ), bf16), pltpu.VMEM((TK, TQ), bf16)],
        compiler_params=_cparams(("arbitrary", "arbitrary", "arbitrary")),
        name="mla_attention",
    )(qt, k, vt)


def _mla_out_kernel(x_ref, ot_ref, w_ref, o_ref):
    yt = _dot(w_ref[...], ot_ref[0])
    o_ref[...] = x_ref[...] + yt.T


def _mla_out(x, ot, w_out_t):
    tps = SEQ // TM
    return pl.pallas_call(
        _mla_out_kernel,
        grid=(N_TOK // TM,),
        in_specs=[pl.BlockSpec((TM, D_MODEL), lambda i: (i, 0)),
                  pl.BlockSpec((1, N_HEADS * V_HEAD_DIM, TM), lambda i: (i // tps, 0, i % tps)),
                  pl.BlockSpec((D_MODEL, N_HEADS * V_HEAD_DIM), lambda i: (0, 0))],
        out_specs=pl.BlockSpec((TM, D_MODEL), lambda i: (i, 0)),
        out_shape=jax.ShapeDtypeStruct((N_TOK, D_MODEL), f32),
        compiler_params=_cparams(("arbitrary",)),
        name="mla_out",
    )(x, ot, w_out_t)


def _first_argmax(v, n):
    idx = lax.broadcasted_iota(jnp.int32, v.shape, 0)
    mx = jnp.max(v, axis=0, keepdims=True)
    return jnp.min(jnp.where(v == mx, idx, n), axis=0, keepdims=True)


def _router_kernel(x_ref, nw_ref, wt_hi_ref, wt_lo_ref, bias_ref, tri_ref,
                   idx_ref, gate_ref, cnt_ref, base_ref):
    i = pl.program_id(0)

    @pl.when(i == 0)
    def _():
        base_ref[...] = jnp.zeros_like(base_ref)

    h = _rms(x_ref[...], nw_ref[...])
    h_hi = h.astype(bf16)
    h_lo = (h - h_hi.astype(f32)).astype(bf16)
    logits = (_dot_nt(wt_hi_ref[...], h_hi) + _dot_nt(wt_lo_ref[...], h_hi)
              + _dot_nt(wt_hi_ref[...], h_lo))
    g_logits = logits[0:N_GROUPS, :]
    g_exp = jnp.exp(g_logits - jnp.max(g_logits, axis=0, keepdims=True))
    g_prob = g_exp / jnp.sum(g_exp, axis=0, keepdims=True)
    g_sel = _first_argmax(g_logits + bias_ref[0:N_GROUPS, :], N_GROUPS)
    gidx = lax.broadcasted_iota(jnp.int32, g_logits.shape, 0)
    g_gate = jnp.sum(jnp.where(gidx == g_sel, g_prob, 0.0), axis=0, keepdims=True)

    e_logits = jnp.zeros((EXPERTS_PER_GROUP, TM), f32)
    e_bias = jnp.zeros((EXPERTS_PER_GROUP, TM), f32)
    for g in range(N_GROUPS):
        lo = N_GROUPS + g * EXPERTS_PER_GROUP
        e_logits = jnp.where(g_sel == g, logits[lo:lo + EXPERTS_PER_GROUP, :], e_logits)
        e_bias = jnp.where(g_sel == g, bias_ref[lo:lo + EXPERTS_PER_GROUP, :], e_bias)
    scored = e_logits + e_bias
    eidx = lax.broadcasted_iota(jnp.int32, scored.shape, 0)
    top0 = _first_argmax(scored, EXPERTS_PER_GROUP)
    top1 = _first_argmax(jnp.where(eidx == top0, -jnp.inf, scored), EXPERTS_PER_GROUP)
    e_exp = jnp.exp(e_logits - jnp.max(e_logits, axis=0, keepdims=True))
    e_prob = e_exp / jnp.sum(e_exp, axis=0, keepdims=True)
    p0 = jnp.sum(jnp.where(eidx == top0, e_prob, 0.0), axis=0, keepdims=True)
    p1 = jnp.sum(jnp.where(eidx == top1, e_prob, 0.0), axis=0, keepdims=True)
    psum = p0 + p1
    gate0 = g_gate * (p0 / psum)
    gate1 = g_gate * (p1 / psum)
    eid0 = g_sel * EXPERTS_PER_GROUP + top0
    eid1 = g_sel * EXPERTS_PER_GROUP + top1

    xidx = lax.broadcasted_iota(jnp.int32, (N_EXPERTS, TM), 0)
    oh0 = xidx == eid0
    oh1 = xidx == eid1
    pre0 = _dot(jnp.where(oh0, 1.0, 0.0).astype(bf16), tri_ref[...])
    pre1 = _dot(jnp.where(oh1, 1.0, 0.0).astype(bf16), tri_ref[...])
    tot0 = jnp.sum(jnp.where(oh0, 1.0, 0.0), axis=1, keepdims=True)
    tot1 = jnp.sum(jnp.where(oh1, 1.0, 0.0), axis=1, keepdims=True)
    base = base_ref[:, 0:1]
    rank0 = jnp.sum(jnp.where(oh0, base + pre0, 0.0), axis=0, keepdims=True)
    rank1 = jnp.sum(jnp.where(oh1, base + tot0 + pre1, 0.0), axis=0, keepdims=True)
    new_base = base + tot0 + tot1
    base_ref[...] = jnp.broadcast_to(new_base, base_ref.shape)
    cnt_ref[...] = jnp.broadcast_to(new_base, cnt_ref.shape)

    zi = jnp.zeros((4, TM), jnp.int32)
    idx_ref[...] = jnp.concatenate(
        [eid0, eid1, rank0.astype(jnp.int32), rank1.astype(jnp.int32), zi], axis=0)
    gate_ref[...] = jnp.concatenate([gate0, gate1, jnp.zeros((6, TM), f32)], axis=0)


def _router(x, nw, wt_hi, wt_lo, bias_col, tri):
    const = lambda i: (0, 0)
    return pl.pallas_call(
        _router_kernel,
        grid=(N_TOK // TM,),
        in_specs=[pl.BlockSpec((TM, D_MODEL), lambda i: (i, 0)),
                  pl.BlockSpec((1, D_MODEL), const),
                  pl.BlockSpec((LANES, D_MODEL), const),
                  pl.BlockSpec((LANES, D_MODEL), const),
                  pl.BlockSpec((LANES, 1), const),
                  pl.BlockSpec((TM, TM), const)],
        out_specs=[pl.BlockSpec((8, TM), lambda i: (0, i)),
                   pl.BlockSpec((8, TM), lambda i: (0, i)),
                   pl.BlockSpec((N_EXPERTS, LANES), const)],
        out_shape=[jax.ShapeDtypeStruct((8, N_TOK), jnp.int32),
                   jax.ShapeDtypeStruct((8, N_TOK), f32),
                   jax.ShapeDtypeStruct((N_EXPERTS, LANES), f32)],
        scratch_shapes=[pltpu.VMEM((N_EXPERTS, LANES), f32)],
        compiler_params=_cparams(("arbitrary",)),
        name="moe_router",
    )(x, nw, wt_hi, wt_lo, bias_col, tri)


N_ROW_TILES = N_TOK // TM_ROW
ROW_TILE = 8
IDX_GROUPS = TM_ROW // LANES
IDX_TRIPS = LANES // ROW_UNROLL
IDX_PER_TRIP = IDX_GROUPS * ROW_UNROLL * TOP_K


def _issue_rows(idx_smem, s, make_copy):
    def body(r8, c):
        for g in range(IDX_GROUPS):
            for u in range(ROW_UNROLL):
                for k in range(TOP_K):
                    d = pl.multiple_of(idx_smem[s, r8, (g * ROW_UNROLL + u) * TOP_K + k] * ROW_TILE,
                                       ROW_TILE)
                    r = pl.multiple_of((g * LANES + r8 * ROW_UNROLL + u) * ROW_TILE, ROW_TILE)
                    make_copy(k, r, d).start(priority=k)
        return c
    lax.fori_loop(0, IDX_TRIPS, body, 0)


def _dest_tiles(dest):
    d = dest.reshape(TOP_K, N_ROW_TILES, IDX_GROUPS, IDX_TRIPS, ROW_UNROLL)
    return d.transpose(1, 3, 2, 4, 0).reshape(N_ROW_TILES, IDX_TRIPS, IDX_PER_TRIP)


def _dispatch_kernel(x_ref, nw_ref, dest_hbm, xb_in, xb_hbm, hbuf, idx_smem, idx_sem, row_sem):
    del xb_in
    i = pl.program_id(0)
    slot = i % 2

    def idx_copy(tile, s):
        return pltpu.make_async_copy(dest_hbm.at[tile], idx_smem.at[s], idx_sem.at[s])

    def drain(s):
        for _ in range(TOP_K):
            pltpu.make_async_copy(hbuf.at[s], xb_hbm.at[pl.ds(0, TM_ROW * ROW_TILE), :],
                                  row_sem.at[s]).wait()

    @pl.when(i == 0)
    def _():
        idx_copy(0, 0).start()

    idx_copy(i, slot).wait()

    @pl.when(i + 1 < N_ROW_TILES)
    def _():
        idx_copy(i + 1, 1 - slot).start()

    h = _rms(x_ref[...], nw_ref[...])
    for s in range(2):
        @pl.when(slot == s)
        def _(s=s):
            @pl.when(i >= 2)
            def _():
                drain(s)
            for c in range(ROW_TILE):
                hbuf[s, pl.ds(c, TM_ROW, stride=ROW_TILE), :] = h[:, c * LANES:(c + 1) * LANES]
            _issue_rows(idx_smem, s, lambda k, r, d: pltpu.make_async_copy(
                hbuf.at[s, pl.ds(r, ROW_TILE), :], xb_hbm.at[pl.ds(d, ROW_TILE), :], row_sem.at[s]))

    @pl.when(i == N_ROW_TILES - 1)
    def _():
        drain(0)
        drain(1)


def _dispatch(x, nw, dest_tiles, xb_zero):
    return pl.pallas_call(
        _dispatch_kernel,
        grid=(N_ROW_TILES,),
        in_specs=[pl.BlockSpec((TM_ROW, D_MODEL), lambda i: (i, 0)),
                  pl.BlockSpec((1, D_MODEL), lambda i: (0, 0)),
                  pl.BlockSpec(memory_space=pl.ANY),
                  pl.BlockSpec(memory_space=pl.ANY)],
        out_specs=pl.BlockSpec(memory_space=pl.ANY),
        out_shape=jax.ShapeDtypeStruct((N_MOE_ROWS * ROW_TILE, LANES), f32),
        scratch_shapes=[pltpu.VMEM((2, TM_ROW * ROW_TILE, LANES), f32),
                        pltpu.SMEM((2, IDX_TRIPS, IDX_PER_TRIP), jnp.int32),
                        pltpu.SemaphoreType.DMA((2,)),
                        pltpu.SemaphoreType.DMA((2,))],
        input_output_aliases={3: 0},
        compiler_params=_cparams(("arbitrary",)),
        name="moe_dispatch",
    )(x, nw, dest_tiles, xb_zero)


def _expert_kernel(be_ref, nused_ref, xb_ref, wg_ref, wu_ref, wd_ref, yb_ref, wg_b, wu_b, wd_b):
    j = pl.program_id(0)
    used = j < nused_ref[0]
    changed = jnp.logical_or(j == 0, be_ref[j] != be_ref[jnp.maximum(j - 1, 0)])

    @pl.when(jnp.logical_and(used, changed))
    def _():
        wg_b[...] = wg_ref[...].astype(bf16)
        wu_b[...] = wu_ref[...].astype(bf16)
        wd_b[...] = wd_ref[...].astype(bf16)

    @pl.when(used)
    def _():
        xb = jnp.concatenate([xb_ref[pl.ds(c, MOE_ROWS, stride=ROW_TILE), :] for c in range(ROW_TILE)],
                             axis=1).astype(bf16)
        hid = jax.nn.silu(_dot(xb, wg_b[...])) * _dot(xb, wu_b[...])
        y = _dot(hid.astype(bf16), wd_b[...])
        for c in range(ROW_TILE):
            yb_ref[pl.ds(c, MOE_ROWS, stride=ROW_TILE), :] = y[:, c * LANES:(c + 1) * LANES]

    @pl.when(jnp.logical_not(used))
    def _():
        yb_ref[...] = jnp.zeros_like(yb_ref)


def _experts(layer, block_expert, n_used, xb, w_gate, w_up, w_down):
    grid_spec = pltpu.PrefetchScalarGridSpec(
        num_scalar_prefetch=2,
        grid=(N_MOE_BLOCKS,),
        in_specs=[pl.BlockSpec((MOE_ROWS * ROW_TILE, LANES), lambda j, be, nu: (j, 0)),
                  pl.BlockSpec((None, None, D_MODEL, D_FF_EXPERT), lambda j, be, nu: (layer, be[j], 0, 0)),
                  pl.BlockSpec((None, None, D_MODEL, D_FF_EXPERT), lambda j, be, nu: (layer, be[j], 0, 0)),
                  pl.BlockSpec((None, None, D_FF_EXPERT, D_MODEL), lambda j, be, nu: (layer, be[j], 0, 0))],
        out_specs=pl.BlockSpec((MOE_ROWS * ROW_TILE, LANES), lambda j, be, nu: (j, 0)),
        scratch_shapes=[pltpu.VMEM((D_MODEL, D_FF_EXPERT), bf16),
                        pltpu.VMEM((D_MODEL, D_FF_EXPERT), bf16),
                        pltpu.VMEM((D_FF_EXPERT, D_MODEL), bf16)])
    return pl.pallas_call(
        _expert_kernel,
        grid_spec=grid_spec,
        out_shape=jax.ShapeDtypeStruct((N_MOE_ROWS * ROW_TILE, LANES), f32),
        compiler_params=_cparams(("arbitrary",)),
        name="moe_experts",
    )(block_expert, n_used, xb, w_gate, w_up, w_down)


def _combine_kernel(x_ref, gate_ref, dest_hbm, yb_hbm, o_ref, ybuf, idx_smem, idx_sem, row_sem):
    i = pl.program_id(0)
    slot = i % 2

    def idx_copy(tile, s):
        return pltpu.make_async_copy(dest_hbm.at[tile], idx_smem.at[s], idx_sem.at[s])

    def gather(s):
        _issue_rows(idx_smem, s, lambda k, r, d: pltpu.make_async_copy(
            yb_hbm.at[pl.ds(d, ROW_TILE), :], ybuf.at[s, k, pl.ds(r, ROW_TILE), :], row_sem.at[s]))

    @pl.when(i == 0)
    def _():
        idx_copy(0, 0).start()
        idx_copy(0, 0).wait()
        gather(0)
        idx_copy(1, 1).start()

    for s in range(2):
        @pl.when(jnp.logical_and(i + 1 < N_ROW_TILES, slot == 1 - s))
        def _(s=s):
            idx_copy(i + 1, s).wait()
            gather(s)

    @pl.when(i + 2 < N_ROW_TILES)
    def _():
        idx_copy(i + 2, slot).start()

    g0 = gate_ref[:, 0:1]
    g1 = gate_ref[:, 1:2]
    for s in range(2):
        @pl.when(slot == s)
        def _(s=s):
            for k in range(TOP_K):
                pltpu.make_async_copy(yb_hbm.at[pl.ds(0, TM_ROW * ROW_TILE), :], ybuf.at[s, k],
                                      row_sem.at[s]).wait()
            for c in range(ROW_TILE):
                y0 = ybuf[s, 0, pl.ds(c, TM_ROW, stride=ROW_TILE), :]
                y1 = ybuf[s, 1, pl.ds(c, TM_ROW, stride=ROW_TILE), :]
                lanes = slice(c * LANES, (c + 1) * LANES)
                o_ref[:, lanes] = x_ref[:, lanes] + (y0 * g0 + y1 * g1)


def _combine(x, gate_cols, dest_tiles, yb):
    return pl.pallas_call(
        _combine_kernel,
        grid=(N_ROW_TILES,),
        in_specs=[pl.BlockSpec((TM_ROW, D_MODEL), lambda i: (i, 0)),
                  pl.BlockSpec((TM_ROW, TOP_K), lambda i: (i, 0)),
                  pl.BlockSpec(memory_space=pl.ANY),
                  pl.BlockSpec(memory_space=pl.ANY)],
        out_specs=pl.BlockSpec((TM_ROW, D_MODEL), lambda i: (i, 0)),
        out_shape=jax.ShapeDtypeStruct((N_TOK, D_MODEL), f32),
        scratch_shapes=[pltpu.VMEM((2, TOP_K, TM_ROW * ROW_TILE, LANES), f32),
                        pltpu.SMEM((2, IDX_TRIPS, IDX_PER_TRIP), jnp.int32),
                        pltpu.SemaphoreType.DMA((2,)),
                        pltpu.SemaphoreType.DMA((2,))],
        compiler_params=_cparams(("arbitrary",)),
        name="moe_combine",
    )(x, gate_cols, dest_tiles, yb)


def _hier_moe(layer, x, nw, w_group, b_group, w_expert, b_expert, w_gate, w_up, w_down, tri):
    wt = jnp.concatenate([w_group, w_expert], axis=1).T
    wt = jnp.pad(wt, ((0, LANES - wt.shape[0]), (0, 0)))
    wt_hi = wt.astype(bf16)
    wt_lo = (wt - wt_hi.astype(f32)).astype(bf16)
    bias_col = jnp.pad(jnp.concatenate([b_group, b_expert]), (0, LANES - N_GROUPS - N_EXPERTS))[:, None]

    idx, gates, cnt = _router(x, nw, wt_hi, wt_lo, bias_col, tri)
    counts = cnt[:, 0].astype(jnp.int32)
    padded = (counts + MOE_ROWS - 1) // MOE_ROWS * MOE_ROWS
    pends = jnp.cumsum(padded)
    pstarts = pends - padded
    eids = jnp.arange(N_EXPERTS, dtype=jnp.int32)[:, None, None]
    seg_start = jnp.sum(jnp.where(idx[None, 0:2] == eids, pstarts[:, None, None], 0), axis=0)
    dest = seg_start + idx[2:4]
    dest_tiles = _dest_tiles(dest)
    block_row0 = jnp.arange(N_MOE_BLOCKS, dtype=jnp.int32) * MOE_ROWS
    block_expert = jnp.minimum(
        jnp.sum((pends[None, :] <= block_row0[:, None]).astype(jnp.int32), axis=1), N_EXPERTS - 1)
    n_used = (pends[-1:] // MOE_ROWS).astype(jnp.int32)

    xb = _dispatch(x, nw, dest_tiles, jnp.zeros((N_MOE_ROWS * ROW_TILE, LANES), f32))
    yb = _experts(layer, block_expert, n_used, xb, w_gate, w_up, w_down)
    return _combine(x, gates[0:2].T, dest_tiles, yb)


def kernel(x, positions, conv_norm_w, conv_w_in, conv_k, conv_w_out, mla_norm_w, mla_w_down, mla_q_lat_norm, mla_kv_lat_norm, mla_w_q_up, mla_w_kv_up, mla_q_head_norm, mla_k_head_norm, mla_w_out, ffn_norm_w, router_w_group, router_b_group, router_w_expert, router_b_expert, exp_w_gate, exp_w_up, exp_w_down):
    half = QK_ROPE_DIM // 2
    inv_freq = ROPE_THETA ** (-jnp.arange(0, QK_ROPE_DIM, 2, dtype=f32) / QK_ROPE_DIM)
    ang = positions.astype(f32)[..., None] * inv_freq
    cos = jnp.cos(ang)
    sin = jnp.sin(ang)
    cos_t = cos.transpose(0, 2, 1)
    sin_t = sin.transpose(0, 2, 1)
    cs = jnp.concatenate([cos, cos, -sin, sin], axis=-1).reshape(N_TOK, LANES)
    row = lax.broadcasted_iota(jnp.int32, (TM, TM), 0)
    col = lax.broadcasted_iota(jnp.int32, (TM, TM), 1)
    tri = (row < col).astype(bf16)

    xt = x.reshape(N_TOK, D_MODEL)
    for i in range(DEPTH):
        j = i // N_MIXERS
        if i % N_MIXERS == 0:
            z, gb = _conv_in(xt, conv_norm_w[j][None, :], conv_w_in[j].astype(bf16))
            xt = _conv_out(xt, z, gb, conv_k[j], conv_w_out[j].astype(bf16))
        else:
            wd = mla_w_down[j]
            ro = Q_LORA_RANK + KV_LORA_RANK
            wd_aug = jnp.concatenate([wd, wd[:, ro + half:ro + 2 * half], wd[:, ro:ro + half]],
                                     axis=1).astype(bf16)
            wkv = mla_w_kv_up[j].reshape(KV_LORA_RANK, N_HEADS, QK_NOPE_DIM + V_HEAD_DIM)
            wk = wkv[:, :, :QK_NOPE_DIM].reshape(KV_LORA_RANK, N_HEADS * QK_NOPE_DIM).astype(bf16)
            wvt = wkv[:, :, QK_NOPE_DIM:].reshape(KV_LORA_RANK, N_HEADS * V_HEAD_DIM).T.astype(bf16)
            wqt = mla_w_q_up[j].T.astype(bf16)
            khn = mla_k_head_norm[j]
            kw1 = khn[QK_NOPE_DIM:QK_NOPE_DIM + half]
            kw2 = khn[QK_NOPE_DIM + half:]
            g128 = jnp.concatenate([kw1, kw2, kw2, kw1])[None, :]
            qt, k, vt = _mla_proj(xt, mla_norm_w[j][None, :], wd_aug, mla_q_lat_norm[j][None, :],
                                  mla_kv_lat_norm[j][None, :], wqt, wk, wvt,
                                  mla_q_head_norm[j][:, None], khn[None, :], g128, cs, cos_t, sin_t)
            ot = _attention(qt, k, vt)
            xt = _mla_out(xt, ot.reshape(BATCH, N_HEADS * V_HEAD_DIM, SEQ), mla_w_out[j].T.astype(bf16))
        xt = _hier_moe(i, xt, ffn_norm_w[i][None, :], router_w_group[i], router_b_group[i],
                       router_w_expert[i], router_b_expert[i], exp_w_gate, exp_w_up,
                       exp_w_down, tri)
    return xt.reshape(BATCH, SEQ, D_MODEL)
```

```python
import math

import jax
import jax.numpy as jnp
from jax import lax
from jax.experimental import pallas as pl
from jax.experimental.pallas import tpu as pltpu

D_MODEL = 1024
BATCH = 2
SEQ = 16384
DEPTH = 4
N_TOK = BATCH * SEQ
N_MIXERS = 2
N_HEADS = 8
Q_LORA_RANK = 384
KV_LORA_RANK = 256
QK_NOPE_DIM = 128
QK_ROPE_DIM = 64
QK_HEAD_DIM = QK_NOPE_DIM + QK_ROPE_DIM
V_HEAD_DIM = 128
LATENT_DIM = Q_LORA_RANK + KV_LORA_RANK + QK_ROPE_DIM
LATENT_AUG = LATENT_DIM + QK_ROPE_DIM
ROPE_THETA = 10000.0
N_GROUPS = 8
EXPERTS_PER_GROUP = 8
N_EXPERTS = N_GROUPS * EXPERTS_PER_GROUP
TOP_K = 2
D_FF_EXPERT = 384
RMS_EPS = 1e-6

LANES = 128
VMEM_LIMIT = 56 * 1024 * 1024

TM = 512
TM_ROW = 256
ROW_UNROLL = 8
MOE_ROWS = 256
N_MOE_BLOCKS = N_TOK * TOP_K // MOE_ROWS + N_EXPERTS
N_MOE_ROWS = N_MOE_BLOCKS * MOE_ROWS
TQ = 2048
TK = 512
QK_SCALE_LOG2E = (QK_HEAD_DIM ** -0.5) * math.log2(math.e)

bf16 = jnp.bfloat16
f32 = jnp.float32


def _cparams(sem):
    return pltpu.CompilerParams(dimension_semantics=sem, vmem_limit_bytes=VMEM_LIMIT)


def _rms(x, w):
    return x * lax.rsqrt(jnp.mean(x * x, axis=-1, keepdims=True) + RMS_EPS) * w


def _dot(a, b):
    return jnp.dot(a, b, preferred_element_type=f32)


def _dot_nt(a, b):
    return lax.dot_general(a, b, (((1,), (1,)), ((), ())), preferred_element_type=f32)


def _conv_in_kernel(x_ref, nw_ref, w_ref, z_ref, gb_ref):
    h = _rms(x_ref[...], nw_ref[...]).astype(bf16)
    gb = _dot(h, w_ref[:, 0:D_MODEL])
    gc = _dot(h, w_ref[:, D_MODEL:2 * D_MODEL])
    u = _dot(h, w_ref[:, 2 * D_MODEL:3 * D_MODEL])
    z_ref[...] = (gc * u).astype(bf16)
    gb_ref[...] = gb.astype(bf16)


def _conv_in(x, nw, w_in):
    return pl.pallas_call(
        _conv_in_kernel,
        grid=(N_TOK // TM,),
        in_specs=[pl.BlockSpec((TM, D_MODEL), lambda i: (i, 0)),
                  pl.BlockSpec((1, D_MODEL), lambda i: (0, 0)),
                  pl.BlockSpec((D_MODEL, 3 * D_MODEL), lambda i: (0, 0))],
        out_specs=[pl.BlockSpec((TM, D_MODEL), lambda i: (i, 0)),
                   pl.BlockSpec((TM, D_MODEL), lambda i: (i, 0))],
        out_shape=[jax.ShapeDtypeStruct((N_TOK, D_MODEL), bf16),
                   jax.ShapeDtypeStruct((N_TOK, D_MODEL), bf16)],
        compiler_params=_cparams(("arbitrary",)),
        name="conv_in",
    )(x, nw, w_in)


def _conv_out_kernel(x_ref, z_ref, zp_ref, zn_ref, gb_ref, k_ref, w_ref, o_ref):
    i = pl.program_id(0)
    tiles_per_seq = SEQ // TM
    first = (i % tiles_per_seq) == 0
    last = (i % tiles_per_seq) == tiles_per_seq - 1
    z = z_ref[...].astype(f32)
    prev_row = jnp.where(first, 0.0, zp_ref[7:8, :].astype(f32))
    next_row = jnp.where(last, 0.0, zn_ref[0:1, :].astype(f32))
    row = lax.broadcasted_iota(jnp.int32, z.shape, 0)
    z_m1 = jnp.where(row == 0, prev_row, pltpu.roll(z, 1, 0))
    z_p1 = jnp.where(row == TM - 1, next_row, pltpu.roll(z, TM - 1, 0))
    conv = z_m1 * k_ref[0:1, :] + z * k_ref[1:2, :] + z_p1 * k_ref[2:3, :]
    y = _dot((gb_ref[...].astype(f32) * conv).astype(bf16), w_ref[...])
    o_ref[...] = x_ref[...] + y


def _conv_out(x, z, gb, conv_k, w_out):
    n_halo = N_TOK // 8
    per = TM // 8
    return pl.pallas_call(
        _conv_out_kernel,
        grid=(N_TOK // TM,),
        in_specs=[pl.BlockSpec((TM, D_MODEL), lambda i: (i, 0)),
                  pl.BlockSpec((TM, D_MODEL), lambda i: (i, 0)),
                  pl.BlockSpec((8, D_MODEL), lambda i: (jnp.maximum(i * per - 1, 0), 0)),
                  pl.BlockSpec((8, D_MODEL), lambda i: (jnp.minimum((i + 1) * per, n_halo - 1), 0)),
                  pl.BlockSpec((TM, D_MODEL), lambda i: (i, 0)),
                  pl.BlockSpec((3, D_MODEL), lambda i: (0, 0)),
                  pl.BlockSpec((D_MODEL, D_MODEL), lambda i: (0, 0))],
        out_specs=pl.BlockSpec((TM, D_MODEL), lambda i: (i, 0)),
        out_shape=jax.ShapeDtypeStruct((N_TOK, D_MODEL), f32),
        compiler_params=_cparams(("arbitrary",)),
        name="conv_out",
    )(x, z, z, z, gb, conv_k, w_out)


def _mla_proj_kernel(x_ref, nw_ref, wd_ref, qln_ref, kvln_ref, wqt_ref, wk_ref, wvt_ref,
                     qhn_ref, khn_ref, g128_ref, cs_ref, cost_ref, sint_ref,
                     qt_ref, k_ref, vt_ref):
    h = _rms(x_ref[...], nw_ref[...]).astype(bf16)
    lat = _dot(h, wd_ref[...])
    q_lat = _rms(lat[:, 0:Q_LORA_RANK], qln_ref[...]).astype(bf16)
    kv_lat = _rms(lat[:, Q_LORA_RANK:Q_LORA_RANK + KV_LORA_RANK], kvln_ref[...]).astype(bf16)
    kr_blk = lat[:, Q_LORA_RANK + KV_LORA_RANK:LATENT_AUG]
    lane = lax.broadcasted_iota(jnp.int32, kr_blk.shape, 1)
    kr_ss = jnp.sum(jnp.where(lane < QK_ROPE_DIM, kr_blk * kr_blk, 0.0), axis=-1, keepdims=True)
    t = kr_blk * g128_ref[...] * cs_ref[...]
    kr_roped = t + pltpu.roll(t, QK_ROPE_DIM, 1)

    k_nope = _dot(kv_lat, wk_ref[...])
    cos_t = cost_ref[0]
    sin_t = sint_ref[0]
    for hd in range(N_HEADS):
        kn = k_nope[:, hd * QK_NOPE_DIM:(hd + 1) * QK_NOPE_DIM]
        ss = jnp.sum(kn * kn, axis=-1, keepdims=True) + kr_ss
        r = lax.rsqrt(ss * (1.0 / QK_HEAD_DIM) + RMS_EPS)
        k_ref[0, hd, :, 0:QK_NOPE_DIM] = (kn * r * khn_ref[:, 0:QK_NOPE_DIM]).astype(bf16)
        k_ref[0, hd, :, QK_NOPE_DIM:QK_HEAD_DIM] = (kr_roped[:, 0:QK_ROPE_DIM] * r).astype(bf16)

        qt = _dot_nt(wqt_ref[hd * QK_HEAD_DIM:(hd + 1) * QK_HEAD_DIM, :], q_lat)
        rq = lax.rsqrt(jnp.mean(qt * qt, axis=0, keepdims=True) + RMS_EPS)
        qn = qt * rq * qhn_ref[...] * QK_SCALE_LOG2E
        half = QK_ROPE_DIM // 2
        x1 = qn[QK_NOPE_DIM:QK_NOPE_DIM + half, :]
        x2 = qn[QK_NOPE_DIM + half:QK_HEAD_DIM, :]
        qt_ref[0, hd, 0:QK_NOPE_DIM, :] = qn[0:QK_NOPE_DIM, :].astype(bf16)
        qt_ref[0, hd, QK_NOPE_DIM:QK_NOPE_DIM + half, :] = (x1 * cos_t - x2 * sin_t).astype(bf16)
        qt_ref[0, hd, QK_NOPE_DIM + half:QK_HEAD_DIM, :] = (x2 * cos_t + x1 * sin_t).astype(bf16)

        vt = _dot_nt(wvt_ref[hd * V_HEAD_DIM:(hd + 1) * V_HEAD_DIM, :], kv_lat)
        vt_ref[0, hd, :, :] = vt.astype(bf16)


def _mla_proj(x, nw, wd_aug, qln, kvln, wqt, wk, wvt, qhn_col, khn_row, g128, cs, cos_t, sin_t):
    tps = SEQ // TM
    const = lambda i: (0, 0)
    return pl.pallas_call(
        _mla_proj_kernel,
        grid=(N_TOK // TM,),
        in_specs=[pl.BlockSpec((TM, D_MODEL), lambda i: (i, 0)),
                  pl.BlockSpec((1, D_MODEL), const),
                  pl.BlockSpec((D_MODEL, LATENT_AUG), const),
                  pl.BlockSpec((1, Q_LORA_RANK), const),
                  pl.BlockSpec((1, KV_LORA_RANK), const),
                  pl.BlockSpec((N_HEADS * QK_HEAD_DIM, Q_LORA_RANK), const),
                  pl.BlockSpec((KV_LORA_RANK, N_HEADS * QK_NOPE_DIM), const),
                  pl.BlockSpec((N_HEADS * V_HEAD_DIM, KV_LORA_RANK), const),
                  pl.BlockSpec((QK_HEAD_DIM, 1), const),
                  pl.BlockSpec((1, QK_HEAD_DIM), const),
                  pl.BlockSpec((1, LANES), const),
                  pl.BlockSpec((TM, LANES), lambda i: (i, 0)),
                  pl.BlockSpec((1, QK_ROPE_DIM // 2, TM), lambda i: (i // tps, 0, i % tps)),
                  pl.BlockSpec((1, QK_ROPE_DIM // 2, TM), lambda i: (i // tps, 0, i % tps))],
        out_specs=[pl.BlockSpec((1, N_HEADS, QK_HEAD_DIM, TM), lambda i: (i // tps, 0, 0, i % tps)),
                   pl.BlockSpec((1, N_HEADS, TM, QK_HEAD_DIM), lambda i: (i // tps, 0, i % tps, 0)),
                   pl.BlockSpec((1, N_HEADS, V_HEAD_DIM, TM), lambda i: (i // tps, 0, 0, i % tps))],
        out_shape=[jax.ShapeDtypeStruct((BATCH, N_HEADS, QK_HEAD_DIM, SEQ), bf16),
                   jax.ShapeDtypeStruct((BATCH, N_HEADS, SEQ, QK_HEAD_DIM), bf16),
                   jax.ShapeDtypeStruct((BATCH, N_HEADS, V_HEAD_DIM, SEQ), bf16)],
        compiler_params=_cparams(("arbitrary",)),
        name="mla_proj",
    )(x, nw, wd_aug, qln, kvln, wqt, wk, wvt, qhn_col, khn_row, g128, cs, cos_t, sin_t)


def _attn_kernel(qt_ref, k_ref, vt_ref, o_ref, acc_ref, s0_ref, s1_ref, p0_ref, p1_ref):
    qt = qt_ref[0, 0]
    n = SEQ // TK
    acc_ref[...] = jnp.zeros_like(acc_ref)

    def scores(j, s_ref):
        off = pl.multiple_of(j * TK, TK)
        s = _dot(k_ref[0, 0, pl.ds(off, TK), :], qt)
        s_ref[...] = s
        return jnp.max(s, axis=0, keepdims=True)

    def softmax(s_ref, p_ref, m, l, cmax):
        m_new = jnp.maximum(m, cmax)
        alpha = jnp.exp2(m - m_new)
        p = jnp.exp2(s_ref[...] - m_new)
        p_ref[...] = p.astype(bf16)
        return m_new, alpha * l + jnp.sum(p, axis=0, keepdims=True), alpha

    def values(j, p_ref, alpha):
        off = pl.multiple_of(j * TK, TK)
        pv = _dot(vt_ref[0, 0, :, pl.ds(off, TK)], p_ref[...])
        acc_ref[...] = alpha * acc_ref[...] + pv

    m = jnp.full((1, TQ), -jnp.inf, f32)
    l = jnp.zeros((1, TQ), f32)
    c0 = scores(0, s0_ref)
    c1 = scores(1, s1_ref)
    m, l, a0 = softmax(s0_ref, p0_ref, m, l, c0)
    c0 = scores(2, s0_ref)
    m, l, a1 = softmax(s1_ref, p1_ref, m, l, c1)
    values(0, p0_ref, a0)

    def pair(jj, carry):
        m, l, a_prev, c0 = carry
        j = 2 * jj
        c1 = scores(j + 1, s1_ref)
        m, l, a0 = softmax(s0_ref, p0_ref, m, l, c0)
        values(j - 1, p1_ref, a_prev)
        c0 = scores(j + 2, s0_ref)
        m, l, a1 = softmax(s1_ref, p1_ref, m, l, c1)
        values(j, p0_ref, a0)
        return m, l, a1, c0

    m, l, a_prev, c0 = lax.fori_loop(1, n // 2 - 1, pair, (m, l, a1, c0))
    c1 = scores(n - 1, s1_ref)
    m, l, a0 = softmax(s0_ref, p0_ref, m, l, c0)
    values(n - 3, p1_ref, a_prev)
    m, l, a1 = softmax(s1_ref, p1_ref, m, l, c1)
    values(n - 2, p0_ref, a0)
    values(n - 1, p1_ref, a1)
    o_ref[0, 0] = (acc_ref[...] / l).astype(bf16)


def _attention(qt, k, vt):
    return pl.pallas_call(
        _attn_kernel,
        grid=(BATCH, N_HEADS, SEQ // TQ),
        in_specs=[pl.BlockSpec((1, 1, QK_HEAD_DIM, TQ), lambda b, h, i: (b, h, 0, i)),
                  pl.BlockSpec((1, 1, SEQ, QK_HEAD_DIM), lambda b, h, i: (b, h, 0, 0)),
                  pl.BlockSpec((1, 1, V_HEAD_DIM, SEQ), lambda b, h, i: (b, h, 0, 0))],
        out_specs=pl.BlockSpec((1, 1, V_HEAD_DIM, TQ), lambda b, h, i: (b, h, 0, i)),
        out_shape=jax.ShapeDtypeStruct((BATCH, N_HEADS, V_HEAD_DIM, SEQ), bf16),
        scratch_shapes=[pltpu.VMEM((V_HEAD_DIM, TQ), f32),
                        pltpu.VMEM((TK, TQ), f32), pltpu.VMEM((TK, TQ), f32),
                        pltpu.VMEM((TK, TQ), bf16), pltpu.VMEM((TK, TQ), bf16)],
        compiler_params=_cparams(("arbitrary", "arbitrary", "arbitrary")),
        name="mla_attention",
    )(qt, k, vt)


def _mla_out_kernel(x_ref, ot_ref, w_ref, o_ref):
    yt = _dot(w_ref[...], ot_ref[0])
    o_ref[...] = x_ref[...] + yt.T


def _mla_out(x, ot, w_out_t):
    tps = SEQ // TM
    return pl.pallas_call(
        _mla_out_kernel,
        grid=(N_TOK // TM,),
        in_specs=[pl.BlockSpec((TM, D_MODEL), lambda i: (i, 0)),
                  pl.BlockSpec((1, N_HEADS * V_HEAD_DIM, TM), lambda i: (i // tps, 0, i % tps)),
                  pl.BlockSpec((D_MODEL, N_HEADS * V_HEAD_DIM), lambda i: (0, 0))],
        out_specs=pl.BlockSpec((TM, D_MODEL), lambda i: (i, 0)),
        out_shape=jax.ShapeDtypeStruct((N_TOK, D_MODEL), f32),
        compiler_params=_cparams(("arbitrary",)),
        name="mla_out",
    )(x, ot, w_out_t)


def _first_argmax(v, n):
    idx = lax.broadcasted_iota(jnp.int32, v.shape, 0)
    mx = jnp.max(v, axis=0, keepdims=True)
    return jnp.min(jnp.where(v == mx, idx, n), axis=0, keepdims=True)


def _router_kernel(x_ref, nw_ref, wt_hi_ref, wt_lo_ref, bias_ref, tri_ref,
                   idx_ref, gate_ref, cnt_ref, base_ref):
    i = pl.program_id(0)

    @pl.when(i == 0)
    def _():
        base_ref[...] = jnp.zeros_like(base_ref)

    h = _rms(x_ref[...], nw_ref[...])
    h_hi = h.astype(bf16)
    h_lo = (h - h_hi.astype(f32)).astype(bf16)
    logits = (_dot_nt(wt_hi_ref[...], h_hi) + _dot_nt(wt_lo_ref[...], h_hi)
              + _dot_nt(wt_hi_ref[...], h_lo))
    g_logits = logits[0:N_GROUPS, :]
    g_exp = jnp.exp(g_logits - jnp.max(g_logits, axis=0, keepdims=True))
    g_prob = g_exp / jnp.sum(g_exp, axis=0, keepdims=True)
    g_sel = _first_argmax(g_logits + bias_ref[0:N_GROUPS, :], N_GROUPS)
    gidx = lax.broadcasted_iota(jnp.int32, g_logits.shape, 0)
    g_gate = jnp.sum(jnp.where(gidx == g_sel, g_prob, 0.0), axis=0, keepdims=True)

    e_logits = jnp.zeros((EXPERTS_PER_GROUP, TM), f32)
    e_bias = jnp.zeros((EXPERTS_PER_GROUP, TM), f32)
    for g in range(N_GROUPS):
        lo = N_GROUPS + g * EXPERTS_PER_GROUP
        e_logits = jnp.where(g_sel == g, logits[lo:lo + EXPERTS_PER_GROUP, :], e_logits)
        e_bias = jnp.where(g_sel == g, bias_ref[lo:lo + EXPERTS_PER_GROUP, :], e_bias)
    scored = e_logits + e_bias
    eidx = lax.broadcasted_iota(jnp.int32, scored.shape, 0)
    top0 = _first_argmax(scored, EXPERTS_PER_GROUP)
    top1 = _first_argmax(jnp.where(eidx == top0, -jnp.inf, scored), EXPERTS_PER_GROUP)
    e_exp = jnp.exp(e_logits - jnp.max(e_logits, axis=0, keepdims=True))
    e_prob = e_exp / jnp.sum(e_exp, axis=0, keepdims=True)
    p0 = jnp.sum(jnp.where(eidx == top0, e_prob, 0.0), axis=0, keepdims=True)
    p1 = jnp.sum(jnp.where(eidx == top1, e_prob, 0.0), axis=0, keepdims=True)
    psum = p0 + p1
    gate0 = g_gate * (p0 / psum)
    gate1 = g_gate * (p1 / psum)
    eid0 = g_sel * EXPERTS_PER_GROUP + top0
    eid1 = g_sel * EXPERTS_PER_GROUP + top1

    xidx = lax.broadcasted_iota(jnp.int32, (N_EXPERTS, TM), 0)
    oh0 = xidx == eid0
    oh1 = xidx == eid1
    pre0 = _dot(jnp.where(oh0, 1.0, 0.0).astype(bf16), tri_ref[...])
    pre1 = _dot(jnp.where(oh1, 1.0, 0.0).astype(bf16), tri_ref[...])
    tot0 = jnp.sum(jnp.where(oh0, 1.0, 0.0), axis=1, keepdims=True)
    tot1 = jnp.sum(jnp.where(oh1, 1.0, 0.0), axis=1, keepdims=True)
    base = base_ref[:, 0:1]
    rank0 = jnp.sum(jnp.where(oh0, base + pre0, 0.0), axis=0, keepdims=True)
    rank1 = jnp.sum(jnp.where(oh1, base + tot0 + pre1, 0.0), axis=0, keepdims=True)
    new_base = base + tot0 + tot1
    base_ref[...] = jnp.broadcast_to(new_base, base_ref.shape)
    cnt_ref[...] = jnp.broadcast_to(new_base, cnt_ref.shape)

    zi = jnp.zeros((4, TM), jnp.int32)
    idx_ref[...] = jnp.concatenate(
        [eid0, eid1, rank0.astype(jnp.int32), rank1.astype(jnp.int32), zi], axis=0)
    gate_ref[...] = jnp.concatenate([gate0, gate1, jnp.zeros((6, TM), f32)], axis=0)


def _router(x, nw, wt_hi, wt_lo, bias_col, tri):
    const = lambda i: (0, 0)
    return pl.pallas_call(
        _router_kernel,
        grid=(N_TOK // TM,),
        in_specs=[pl.BlockSpec((TM, D_MODEL), lambda i: (i, 0)),
                  pl.BlockSpec((1, D_MODEL), const),
                  pl.BlockSpec((LANES, D_MODEL), const),
                  pl.BlockSpec((LANES, D_MODEL), const),
                  pl.BlockSpec((LANES, 1), const),
                  pl.BlockSpec((TM, TM), const)],
        out_specs=[pl.BlockSpec((8, TM), lambda i: (0, i)),
                   pl.BlockSpec((8, TM), lambda i: (0, i)),
                   pl.BlockSpec((N_EXPERTS, LANES), const)],
        out_shape=[jax.ShapeDtypeStruct((8, N_TOK), jnp.int32),
                   jax.ShapeDtypeStruct((8, N_TOK), f32),
                   jax.ShapeDtypeStruct((N_EXPERTS, LANES), f32)],
        scratch_shapes=[pltpu.VMEM((N_EXPERTS, LANES), f32)],
        compiler_params=_cparams(("arbitrary",)),
        name="moe_router",
    )(x, nw, wt_hi, wt_lo, bias_col, tri)


N_ROW_TILES = N_TOK // TM_ROW
ROW_TILE = 8
IDX_GROUPS = TM_ROW // LANES
IDX_TRIPS = LANES // ROW_UNROLL
IDX_PER_TRIP = IDX_GROUPS * ROW_UNROLL * TOP_K


def _issue_rows(idx_smem, s, make_copy):
    def body(r8, c):
        for g in range(IDX_GROUPS):
            for u in range(ROW_UNROLL):
                for k in range(TOP_K):
                    d = pl.multiple_of(idx_smem[s, r8, (g * ROW_UNROLL + u) * TOP_K + k] * ROW_TILE,
                                       ROW_TILE)
                    r = pl.multiple_of((g * LANES + r8 * ROW_UNROLL + u) * ROW_TILE, ROW_TILE)
                    make_copy(k, r, d).start(priority=k)
        return c
    lax.fori_loop(0, IDX_TRIPS, body, 0)


def _dest_tiles(dest):
    d = dest.reshape(TOP_K, N_ROW_TILES, IDX_GROUPS, IDX_TRIPS, ROW_UNROLL)
    return d.transpose(1, 3, 2, 4, 0).reshape(N_ROW_TILES, IDX_TRIPS, IDX_PER_TRIP)


def _dispatch_kernel(last_ref, x_ref, nw_ref, dest_hbm, xb_hbm, hbuf, zbuf, idx_smem, idx_sem,
                     row_sem, zero_sem):
    i = pl.program_id(0)
    slot = i % 2

    def idx_copy(tile, s):
        return pltpu.make_async_copy(dest_hbm.at[tile], idx_smem.at[s], idx_sem.at[s])

    @pl.when(i == 0)
    def _():
        zbuf[...] = jnp.zeros_like(zbuf)

        def zero_copy(row):
            row0 = pl.multiple_of(row * ROW_TILE, ROW_TILE)
            return pltpu.make_async_copy(zbuf, xb_hbm.at[pl.ds(row0, MOE_ROWS * ROW_TILE), :], zero_sem)

        def start(e, c):
            @pl.when(last_ref[e] >= 0)
            def _():
                zero_copy(last_ref[e]).start()
            return c

        def wait(e, c):
            @pl.when(last_ref[e] >= 0)
            def _():
                zero_copy(last_ref[e]).wait()
            return c

        def start_tail(b, c):
            zero_copy(b * MOE_ROWS).start()
            return c

        def wait_tail(b, c):
            zero_copy(b * MOE_ROWS).wait()
            return c

        n_used = last_ref[N_EXPERTS]
        lax.fori_loop(0, N_EXPERTS, start, 0)
        lax.fori_loop(n_used, N_MOE_BLOCKS, start_tail, 0)
        lax.fori_loop(0, N_EXPERTS, wait, 0)
        lax.fori_loop(n_used, N_MOE_BLOCKS, wait_tail, 0)

    def drain(s):
        for _ in range(TOP_K):
            pltpu.make_async_copy(hbuf.at[s], xb_hbm.at[pl.ds(0, TM_ROW * ROW_TILE), :],
                                  row_sem.at[s]).wait()

    @pl.when(i == 0)
    def _():
        idx_copy(0, 0).start()

    idx_copy(i, slot).wait()

    @pl.when(i + 1 < N_ROW_TILES)
    def _():
        idx_copy(i + 1, 1 - slot).start()

    h = _rms(x_ref[...], nw_ref[...])
    for s in range(2):
        @pl.when(slot == s)
        def _(s=s):
            @pl.when(i >= 2)
            def _():
                drain(s)
            for c in range(ROW_TILE):
                hbuf[s, pl.ds(c, TM_ROW, stride=ROW_TILE), :] = h[:, c * LANES:(c + 1) * LANES]
            _issue_rows(idx_smem, s, lambda k, r, d: pltpu.make_async_copy(
                hbuf.at[s, pl.ds(r, ROW_TILE), :], xb_hbm.at[pl.ds(d, ROW_TILE), :], row_sem.at[s]))

    @pl.when(i == N_ROW_TILES - 1)
    def _():
        drain(0)
        drain(1)


def _dispatch(x, nw, dest_tiles, zero_plan):
    grid_spec = pltpu.PrefetchScalarGridSpec(
        num_scalar_prefetch=1,
        grid=(N_ROW_TILES,),
        in_specs=[pl.BlockSpec((TM_ROW, D_MODEL), lambda i, lb: (i, 0)),
                  pl.BlockSpec((1, D_MODEL), lambda i, lb: (0, 0)),
                  pl.BlockSpec(memory_space=pl.ANY)],
        out_specs=pl.BlockSpec(memory_space=pl.ANY),
        scratch_shapes=[pltpu.VMEM((2, TM_ROW * ROW_TILE, LANES), f32),
                        pltpu.VMEM((MOE_ROWS * ROW_TILE, LANES), f32),
                        pltpu.SMEM((2, IDX_TRIPS, IDX_PER_TRIP), jnp.int32),
                        pltpu.SemaphoreType.DMA((2,)),
                        pltpu.SemaphoreType.DMA((2,)),
                        pltpu.SemaphoreType.DMA])
    return pl.pallas_call(
        _dispatch_kernel,
        grid_spec=grid_spec,
        out_shape=jax.ShapeDtypeStruct((N_MOE_ROWS * ROW_TILE, LANES), f32),
        compiler_params=_cparams(("arbitrary",)),
        name="moe_dispatch",
    )(zero_plan, x, nw, dest_tiles)


def _expert_kernel(be_ref, nused_ref, xb_ref, wg_ref, wu_ref, wd_ref, yb_ref, wg_b, wu_b, wd_b):
    j = pl.program_id(0)
    used = j < nused_ref[0]
    changed = jnp.logical_or(j == 0, be_ref[j] != be_ref[jnp.maximum(j - 1, 0)])

    @pl.when(jnp.logical_and(used, changed))
    def _():
        wg_b[...] = wg_ref[...].astype(bf16)
        wu_b[...] = wu_ref[...].astype(bf16)
        wd_b[...] = wd_ref[...].astype(bf16)

    @pl.when(used)
    def _():
        xb = jnp.concatenate([xb_ref[pl.ds(c, MOE_ROWS, stride=ROW_TILE), :] for c in range(ROW_TILE)],
                             axis=1).astype(bf16)
        hid = jax.nn.silu(_dot(xb, wg_b[...])) * _dot(xb, wu_b[...])
        y = _dot(hid.astype(bf16), wd_b[...])
        for c in range(ROW_TILE):
            yb_ref[pl.ds(c, MOE_ROWS, stride=ROW_TILE), :] = y[:, c * LANES:(c + 1) * LANES]

    @pl.when(jnp.logical_not(used))
    def _():
        yb_ref[...] = jnp.zeros_like(yb_ref)


def _experts(layer, block_expert, n_used, xb, w_gate, w_up, w_down):
    grid_spec = pltpu.PrefetchScalarGridSpec(
        num_scalar_prefetch=2,
        grid=(N_MOE_BLOCKS,),
        in_specs=[pl.BlockSpec((MOE_ROWS * ROW_TILE, LANES),
                               lambda j, be, nu: (jnp.minimum(j, nu[0] - 1), 0)),
                  pl.BlockSpec((None, None, D_MODEL, D_FF_EXPERT), lambda j, be, nu: (layer, be[j], 0, 0)),
                  pl.BlockSpec((None, None, D_MODEL, D_FF_EXPERT), lambda j, be, nu: (layer, be[j], 0, 0)),
                  pl.BlockSpec((None, None, D_FF_EXPERT, D_MODEL), lambda j, be, nu: (layer, be[j], 0, 0))],
        out_specs=pl.BlockSpec((MOE_ROWS * ROW_TILE, LANES), lambda j, be, nu: (j, 0)),
        scratch_shapes=[pltpu.VMEM((D_MODEL, D_FF_EXPERT), bf16),
                        pltpu.VMEM((D_MODEL, D_FF_EXPERT), bf16),
                        pltpu.VMEM((D_FF_EXPERT, D_MODEL), bf16)])
    return pl.pallas_call(
        _expert_kernel,
        grid_spec=grid_spec,
        out_shape=jax.ShapeDtypeStruct((N_MOE_ROWS * ROW_TILE, LANES), f32),
        compiler_params=_cparams(("arbitrary",)),
        name="moe_experts",
    )(block_expert, n_used, xb, w_gate, w_up, w_down)


def _combine_kernel(x_ref, gate_ref, dest_hbm, yb_hbm, o_ref, ybuf, idx_smem, idx_sem, row_sem):
    i = pl.program_id(0)
    slot = i % 2

    def idx_copy(tile, s):
        return pltpu.make_async_copy(dest_hbm.at[tile], idx_smem.at[s], idx_sem.at[s])

    def gather(s):
        _issue_rows(idx_smem, s, lambda k, r, d: pltpu.make_async_copy(
            yb_hbm.at[pl.ds(d, ROW_TILE), :], ybuf.at[s, k, pl.ds(r, ROW_TILE), :], row_sem.at[s]))

    @pl.when(i == 0)
    def _():
        idx_copy(0, 0).start()
        idx_copy(0, 0).wait()
        gather(0)
        idx_copy(1, 1).start()

    for s in range(2):
        @pl.when(jnp.logical_and(i + 1 < N_ROW_TILES, slot == 1 - s))
        def _(s=s):
            idx_copy(i + 1, s).wait()
            gather(s)

    @pl.when(i + 2 < N_ROW_TILES)
    def _():
        idx_copy(i + 2, slot).start()

    g0 = gate_ref[:, 0:1]
    g1 = gate_ref[:, 1:2]
    for s in range(2):
        @pl.when(slot == s)
        def _(s=s):
            for k in range(TOP_K):
                pltpu.make_async_copy(yb_hbm.at[pl.ds(0, TM_ROW * ROW_TILE), :], ybuf.at[s, k],
                                      row_sem.at[s]).wait()
            for c in range(ROW_TILE):
                y0 = ybuf[s, 0, pl.ds(c, TM_ROW, stride=ROW_TILE), :]
                y1 = ybuf[s, 1, pl.ds(c, TM_ROW, stride=ROW_TILE), :]
                lanes = slice(c * LANES, (c + 1) * LANES)
                o_ref[:, lanes] = x_ref[:, lanes] + (y0 * g0 + y1 * g1)


def _combine(x, gate_cols, dest_tiles, yb):
    return pl.pallas_call(
        _combine_kernel,
        grid=(N_ROW_TILES,),
        in_specs=[pl.BlockSpec((TM_ROW, D_MODEL), lambda i: (i, 0)),
                  pl.BlockSpec((TM_ROW, TOP_K), lambda i: (i, 0)),
                  pl.BlockSpec(memory_space=pl.ANY),
                  pl.BlockSpec(memory_space=pl.ANY)],
        out_specs=pl.BlockSpec((TM_ROW, D_MODEL), lambda i: (i, 0)),
        out_shape=jax.ShapeDtypeStruct((N_TOK, D_MODEL), f32),
        scratch_shapes=[pltpu.VMEM((2, TOP_K, TM_ROW * ROW_TILE, LANES), f32),
                        pltpu.SMEM((2, IDX_TRIPS, IDX_PER_TRIP), jnp.int32),
                        pltpu.SemaphoreType.DMA((2,)),
                        pltpu.SemaphoreType.DMA((2,))],
        compiler_params=_cparams(("arbitrary",)),
        name="moe_combine",
    )(x, gate_cols, dest_tiles, yb)


def _hier_moe(layer, x, nw, w_group, b_group, w_expert, b_expert, w_gate, w_up, w_down, tri):
    wt = jnp.concatenate([w_group, w_expert], axis=1).T
    wt = jnp.pad(wt, ((0, LANES - wt.shape[0]), (0, 0)))
    wt_hi = wt.astype(bf16)
    wt_lo = (wt - wt_hi.astype(f32)).astype(bf16)
    bias_col = jnp.pad(jnp.concatenate([b_group, b_expert]), (0, LANES - N_GROUPS - N_EXPERTS))[:, None]

    idx, gates, cnt = _router(x, nw, wt_hi, wt_lo, bias_col, tri)
    counts = cnt[:, 0].astype(jnp.int32)
    padded = (counts + MOE_ROWS - 1) // MOE_ROWS * MOE_ROWS
    pends = jnp.cumsum(padded)
    pstarts = pends - padded
    eids = jnp.arange(N_EXPERTS, dtype=jnp.int32)[:, None, None]
    seg_start = jnp.sum(jnp.where(idx[None, 0:2] == eids, pstarts[:, None, None], 0), axis=0)
    dest = seg_start + idx[2:4]
    dest_tiles = _dest_tiles(dest)
    block_row0 = jnp.arange(N_MOE_BLOCKS, dtype=jnp.int32) * MOE_ROWS
    block_expert = jnp.minimum(
        jnp.sum((pends[None, :] <= block_row0[:, None]).astype(jnp.int32), axis=1), N_EXPERTS - 1)
    n_used = (pends[-1:] // MOE_ROWS).astype(jnp.int32)

    zero_plan = jnp.concatenate([jnp.where(padded > 0, pends - MOE_ROWS, -1), n_used]).astype(jnp.int32)
    xb = _dispatch(x, nw, dest_tiles, zero_plan)
    yb = _experts(layer, block_expert, n_used, xb, w_gate, w_up, w_down)
    return _combine(x, gates[0:2].T, dest_tiles, yb)


def kernel(x, positions, conv_norm_w, conv_w_in, conv_k, conv_w_out, mla_norm_w, mla_w_down, mla_q_lat_norm, mla_kv_lat_norm, mla_w_q_up, mla_w_kv_up, mla_q_head_norm, mla_k_head_norm, mla_w_out, ffn_norm_w, router_w_group, router_b_group, router_w_expert, router_b_expert, exp_w_gate, exp_w_up, exp_w_down):
    half = QK_ROPE_DIM // 2
    inv_freq = ROPE_THETA ** (-jnp.arange(0, QK_ROPE_DIM, 2, dtype=f32) / QK_ROPE_DIM)
    ang = positions.astype(f32)[..., None] * inv_freq
    cos = jnp.cos(ang)
    sin = jnp.sin(ang)
    cos_t = cos.transpose(0, 2, 1)
    sin_t = sin.transpose(0, 2, 1)
    cs = jnp.concatenate([cos, cos, -sin, sin], axis=-1).reshape(N_TOK, LANES)
    row = lax.broadcasted_iota(jnp.int32, (TM, TM), 0)
    col = lax.broadcasted_iota(jnp.int32, (TM, TM), 1)
    tri = (row < col).astype(bf16)

    xt = x.reshape(N_TOK, D_MODEL)
    for i in range(DEPTH):
        j = i // N_MIXERS
        if i % N_MIXERS == 0:
            z, gb = _conv_in(xt, conv_norm_w[j][None, :], conv_w_in[j].astype(bf16))
            xt = _conv_out(xt, z, gb, conv_k[j], conv_w_out[j].astype(bf16))
        else:
            wd = mla_w_down[j]
            ro = Q_LORA_RANK + KV_LORA_RANK
            wd_aug = jnp.concatenate([wd, wd[:, ro + half:ro + 2 * half], wd[:, ro:ro + half]],
                                     axis=1).astype(bf16)
            wkv = mla_w_kv_up[j].reshape(KV_LORA_RANK, N_HEADS, QK_NOPE_DIM + V_HEAD_DIM)
            wk = wkv[:, :, :QK_NOPE_DIM].reshape(KV_LORA_RANK, N_HEADS * QK_NOPE_DIM).astype(bf16)
            wvt = wkv[:, :, QK_NOPE_DIM:].reshape(KV_LORA_RANK, N_HEADS * V_HEAD_DIM).T.astype(bf16)
            wqt = mla_w_q_up[j].T.astype(bf16)
            khn = mla_k_head_norm[j]
            kw1 = khn[QK_NOPE_DIM:QK_NOPE_DIM + half]
            kw2 = khn[QK_NOPE_DIM + half:]
            g128 = jnp.concatenate([kw1, kw2, kw2, kw1])[None, :]
            qt, k, vt = _mla_proj(xt, mla_norm_w[j][None, :], wd_aug, mla_q_lat_norm[j][None, :],
                                  mla_kv_lat_norm[j][None, :], wqt, wk, wvt,
                                  mla_q_head_norm[j][:, None], khn[None, :], g128, cs, cos_t, sin_t)
            ot = _attention(qt, k, vt)
            xt = _mla_out(xt, ot.reshape(BATCH, N_HEADS * V_HEAD_DIM, SEQ), mla_w_out[j].T.astype(bf16))
        xt = _hier_moe(i, xt, ffn_norm_w[i][None, :], router_w_group[i], router_b_group[i],
                       router_w_expert[i], router_b_expert[i], exp_w_gate, exp_w_up,
                       exp_w_down, tri)
    return xt.reshape(BATCH, SEQ, D_MODEL)
```

```python
import math

import jax
import jax.numpy as jnp
from jax import lax
from jax.experimental import pallas as pl
from jax.experimental.pallas import tpu as pltpu

D_MODEL = 1024
BATCH = 2
SEQ = 16384
DEPTH = 4
N_TOK = BATCH * SEQ
N_MIXERS = 2
N_HEADS = 8
Q_LORA_RANK = 384
KV_LORA_RANK = 256
QK_NOPE_DIM = 128
QK_ROPE_DIM = 64
QK_HEAD_DIM = QK_NOPE_DIM + QK_ROPE_DIM
V_HEAD_DIM = 128
V_ROWS = V_HEAD_DIM + 16
LATENT_DIM = Q_LORA_RANK + KV_LORA_RANK + QK_ROPE_DIM
LATENT_AUG = LATENT_DIM + QK_ROPE_DIM
ROPE_THETA = 10000.0
N_GROUPS = 8
EXPERTS_PER_GROUP = 8
N_EXPERTS = N_GROUPS * EXPERTS_PER_GROUP
TOP_K = 2
D_FF_EXPERT = 384
RMS_EPS = 1e-6

LANES = 128
VMEM_LIMIT = 56 * 1024 * 1024

TM = 512
TM_ROW = 256
ROW_UNROLL = 8
MOE_ROWS = 256
N_MOE_BLOCKS = N_TOK * TOP_K // MOE_ROWS + N_EXPERTS
N_MOE_ROWS = N_MOE_BLOCKS * MOE_ROWS
TQ = 2048
TK = 512
QK_SCALE_LOG2E = (QK_HEAD_DIM ** -0.5) * math.log2(math.e)

bf16 = jnp.bfloat16
f32 = jnp.float32


def _cparams(sem):
    return pltpu.CompilerParams(dimension_semantics=sem, vmem_limit_bytes=VMEM_LIMIT)


def _rms(x, w):
    return x * lax.rsqrt(jnp.mean(x * x, axis=-1, keepdims=True) + RMS_EPS) * w


def _dot(a, b):
    return jnp.dot(a, b, preferred_element_type=f32)


def _dot_nt(a, b):
    return lax.dot_general(a, b, (((1,), (1,)), ((), ())), preferred_element_type=f32)


def _conv_in_kernel(x_ref, nw_ref, w_ref, z_ref, gb_ref):
    h = _rms(x_ref[...], nw_ref[...]).astype(bf16)
    gb = _dot(h, w_ref[:, 0:D_MODEL])
    gc = _dot(h, w_ref[:, D_MODEL:2 * D_MODEL])
    u = _dot(h, w_ref[:, 2 * D_MODEL:3 * D_MODEL])
    z_ref[...] = (gc * u).astype(bf16)
    gb_ref[...] = gb.astype(bf16)


def _conv_in(x, nw, w_in):
    return pl.pallas_call(
        _conv_in_kernel,
        grid=(N_TOK // TM,),
        in_specs=[pl.BlockSpec((TM, D_MODEL), lambda i: (i, 0)),
                  pl.BlockSpec((1, D_MODEL), lambda i: (0, 0)),
                  pl.BlockSpec((D_MODEL, 3 * D_MODEL), lambda i: (0, 0))],
        out_specs=[pl.BlockSpec((TM, D_MODEL), lambda i: (i, 0)),
                   pl.BlockSpec((TM, D_MODEL), lambda i: (i, 0))],
        out_shape=[jax.ShapeDtypeStruct((N_TOK, D_MODEL), bf16),
                   jax.ShapeDtypeStruct((N_TOK, D_MODEL), bf16)],
        compiler_params=_cparams(("arbitrary",)),
        name="conv_in",
    )(x, nw, w_in)


def _conv_out_kernel(x_ref, z_ref, zp_ref, zn_ref, gb_ref, k_ref, w_ref, *route_refs):
    o_ref = route_refs[len(ROUTE_IN_SPECS)]
    i = pl.program_id(0)
    tiles_per_seq = SEQ // TM
    first = (i % tiles_per_seq) == 0
    last = (i % tiles_per_seq) == tiles_per_seq - 1
    z = z_ref[...].astype(f32)
    prev_row = jnp.where(first, 0.0, zp_ref[7:8, :].astype(f32))
    next_row = jnp.where(last, 0.0, zn_ref[0:1, :].astype(f32))
    row = lax.broadcasted_iota(jnp.int32, z.shape, 0)
    z_m1 = jnp.where(row == 0, prev_row, pltpu.roll(z, 1, 0))
    z_p1 = jnp.where(row == TM - 1, next_row, pltpu.roll(z, TM - 1, 0))
    conv = z_m1 * k_ref[0:1, :] + z * k_ref[1:2, :] + z_p1 * k_ref[2:3, :]
    y = _dot((gb_ref[...].astype(f32) * conv).astype(bf16), w_ref[...])
    x_new = x_ref[...] + y
    o_ref[...] = x_new
    _route_tile(x_new, *route_refs[:len(ROUTE_IN_SPECS)], *route_refs[len(ROUTE_IN_SPECS) + 1:])


def _conv_out(x, z, gb, conv_k, w_out, route_ops):
    n_halo = N_TOK // 8
    per = TM // 8
    return pl.pallas_call(
        _conv_out_kernel,
        grid=(N_TOK // TM,),
        in_specs=[pl.BlockSpec((TM, D_MODEL), lambda i: (i, 0)),
                  pl.BlockSpec((TM, D_MODEL), lambda i: (i, 0)),
                  pl.BlockSpec((8, D_MODEL), lambda i: (jnp.maximum(i * per - 1, 0), 0)),
                  pl.BlockSpec((8, D_MODEL), lambda i: (jnp.minimum((i + 1) * per, n_halo - 1), 0)),
                  pl.BlockSpec((TM, D_MODEL), lambda i: (i, 0)),
                  pl.BlockSpec((3, D_MODEL), lambda i: (0, 0)),
                  pl.BlockSpec((D_MODEL, D_MODEL), lambda i: (0, 0))] + ROUTE_IN_SPECS,
        out_specs=[pl.BlockSpec((TM, D_MODEL), lambda i: (i, 0))] + ROUTE_OUT_SPECS,
        out_shape=[jax.ShapeDtypeStruct((N_TOK, D_MODEL), f32)] + ROUTE_OUT_SHAPES,
        scratch_shapes=ROUTE_SCRATCH,
        compiler_params=_cparams(("arbitrary",)),
        name="conv_out",
    )(x, z, z, z, gb, conv_k, w_out, *route_ops)


def _mla_proj_kernel(x_ref, nw_ref, wd_ref, qln_ref, kvln_ref, wqt_ref, wk_ref, wvt_ref,
                     qhn_ref, khn_ref, g128_ref, cs_ref, cost_ref, sint_ref,
                     qt_ref, k_ref, vt_ref):
    h = _rms(x_ref[...], nw_ref[...]).astype(bf16)
    lat = _dot(h, wd_ref[...])
    q_lat = _rms(lat[:, 0:Q_LORA_RANK], qln_ref[...]).astype(bf16)
    kv_lat = _rms(lat[:, Q_LORA_RANK:Q_LORA_RANK + KV_LORA_RANK], kvln_ref[...]).astype(bf16)
    kr_blk = lat[:, Q_LORA_RANK + KV_LORA_RANK:LATENT_AUG]
    lane = lax.broadcasted_iota(jnp.int32, kr_blk.shape, 1)
    kr_ss = jnp.sum(jnp.where(lane < QK_ROPE_DIM, kr_blk * kr_blk, 0.0), axis=-1, keepdims=True)
    t = kr_blk * g128_ref[...] * cs_ref[...]
    kr_roped = t + pltpu.roll(t, QK_ROPE_DIM, 1)

    k_nope = _dot(kv_lat, wk_ref[...])
    cos_t = cost_ref[0]
    sin_t = sint_ref[0]
    for hd in range(N_HEADS):
        kn = k_nope[:, hd * QK_NOPE_DIM:(hd + 1) * QK_NOPE_DIM]
        ss = jnp.sum(kn * kn, axis=-1, keepdims=True) + kr_ss
        r = lax.rsqrt(ss * (1.0 / QK_HEAD_DIM) + RMS_EPS)
        k_ref[0, hd, :, 0:QK_NOPE_DIM] = (kn * r * khn_ref[:, 0:QK_NOPE_DIM]).astype(bf16)
        k_ref[0, hd, :, QK_NOPE_DIM:QK_HEAD_DIM] = (kr_roped[:, 0:QK_ROPE_DIM] * r).astype(bf16)

        qt = _dot_nt(wqt_ref[hd * QK_HEAD_DIM:(hd + 1) * QK_HEAD_DIM, :], q_lat)
        rq = lax.rsqrt(jnp.mean(qt * qt, axis=0, keepdims=True) + RMS_EPS)
        qn = qt * rq * qhn_ref[...] * QK_SCALE_LOG2E
        half = QK_ROPE_DIM // 2
        x1 = qn[QK_NOPE_DIM:QK_NOPE_DIM + half, :]
        x2 = qn[QK_NOPE_DIM + half:QK_HEAD_DIM, :]
        qt_ref[0, hd, 0:QK_NOPE_DIM, :] = qn[0:QK_NOPE_DIM, :].astype(bf16)
        qt_ref[0, hd, QK_NOPE_DIM:QK_NOPE_DIM + half, :] = (x1 * cos_t - x2 * sin_t).astype(bf16)
        qt_ref[0, hd, QK_NOPE_DIM + half:QK_HEAD_DIM, :] = (x2 * cos_t + x1 * sin_t).astype(bf16)

        vt = _dot_nt(wvt_ref[hd * V_HEAD_DIM:(hd + 1) * V_HEAD_DIM, :], kv_lat)
        vt_ref[0, hd, 0:V_HEAD_DIM, :] = vt.astype(bf16)
        vt_ref[0, hd, V_HEAD_DIM:V_ROWS, :] = jnp.ones((V_ROWS - V_HEAD_DIM, TM), bf16)


def _mla_proj(x, nw, wd_aug, qln, kvln, wqt, wk, wvt, qhn_col, khn_row, g128, cs, cos_t, sin_t):
    tps = SEQ // TM
    const = lambda i: (0, 0)
    return pl.pallas_call(
        _mla_proj_kernel,
        grid=(N_TOK // TM,),
        in_specs=[pl.BlockSpec((TM, D_MODEL), lambda i: (i, 0)),
                  pl.BlockSpec((1, D_MODEL), const),
                  pl.BlockSpec((D_MODEL, LATENT_AUG), const),
                  pl.BlockSpec((1, Q_LORA_RANK), const),
                  pl.BlockSpec((1, KV_LORA_RANK), const),
                  pl.BlockSpec((N_HEADS * QK_HEAD_DIM, Q_LORA_RANK), const),
                  pl.BlockSpec((KV_LORA_RANK, N_HEADS * QK_NOPE_DIM), const),
                  pl.BlockSpec((N_HEADS * V_HEAD_DIM, KV_LORA_RANK), const),
                  pl.BlockSpec((QK_HEAD_DIM, 1), const),
                  pl.BlockSpec((1, QK_HEAD_DIM), const),
                  pl.BlockSpec((1, LANES), const),
                  pl.BlockSpec((TM, LANES), lambda i: (i, 0)),
                  pl.BlockSpec((1, QK_ROPE_DIM // 2, TM), lambda i: (i // tps, 0, i % tps)),
                  pl.BlockSpec((1, QK_ROPE_DIM // 2, TM), lambda i: (i // tps, 0, i % tps))],
        out_specs=[pl.BlockSpec((1, N_HEADS, QK_HEAD_DIM, TM), lambda i: (i // tps, 0, 0, i % tps)),
                   pl.BlockSpec((1, N_HEADS, TM, QK_HEAD_DIM), lambda i: (i // tps, 0, i % tps, 0)),
                   pl.BlockSpec((1, N_HEADS, V_ROWS, TM), lambda i: (i // tps, 0, 0, i % tps))],
        out_shape=[jax.ShapeDtypeStruct((BATCH, N_HEADS, QK_HEAD_DIM, SEQ), bf16),
                   jax.ShapeDtypeStruct((BATCH, N_HEADS, SEQ, QK_HEAD_DIM), bf16),
                   jax.ShapeDtypeStruct((BATCH, N_HEADS, V_ROWS, SEQ), bf16)],
        compiler_params=_cparams(("arbitrary",)),
        name="mla_proj",
    )(x, nw, wd_aug, qln, kvln, wqt, wk, wvt, qhn_col, khn_row, g128, cs, cos_t, sin_t)


def _attn_kernel(qt_ref, k_ref, vt_ref, o_ref, acc_ref, s0_ref, s1_ref, p0_ref, p1_ref):
    qt = qt_ref[0, 0]
    n = SEQ // TK
    acc_ref[...] = jnp.zeros_like(acc_ref)

    def scores(j, s_ref):
        off = pl.multiple_of(j * TK, TK)
        s = _dot(k_ref[0, 0, pl.ds(off, TK), :], qt)
        s_ref[...] = s
        return jnp.max(s, axis=0, keepdims=True)

    def softmax(s_ref, p_ref, m, cmax):
        m_new = jnp.maximum(m, cmax)
        alpha = jnp.exp2(m - m_new)
        p_ref[...] = jnp.exp2((s_ref[...] - m_new).astype(bf16))
        return m_new, alpha

    def values(j, p_ref, alpha):
        off = pl.multiple_of(j * TK, TK)
        pv = _dot(vt_ref[0, 0, :, pl.ds(off, TK)], p_ref[...])
        acc_ref[...] = alpha * acc_ref[...] + pv

    m = jnp.full((1, TQ), -jnp.inf, f32)
    c0 = scores(0, s0_ref)
    c1 = scores(1, s1_ref)
    m, a0 = softmax(s0_ref, p0_ref, m, c0)
    c0 = scores(2, s0_ref)
    m, a1 = softmax(s1_ref, p1_ref, m, c1)
    values(0, p0_ref, a0)

    def pair(jj, carry):
        m, a_prev, c0 = carry
        j = 2 * jj
        c1 = scores(j + 1, s1_ref)
        m, a0 = softmax(s0_ref, p0_ref, m, c0)
        values(j - 1, p1_ref, a_prev)
        c0 = scores(j + 2, s0_ref)
        m, a1 = softmax(s1_ref, p1_ref, m, c1)
        values(j, p0_ref, a0)
        return m, a1, c0

    m, a_prev, c0 = lax.fori_loop(1, n // 2 - 1, pair, (m, a1, c0))
    c1 = scores(n - 1, s1_ref)
    m, a0 = softmax(s0_ref, p0_ref, m, c0)
    values(n - 3, p1_ref, a_prev)
    m, a1 = softmax(s1_ref, p1_ref, m, c1)
    values(n - 2, p0_ref, a0)
    values(n - 1, p1_ref, a1)
    o_ref[0, 0] = (acc_ref[0:V_HEAD_DIM, :] / acc_ref[V_HEAD_DIM:V_HEAD_DIM + 1, :]).astype(bf16)


def _attention(qt, k, vt):
    return pl.pallas_call(
        _attn_kernel,
        grid=(BATCH, N_HEADS, SEQ // TQ),
        in_specs=[pl.BlockSpec((1, 1, QK_HEAD_DIM, TQ), lambda b, h, i: (b, h, 0, i)),
                  pl.BlockSpec((1, 1, SEQ, QK_HEAD_DIM), lambda b, h, i: (b, h, 0, 0)),
                  pl.BlockSpec((1, 1, V_ROWS, SEQ), lambda b, h, i: (b, h, 0, 0))],
        out_specs=pl.BlockSpec((1, 1, V_HEAD_DIM, TQ), lambda b, h, i: (b, h, 0, i)),
        out_shape=jax.ShapeDtypeStruct((BATCH, N_HEADS, V_HEAD_DIM, SEQ), bf16),
        scratch_shapes=[pltpu.VMEM((V_ROWS, TQ), f32),
                        pltpu.VMEM((TK, TQ), f32), pltpu.VMEM((TK, TQ), f32),
                        pltpu.VMEM((TK, TQ), bf16), pltpu.VMEM((TK, TQ), bf16)],
        compiler_params=_cparams(("arbitrary", "arbitrary", "arbitrary")),
        name="mla_attention",
    )(qt, k, vt)


def _mla_out_kernel(x_ref, ot_ref, w_ref, *route_refs):
    o_ref = route_refs[len(ROUTE_IN_SPECS)]
    yt = _dot(w_ref[...], ot_ref[0])
    x_new = x_ref[...] + yt.T
    o_ref[...] = x_new
    _route_tile(x_new, *route_refs[:len(ROUTE_IN_SPECS)], *route_refs[len(ROUTE_IN_SPECS) + 1:])


def _mla_out(x, ot, w_out_t, route_ops):
    tps = SEQ // TM
    return pl.pallas_call(
        _mla_out_kernel,
        grid=(N_TOK // TM,),
        in_specs=[pl.BlockSpec((TM, D_MODEL), lambda i: (i, 0)),
                  pl.BlockSpec((1, N_HEADS * V_HEAD_DIM, TM), lambda i: (i // tps, 0, i % tps)),
                  pl.BlockSpec((D_MODEL, N_HEADS * V_HEAD_DIM), lambda i: (0, 0))] + ROUTE_IN_SPECS,
        out_specs=[pl.BlockSpec((TM, D_MODEL), lambda i: (i, 0))] + ROUTE_OUT_SPECS,
        out_shape=[jax.ShapeDtypeStruct((N_TOK, D_MODEL), f32)] + ROUTE_OUT_SHAPES,
        scratch_shapes=ROUTE_SCRATCH,
        compiler_params=_cparams(("arbitrary",)),
        name="mla_out",
    )(x, ot, w_out_t, *route_ops)


def _first_argmax(v, n):
    idx = lax.broadcasted_iota(jnp.int32, v.shape, 0)
    mx = jnp.max(v, axis=0, keepdims=True)
    return jnp.min(jnp.where(v == mx, idx, n), axis=0, keepdims=True)


def _route_tile(x, nw_ref, wt_hi_ref, wt_lo_ref, bias_ref, tri_ref,
                idx_ref, gate_ref, cnt_ref, base_ref):
    i = pl.program_id(0)

    @pl.when(i == 0)
    def _():
        base_ref[...] = jnp.zeros_like(base_ref)

    h = _rms(x, nw_ref[...])
    h_hi = h.astype(bf16)
    h_lo = (h - h_hi.astype(f32)).astype(bf16)
    logits = (_dot_nt(wt_hi_ref[...], h_hi) + _dot_nt(wt_lo_ref[...], h_hi)
              + _dot_nt(wt_hi_ref[...], h_lo))
    g_logits = logits[0:N_GROUPS, :]
    g_exp = jnp.exp(g_logits - jnp.max(g_logits, axis=0, keepdims=True))
    g_prob = g_exp / jnp.sum(g_exp, axis=0, keepdims=True)
    g_sel = _first_argmax(g_logits + bias_ref[0:N_GROUPS, :], N_GROUPS)
    gidx = lax.broadcasted_iota(jnp.int32, g_logits.shape, 0)
    g_gate = jnp.sum(jnp.where(gidx == g_sel, g_prob, 0.0), axis=0, keepdims=True)

    e_logits = jnp.zeros((EXPERTS_PER_GROUP, TM), f32)
    e_bias = jnp.zeros((EXPERTS_PER_GROUP, TM), f32)
    for g in range(N_GROUPS):
        lo = N_GROUPS + g * EXPERTS_PER_GROUP
        e_logits = jnp.where(g_sel == g, logits[lo:lo + EXPERTS_PER_GROUP, :], e_logits)
        e_bias = jnp.where(g_sel == g, bias_ref[lo:lo + EXPERTS_PER_GROUP, :], e_bias)
    scored = e_logits + e_bias
    eidx = lax.broadcasted_iota(jnp.int32, scored.shape, 0)
    top0 = _first_argmax(scored, EXPERTS_PER_GROUP)
    top1 = _first_argmax(jnp.where(eidx == top0, -jnp.inf, scored), EXPERTS_PER_GROUP)
    e_exp = jnp.exp(e_logits - jnp.max(e_logits, axis=0, keepdims=True))
    e_prob = e_exp / jnp.sum(e_exp, axis=0, keepdims=True)
    p0 = jnp.sum(jnp.where(eidx == top0, e_prob, 0.0), axis=0, keepdims=True)
    p1 = jnp.sum(jnp.where(eidx == top1, e_prob, 0.0), axis=0, keepdims=True)
    psum = p0 + p1
    gate0 = g_gate * (p0 / psum)
    gate1 = g_gate * (p1 / psum)
    eid0 = g_sel * EXPERTS_PER_GROUP + top0
    eid1 = g_sel * EXPERTS_PER_GROUP + top1

    xidx = lax.broadcasted_iota(jnp.int32, (N_EXPERTS, TM), 0)
    oh0 = xidx == eid0
    oh1 = xidx == eid1
    pre0 = _dot(jnp.where(oh0, 1.0, 0.0).astype(bf16), tri_ref[...])
    pre1 = _dot(jnp.where(oh1, 1.0, 0.0).astype(bf16), tri_ref[...])
    tot0 = jnp.sum(jnp.where(oh0, 1.0, 0.0), axis=1, keepdims=True)
    tot1 = jnp.sum(jnp.where(oh1, 1.0, 0.0), axis=1, keepdims=True)
    base = base_ref[:, 0:1]
    rank0 = jnp.sum(jnp.where(oh0, base + pre0, 0.0), axis=0, keepdims=True)
    rank1 = jnp.sum(jnp.where(oh1, base + tot0 + pre1, 0.0), axis=0, keepdims=True)
    new_base = base + tot0 + tot1
    base_ref[...] = jnp.broadcast_to(new_base, base_ref.shape)
    cnt_ref[...] = jnp.broadcast_to(new_base, cnt_ref.shape)

    zi = jnp.zeros((4, TM), jnp.int32)
    idx_ref[...] = jnp.concatenate(
        [eid0, eid1, rank0.astype(jnp.int32), rank1.astype(jnp.int32), zi], axis=0)
    gate_ref[...] = jnp.concatenate([gate0, gate1, jnp.zeros((6, TM), f32)], axis=0)


def _const2(i):
    return (0, 0)


ROUTE_IN_SPECS = [pl.BlockSpec((1, D_MODEL), _const2),
                  pl.BlockSpec((LANES, D_MODEL), _const2),
                  pl.BlockSpec((LANES, D_MODEL), _const2),
                  pl.BlockSpec((LANES, 1), _const2),
                  pl.BlockSpec((TM, TM), _const2)]
ROUTE_OUT_SPECS = [pl.BlockSpec((8, TM), lambda i: (0, i)),
                   pl.BlockSpec((8, TM), lambda i: (0, i)),
                   pl.BlockSpec((N_EXPERTS, LANES), _const2)]
ROUTE_OUT_SHAPES = [jax.ShapeDtypeStruct((8, N_TOK), jnp.int32),
                    jax.ShapeDtypeStruct((8, N_TOK), f32),
                    jax.ShapeDtypeStruct((N_EXPERTS, LANES), f32)]
ROUTE_SCRATCH = [pltpu.VMEM((N_EXPERTS, LANES), f32)]


def _router_operands(nw, w_group, b_group, w_expert, b_expert, tri):
    wt = jnp.concatenate([w_group, w_expert], axis=1).T
    wt = jnp.pad(wt, ((0, LANES - wt.shape[0]), (0, 0)))
    wt_hi = wt.astype(bf16)
    wt_lo = (wt - wt_hi.astype(f32)).astype(bf16)
    bias_col = jnp.pad(jnp.concatenate([b_group, b_expert]), (0, LANES - N_GROUPS - N_EXPERTS))[:, None]
    return nw, wt_hi, wt_lo, bias_col, tri


N_ROW_TILES = N_TOK // TM_ROW
ROW_TILE = 8
IDX_GROUPS = TM_ROW // LANES
IDX_TRIPS = LANES // ROW_UNROLL
IDX_PER_TRIP = IDX_GROUPS * ROW_UNROLL * TOP_K


def _issue_rows(idx_smem, s, make_copy):
    def body(r8, c):
        for g in range(IDX_GROUPS):
            for u in range(ROW_UNROLL):
                for k in range(TOP_K):
                    d = pl.multiple_of(idx_smem[s, r8, (g * ROW_UNROLL + u) * TOP_K + k] * ROW_TILE,
                                       ROW_TILE)
                    r = pl.multiple_of((g * LANES + r8 * ROW_UNROLL + u) * ROW_TILE, ROW_TILE)
                    make_copy(k, r, d).start(priority=k)
        return c
    lax.fori_loop(0, IDX_TRIPS, body, 0)


def _dest_tiles(dest):
    d = dest.reshape(TOP_K, N_ROW_TILES, IDX_GROUPS, IDX_TRIPS, ROW_UNROLL)
    return d.transpose(1, 3, 2, 4, 0).reshape(N_ROW_TILES, IDX_TRIPS, IDX_PER_TRIP)


def _dispatch_kernel(last_ref, x_ref, nw_ref, dest_hbm, xb_hbm, hbuf, zbuf, idx_smem, idx_sem,
                     row_sem, zero_sem):
    i = pl.program_id(0)
    slot = i % 2

    def idx_copy(tile, s):
        return pltpu.make_async_copy(dest_hbm.at[tile], idx_smem.at[s], idx_sem.at[s])

    @pl.when(i == 0)
    def _():
        zbuf[...] = jnp.zeros_like(zbuf)

        def zero_copy(row):
            row0 = pl.multiple_of(row * ROW_TILE, ROW_TILE)
            return pltpu.make_async_copy(zbuf, xb_hbm.at[pl.ds(row0, MOE_ROWS * ROW_TILE), :], zero_sem)

        def start(e, c):
            @pl.when(last_ref[e] >= 0)
            def _():
                zero_copy(last_ref[e]).start()
            return c

        def wait(e, c):
            @pl.when(last_ref[e] >= 0)
            def _():
                zero_copy(last_ref[e]).wait()
            return c

        def start_tail(b, c):
            zero_copy(b * MOE_ROWS).start()
            return c

        def wait_tail(b, c):
            zero_copy(b * MOE_ROWS).wait()
            return c

        n_used = last_ref[N_EXPERTS]
        lax.fori_loop(0, N_EXPERTS, start, 0)
        lax.fori_loop(n_used, N_MOE_BLOCKS, start_tail, 0)
        lax.fori_loop(0, N_EXPERTS, wait, 0)
        lax.fori_loop(n_used, N_MOE_BLOCKS, wait_tail, 0)

    def drain(s):
        for _ in range(TOP_K):
            pltpu.make_async_copy(hbuf.at[s], xb_hbm.at[pl.ds(0, TM_ROW * ROW_TILE), :],
                                  row_sem.at[s]).wait()

    @pl.when(i == 0)
    def _():
        idx_copy(0, 0).start()

    idx_copy(i, slot).wait()

    @pl.when(i + 1 < N_ROW_TILES)
    def _():
        idx_copy(i + 1, 1 - slot).start()

    h = _rms(x_ref[...], nw_ref[...])
    for s in range(2):
        @pl.when(slot == s)
        def _(s=s):
            @pl.when(i >= 2)
            def _():
                drain(s)
            for c in range(ROW_TILE):
                hbuf[s, pl.ds(c, TM_ROW, stride=ROW_TILE), :] = h[:, c * LANES:(c + 1) * LANES]
            _issue_rows(idx_smem, s, lambda k, r, d: pltpu.make_async_copy(
                hbuf.at[s, pl.ds(r, ROW_TILE), :], xb_hbm.at[pl.ds(d, ROW_TILE), :], row_sem.at[s]))

    @pl.when(i == N_ROW_TILES - 1)
    def _():
        drain(0)
        drain(1)


def _dispatch(x, nw, dest_tiles, zero_plan):
    grid_spec = pltpu.PrefetchScalarGridSpec(
        num_scalar_prefetch=1,
        grid=(N_ROW_TILES,),
        in_specs=[pl.BlockSpec((TM_ROW, D_MODEL), lambda i, lb: (i, 0)),
                  pl.BlockSpec((1, D_MODEL), lambda i, lb: (0, 0)),
                  pl.BlockSpec(memory_space=pl.ANY)],
        out_specs=pl.BlockSpec(memory_space=pl.ANY),
        scratch_shapes=[pltpu.VMEM((2, TM_ROW * ROW_TILE, LANES), f32),
                        pltpu.VMEM((MOE_ROWS * ROW_TILE, LANES), f32),
                        pltpu.SMEM((2, IDX_TRIPS, IDX_PER_TRIP), jnp.int32),
                        pltpu.SemaphoreType.DMA((2,)),
                        pltpu.SemaphoreType.DMA((2,)),
                        pltpu.SemaphoreType.DMA])
    return pl.pallas_call(
        _dispatch_kernel,
        grid_spec=grid_spec,
        out_shape=jax.ShapeDtypeStruct((N_MOE_ROWS * ROW_TILE, LANES), f32),
        compiler_params=_cparams(("arbitrary",)),
        name="moe_dispatch",
    )(zero_plan, x, nw, dest_tiles)


def _expert_kernel(be_ref, nused_ref, xb_ref, wg_ref, wu_ref, wd_ref, yb_ref, wg_b, wu_b, wd_b):
    j = pl.program_id(0)
    used = j < nused_ref[0]
    changed = jnp.logical_or(j == 0, be_ref[j] != be_ref[jnp.maximum(j - 1, 0)])

    @pl.when(jnp.logical_and(used, changed))
    def _():
        wg_b[...] = wg_ref[...].astype(bf16)
        wu_b[...] = wu_ref[...].astype(bf16)
        wd_b[...] = wd_ref[...].astype(bf16)

    @pl.when(used)
    def _():
        xb = jnp.concatenate([xb_ref[pl.ds(c, MOE_ROWS, stride=ROW_TILE), :] for c in range(ROW_TILE)],
                             axis=1).astype(bf16)
        hid = jax.nn.silu(_dot(xb, wg_b[...])) * _dot(xb, wu_b[...])
        y = _dot(hid.astype(bf16), wd_b[...])
        for c in range(ROW_TILE):
            yb_ref[pl.ds(c, MOE_ROWS, stride=ROW_TILE), :] = y[:, c * LANES:(c + 1) * LANES]

    @pl.when(jnp.logical_not(used))
    def _():
        yb_ref[...] = jnp.zeros_like(yb_ref)


def _experts(layer, block_expert, n_used, xb, w_gate, w_up, w_down):
    grid_spec = pltpu.PrefetchScalarGridSpec(
        num_scalar_prefetch=2,
        grid=(N_MOE_BLOCKS,),
        in_specs=[pl.BlockSpec((MOE_ROWS * ROW_TILE, LANES),
                               lambda j, be, nu: (jnp.minimum(j, nu[0] - 1), 0)),
                  pl.BlockSpec((None, None, D_MODEL, D_FF_EXPERT), lambda j, be, nu: (layer, be[j], 0, 0)),
                  pl.BlockSpec((None, None, D_MODEL, D_FF_EXPERT), lambda j, be, nu: (layer, be[j], 0, 0)),
                  pl.BlockSpec((None, None, D_FF_EXPERT, D_MODEL), lambda j, be, nu: (layer, be[j], 0, 0))],
        out_specs=pl.BlockSpec((MOE_ROWS * ROW_TILE, LANES), lambda j, be, nu: (j, 0)),
        scratch_shapes=[pltpu.VMEM((D_MODEL, D_FF_EXPERT), bf16),
                        pltpu.VMEM((D_MODEL, D_FF_EXPERT), bf16),
                        pltpu.VMEM((D_FF_EXPERT, D_MODEL), bf16)])
    return pl.pallas_call(
        _expert_kernel,
        grid_spec=grid_spec,
        out_shape=jax.ShapeDtypeStruct((N_MOE_ROWS * ROW_TILE, LANES), f32),
        compiler_params=_cparams(("arbitrary",)),
        name="moe_experts",
    )(block_expert, n_used, xb, w_gate, w_up, w_down)


def _combine_kernel(x_ref, gate_ref, dest_hbm, yb_hbm, o_ref, ybuf, idx_smem, idx_sem, row_sem):
    i = pl.program_id(0)
    slot = i % 2

    def idx_copy(tile, s):
        return pltpu.make_async_copy(dest_hbm.at[tile], idx_smem.at[s], idx_sem.at[s])

    def gather(s):
        _issue_rows(idx_smem, s, lambda k, r, d: pltpu.make_async_copy(
            yb_hbm.at[pl.ds(d, ROW_TILE), :], ybuf.at[s, k, pl.ds(r, ROW_TILE), :], row_sem.at[s]))

    @pl.when(i == 0)
    def _():
        idx_copy(0, 0).start()
        idx_copy(0, 0).wait()
        gather(0)
        idx_copy(1, 1).start()

    for s in range(2):
        @pl.when(jnp.logical_and(i + 1 < N_ROW_TILES, slot == 1 - s))
        def _(s=s):
            idx_copy(i + 1, s).wait()
            gather(s)

    @pl.when(i + 2 < N_ROW_TILES)
    def _():
        idx_copy(i + 2, slot).start()

    g0 = gate_ref[:, 0:1]
    g1 = gate_ref[:, 1:2]
    for s in range(2):
        @pl.when(slot == s)
        def _(s=s):
            for k in range(TOP_K):
                pltpu.make_async_copy(yb_hbm.at[pl.ds(0, TM_ROW * ROW_TILE), :], ybuf.at[s, k],
                                      row_sem.at[s]).wait()
            for c in range(ROW_TILE):
                y0 = ybuf[s, 0, pl.ds(c, TM_ROW, stride=ROW_TILE), :]
                y1 = ybuf[s, 1, pl.ds(c, TM_ROW, stride=ROW_TILE), :]
                lanes = slice(c * LANES, (c + 1) * LANES)
                o_ref[:, lanes] = x_ref[:, lanes] + (y0 * g0 + y1 * g1)


def _combine(x, gate_cols, dest_tiles, yb):
    return pl.pallas_call(
        _combine_kernel,
        grid=(N_ROW_TILES,),
        in_specs=[pl.BlockSpec((TM_ROW, D_MODEL), lambda i: (i, 0)),
                  pl.BlockSpec((TM_ROW, TOP_K), lambda i: (i, 0)),
                  pl.BlockSpec(memory_space=pl.ANY),
                  pl.BlockSpec(memory_space=pl.ANY)],
        out_specs=pl.BlockSpec((TM_ROW, D_MODEL), lambda i: (i, 0)),
        out_shape=jax.ShapeDtypeStruct((N_TOK, D_MODEL), f32),
        scratch_shapes=[pltpu.VMEM((2, TOP_K, TM_ROW * ROW_TILE, LANES), f32),
                        pltpu.SMEM((2, IDX_TRIPS, IDX_PER_TRIP), jnp.int32),
                        pltpu.SemaphoreType.DMA((2,)),
                        pltpu.SemaphoreType.DMA((2,))],
        compiler_params=_cparams(("arbitrary",)),
        name="moe_combine",
    )(x, gate_cols, dest_tiles, yb)


def _hier_moe(layer, x, nw, idx, gates, cnt, w_gate, w_up, w_down):
    counts = cnt[:, 0].astype(jnp.int32)
    padded = (counts + MOE_ROWS - 1) // MOE_ROWS * MOE_ROWS
    pends = jnp.cumsum(padded)
    pstarts = pends - padded
    eids = jnp.arange(N_EXPERTS, dtype=jnp.int32)[:, None, None]
    seg_start = jnp.sum(jnp.where(idx[None, 0:2] == eids, pstarts[:, None, None], 0), axis=0)
    dest = seg_start + idx[2:4]
    dest_tiles = _dest_tiles(dest)
    block_row0 = jnp.arange(N_MOE_BLOCKS, dtype=jnp.int32) * MOE_ROWS
    block_expert = jnp.minimum(
        jnp.sum((pends[None, :] <= block_row0[:, None]).astype(jnp.int32), axis=1), N_EXPERTS - 1)
    n_used = (pends[-1:] // MOE_ROWS).astype(jnp.int32)

    zero_plan = jnp.concatenate([jnp.where(padded > 0, pends - MOE_ROWS, -1), n_used]).astype(jnp.int32)
    xb = _dispatch(x, nw, dest_tiles, zero_plan)
    yb = _experts(layer, block_expert, n_used, xb, w_gate, w_up, w_down)
    return _combine(x, gates[0:2].T, dest_tiles, yb)


def kernel(x, positions, conv_norm_w, conv_w_in, conv_k, conv_w_out, mla_norm_w, mla_w_down, mla_q_lat_norm, mla_kv_lat_norm, mla_w_q_up, mla_w_kv_up, mla_q_head_norm, mla_k_head_norm, mla_w_out, ffn_norm_w, router_w_group, router_b_group, router_w_expert, router_b_expert, exp_w_gate, exp_w_up, exp_w_down):
    half = QK_ROPE_DIM // 2
    inv_freq = ROPE_THETA ** (-jnp.arange(0, QK_ROPE_DIM, 2, dtype=f32) / QK_ROPE_DIM)
    ang = positions.astype(f32)[..., None] * inv_freq
    cos = jnp.cos(ang)
    sin = jnp.sin(ang)
    cos_t = cos.transpose(0, 2, 1)
    sin_t = sin.transpose(0, 2, 1)
    cs = jnp.concatenate([cos, cos, -sin, sin], axis=-1).reshape(N_TOK, LANES)
    row = lax.broadcasted_iota(jnp.int32, (TM, TM), 0)
    col = lax.broadcasted_iota(jnp.int32, (TM, TM), 1)
    tri = (row < col).astype(bf16)

    xt = x.reshape(N_TOK, D_MODEL)
    for i in range(DEPTH):
        j = i // N_MIXERS
        route_ops = _router_operands(ffn_norm_w[i][None, :], router_w_group[i], router_b_group[i],
                                     router_w_expert[i], router_b_expert[i], tri)
        if i % N_MIXERS == 0:
            z, gb = _conv_in(xt, conv_norm_w[j][None, :], conv_w_in[j].astype(bf16))
            xt, idx, gates, cnt = _conv_out(xt, z, gb, conv_k[j], conv_w_out[j].astype(bf16), route_ops)
        else:
            wd = mla_w_down[j]
            ro = Q_LORA_RANK + KV_LORA_RANK
            wd_aug = jnp.concatenate([wd, wd[:, ro + half:ro + 2 * half], wd[:, ro:ro + half]],
                                     axis=1).astype(bf16)
            wkv = mla_w_kv_up[j].reshape(KV_LORA_RANK, N_HEADS, QK_NOPE_DIM + V_HEAD_DIM)
            wk = wkv[:, :, :QK_NOPE_DIM].reshape(KV_LORA_RANK, N_HEADS * QK_NOPE_DIM).astype(bf16)
            wvt = wkv[:, :, QK_NOPE_DIM:].reshape(KV_LORA_RANK, N_HEADS * V_HEAD_DIM).T.astype(bf16)
            wqt = mla_w_q_up[j].T.astype(bf16)
            khn = mla_k_head_norm[j]
            kw1 = khn[QK_NOPE_DIM:QK_NOPE_DIM + half]
            kw2 = khn[QK_NOPE_DIM + half:]
            g128 = jnp.concatenate([kw1, kw2, kw2, kw1])[None, :]
            qt, k, vt = _mla_proj(xt, mla_norm_w[j][None, :], wd_aug, mla_q_lat_norm[j][None, :],
                                  mla_kv_lat_norm[j][None, :], wqt, wk, wvt,
                                  mla_q_head_norm[j][:, None], khn[None, :], g128, cs, cos_t, sin_t)
            ot = _attention(qt, k, vt)
            xt, idx, gates, cnt = _mla_out(xt, ot.reshape(BATCH, N_HEADS * V_HEAD_DIM, SEQ),
                                           mla_w_out[j].T.astype(bf16), route_ops)
        xt = _hier_moe(i, xt, ffn_norm_w[i][None, :], idx, gates, cnt, exp_w_gate, exp_w_up, exp_w_down)
    return xt.reshape(BATCH, SEQ, D_MODEL)
```

```python
import math

import jax
import jax.numpy as jnp
from jax import lax
from jax.experimental import pallas as pl
from jax.experimental.pallas import tpu as pltpu

D_MODEL = 1024
BATCH = 2
SEQ = 16384
DEPTH = 4
N_TOK = BATCH * SEQ
N_MIXERS = 2
N_HEADS = 8
Q_LORA_RANK = 384
KV_LORA_RANK = 256
QK_NOPE_DIM = 128
QK_ROPE_DIM = 64
QK_HEAD_DIM = QK_NOPE_DIM + QK_ROPE_DIM
V_HEAD_DIM = 128
V_ROWS = V_HEAD_DIM + 16
LATENT_DIM = Q_LORA_RANK + KV_LORA_RANK + QK_ROPE_DIM
LATENT_AUG = LATENT_DIM + QK_ROPE_DIM
ROPE_THETA = 10000.0
N_GROUPS = 8
EXPERTS_PER_GROUP = 8
N_EXPERTS = N_GROUPS * EXPERTS_PER_GROUP
TOP_K = 2
D_FF_EXPERT = 384
RMS_EPS = 1e-6

LANES = 128
VMEM_LIMIT = 56 * 1024 * 1024

TM = 1024
TM_ROW = 512
ROW_UNROLL = 8
MOE_ROWS = 256
N_MOE_BLOCKS = N_TOK * TOP_K // MOE_ROWS + N_EXPERTS
N_MOE_ROWS = N_MOE_BLOCKS * MOE_ROWS
TQ = 2048
TK = 512
QK_SCALE_LOG2E = (QK_HEAD_DIM ** -0.5) * math.log2(math.e)

bf16 = jnp.bfloat16
f32 = jnp.float32


def _cparams(sem):
    return pltpu.CompilerParams(dimension_semantics=sem, vmem_limit_bytes=VMEM_LIMIT)


def _rms(x, w):
    return x * lax.rsqrt(jnp.mean(x * x, axis=-1, keepdims=True) + RMS_EPS) * w


def _dot(a, b):
    return jnp.dot(a, b, preferred_element_type=f32)


def _dot_nt(a, b):
    return lax.dot_general(a, b, (((1,), (1,)), ((), ())), preferred_element_type=f32)


def _conv_in_kernel(x_ref, nw_ref, w_ref, z_ref, gb_ref):
    h = _rms(x_ref[...], nw_ref[...]).astype(bf16)
    gb = _dot(h, w_ref[:, 0:D_MODEL])
    gc = _dot(h, w_ref[:, D_MODEL:2 * D_MODEL])
    u = _dot(h, w_ref[:, 2 * D_MODEL:3 * D_MODEL])
    z_ref[...] = (gc * u).astype(bf16)
    gb_ref[...] = gb.astype(bf16)


def _conv_in(x, nw, w_in):
    return pl.pallas_call(
        _conv_in_kernel,
        grid=(N_TOK // TM,),
        in_specs=[pl.BlockSpec((TM, D_MODEL), lambda i: (i, 0)),
                  pl.BlockSpec((1, D_MODEL), lambda i: (0, 0)),
                  pl.BlockSpec((D_MODEL, 3 * D_MODEL), lambda i: (0, 0))],
        out_specs=[pl.BlockSpec((TM, D_MODEL), lambda i: (i, 0)),
                   pl.BlockSpec((TM, D_MODEL), lambda i: (i, 0))],
        out_shape=[jax.ShapeDtypeStruct((N_TOK, D_MODEL), bf16),
                   jax.ShapeDtypeStruct((N_TOK, D_MODEL), bf16)],
        compiler_params=_cparams(("arbitrary",)),
        name="conv_in",
    )(x, nw, w_in)


def _conv_out_kernel(x_ref, z_ref, zp_ref, zn_ref, gb_ref, k_ref, w_ref, *route_refs):
    o_ref = route_refs[len(ROUTE_IN_SPECS)]
    i = pl.program_id(0)
    tiles_per_seq = SEQ // TM
    first = (i % tiles_per_seq) == 0
    last = (i % tiles_per_seq) == tiles_per_seq - 1
    z = z_ref[...].astype(f32)
    prev_row = jnp.where(first, 0.0, zp_ref[7:8, :].astype(f32))
    next_row = jnp.where(last, 0.0, zn_ref[0:1, :].astype(f32))
    row = lax.broadcasted_iota(jnp.int32, z.shape, 0)
    z_m1 = jnp.where(row == 0, prev_row, pltpu.roll(z, 1, 0))
    z_p1 = jnp.where(row == TM - 1, next_row, pltpu.roll(z, TM - 1, 0))
    conv = z_m1 * k_ref[0:1, :] + z * k_ref[1:2, :] + z_p1 * k_ref[2:3, :]
    y = _dot((gb_ref[...].astype(f32) * conv).astype(bf16), w_ref[...])
    x_new = x_ref[...] + y
    o_ref[...] = x_new
    _route_tile(x_new, *route_refs[:len(ROUTE_IN_SPECS)], *route_refs[len(ROUTE_IN_SPECS) + 1:])


def _conv_out(x, z, gb, conv_k, w_out, route_ops):
    n_halo = N_TOK // 8
    per = TM // 8
    return pl.pallas_call(
        _conv_out_kernel,
        grid=(N_TOK // TM,),
        in_specs=[pl.BlockSpec((TM, D_MODEL), lambda i: (i, 0)),
                  pl.BlockSpec((TM, D_MODEL), lambda i: (i, 0)),
                  pl.BlockSpec((8, D_MODEL), lambda i: (jnp.maximum(i * per - 1, 0), 0)),
                  pl.BlockSpec((8, D_MODEL), lambda i: (jnp.minimum((i + 1) * per, n_halo - 1), 0)),
                  pl.BlockSpec((TM, D_MODEL), lambda i: (i, 0)),
                  pl.BlockSpec((3, D_MODEL), lambda i: (0, 0)),
                  pl.BlockSpec((D_MODEL, D_MODEL), lambda i: (0, 0))] + ROUTE_IN_SPECS,
        out_specs=[pl.BlockSpec((TM, D_MODEL), lambda i: (i, 0))] + ROUTE_OUT_SPECS,
        out_shape=[jax.ShapeDtypeStruct((N_TOK, D_MODEL), f32)] + ROUTE_OUT_SHAPES,
        scratch_shapes=ROUTE_SCRATCH,
        compiler_params=_cparams(("arbitrary",)),
        name="conv_out",
    )(x, z, z, z, gb, conv_k, w_out, *route_ops)


def _mla_proj_kernel(x_ref, nw_ref, wd_ref, qln_ref, kvln_ref, wqt_ref, wk_ref, wvt_ref,
                     qhn_ref, khn_ref, g128_ref, cs_ref, cost_ref, sint_ref,
                     qt_ref, k_ref, vt_ref):
    h = _rms(x_ref[...], nw_ref[...]).astype(bf16)
    lat = _dot(h, wd_ref[...])
    q_lat = _rms(lat[:, 0:Q_LORA_RANK], qln_ref[...]).astype(bf16)
    kv_lat = _rms(lat[:, Q_LORA_RANK:Q_LORA_RANK + KV_LORA_RANK], kvln_ref[...]).astype(bf16)
    kr_blk = lat[:, Q_LORA_RANK + KV_LORA_RANK:LATENT_AUG]
    lane = lax.broadcasted_iota(jnp.int32, kr_blk.shape, 1)
    kr_ss = jnp.sum(jnp.where(lane < QK_ROPE_DIM, kr_blk * kr_blk, 0.0), axis=-1, keepdims=True)
    t = kr_blk * g128_ref[...] * cs_ref[...]
    kr_roped = t + pltpu.roll(t, QK_ROPE_DIM, 1)

    k_nope = _dot(kv_lat, wk_ref[...])
    cos_t = cost_ref[0]
    sin_t = sint_ref[0]
    for hd in range(N_HEADS):
        kn = k_nope[:, hd * QK_NOPE_DIM:(hd + 1) * QK_NOPE_DIM]
        ss = jnp.sum(kn * kn, axis=-1, keepdims=True) + kr_ss
        r = lax.rsqrt(ss * (1.0 / QK_HEAD_DIM) + RMS_EPS)
        k_ref[0, hd, :, 0:QK_NOPE_DIM] = (kn * r * khn_ref[:, 0:QK_NOPE_DIM]).astype(bf16)
        k_ref[0, hd, :, QK_NOPE_DIM:QK_HEAD_DIM] = (kr_roped[:, 0:QK_ROPE_DIM] * r).astype(bf16)

        qt = _dot_nt(wqt_ref[hd * QK_HEAD_DIM:(hd + 1) * QK_HEAD_DIM, :], q_lat)
        rq = lax.rsqrt(jnp.mean(qt * qt, axis=0, keepdims=True) + RMS_EPS)
        qn = qt * rq * qhn_ref[...] * QK_SCALE_LOG2E
        half = QK_ROPE_DIM // 2
        x1 = qn[QK_NOPE_DIM:QK_NOPE_DIM + half, :]
        x2 = qn[QK_NOPE_DIM + half:QK_HEAD_DIM, :]
        qt_ref[0, hd, 0:QK_NOPE_DIM, :] = qn[0:QK_NOPE_DIM, :].astype(bf16)
        qt_ref[0, hd, QK_NOPE_DIM:QK_NOPE_DIM + half, :] = (x1 * cos_t - x2 * sin_t).astype(bf16)
        qt_ref[0, hd, QK_NOPE_DIM + half:QK_HEAD_DIM, :] = (x2 * cos_t + x1 * sin_t).astype(bf16)

        vt = _dot_nt(wvt_ref[hd * V_HEAD_DIM:(hd + 1) * V_HEAD_DIM, :], kv_lat)
        vt_ref[0, hd, 0:V_HEAD_DIM, :] = vt.astype(bf16)
        vt_ref[0, hd, V_HEAD_DIM:V_ROWS, :] = jnp.ones((V_ROWS - V_HEAD_DIM, TM), bf16)


def _mla_proj(x, nw, wd_aug, qln, kvln, wqt, wk, wvt, qhn_col, khn_row, g128, cs, cos_t, sin_t):
    tps = SEQ // TM
    const = lambda i: (0, 0)
    return pl.pallas_call(
        _mla_proj_kernel,
        grid=(N_TOK // TM,),
        in_specs=[pl.BlockSpec((TM, D_MODEL), lambda i: (i, 0)),
                  pl.BlockSpec((1, D_MODEL), const),
                  pl.BlockSpec((D_MODEL, LATENT_AUG), const),
                  pl.BlockSpec((1, Q_LORA_RANK), const),
                  pl.BlockSpec((1, KV_LORA_RANK), const),
                  pl.BlockSpec((N_HEADS * QK_HEAD_DIM, Q_LORA_RANK), const),
                  pl.BlockSpec((KV_LORA_RANK, N_HEADS * QK_NOPE_DIM), const),
                  pl.BlockSpec((N_HEADS * V_HEAD_DIM, KV_LORA_RANK), const),
                  pl.BlockSpec((QK_HEAD_DIM, 1), const),
                  pl.BlockSpec((1, QK_HEAD_DIM), const),
                  pl.BlockSpec((1, LANES), const),
                  pl.BlockSpec((TM, LANES), lambda i: (i, 0)),
                  pl.BlockSpec((1, QK_ROPE_DIM // 2, TM), lambda i: (i // tps, 0, i % tps)),
                  pl.BlockSpec((1, QK_ROPE_DIM // 2, TM), lambda i: (i // tps, 0, i % tps))],
        out_specs=[pl.BlockSpec((1, N_HEADS, QK_HEAD_DIM, TM), lambda i: (i // tps, 0, 0, i % tps)),
                   pl.BlockSpec((1, N_HEADS, TM, QK_HEAD_DIM), lambda i: (i // tps, 0, i % tps, 0)),
                   pl.BlockSpec((1, N_HEADS, V_ROWS, TM), lambda i: (i // tps, 0, 0, i % tps))],
        out_shape=[jax.ShapeDtypeStruct((BATCH, N_HEADS, QK_HEAD_DIM, SEQ), bf16),
                   jax.ShapeDtypeStruct((BATCH, N_HEADS, SEQ, QK_HEAD_DIM), bf16),
                   jax.ShapeDtypeStruct((BATCH, N_HEADS, V_ROWS, SEQ), bf16)],
        compiler_params=_cparams(("arbitrary",)),
        name="mla_proj",
    )(x, nw, wd_aug, qln, kvln, wqt, wk, wvt, qhn_col, khn_row, g128, cs, cos_t, sin_t)


def _attn_kernel(qt_ref, k_ref, vt_ref, o_ref, acc_ref, s0_ref, s1_ref, p0_ref, p1_ref):
    qt = qt_ref[0, 0]
    n = SEQ // TK
    acc_ref[...] = jnp.zeros_like(acc_ref)

    def scores(j, s_ref):
        off = pl.multiple_of(j * TK, TK)
        s = _dot(k_ref[0, 0, pl.ds(off, TK), :], qt)
        s_ref[...] = s
        return jnp.max(s, axis=0, keepdims=True)

    def softmax(s_ref, p_ref, m, cmax):
        m_new = jnp.maximum(m, cmax)
        alpha = jnp.exp2(m - m_new)
        p_ref[...] = jnp.exp2((s_ref[...] - m_new).astype(bf16))
        return m_new, alpha

    def values(j, p_ref, alpha):
        off = pl.multiple_of(j * TK, TK)
        pv = _dot(vt_ref[0, 0, :, pl.ds(off, TK)], p_ref[...])
        acc_ref[...] = alpha * acc_ref[...] + pv

    m = jnp.full((1, TQ), -jnp.inf, f32)
    c0 = scores(0, s0_ref)
    c1 = scores(1, s1_ref)
    m, a0 = softmax(s0_ref, p0_ref, m, c0)
    c0 = scores(2, s0_ref)
    m, a1 = softmax(s1_ref, p1_ref, m, c1)
    values(0, p0_ref, a0)

    def pair(jj, carry):
        m, a_prev, c0 = carry
        j = 2 * jj
        c1 = scores(j + 1, s1_ref)
        m, a0 = softmax(s0_ref, p0_ref, m, c0)
        values(j - 1, p1_ref, a_prev)
        c0 = scores(j + 2, s0_ref)
        m, a1 = softmax(s1_ref, p1_ref, m, c1)
        values(j, p0_ref, a0)
        return m, a1, c0

    m, a_prev, c0 = lax.fori_loop(1, n // 2 - 1, pair, (m, a1, c0))
    c1 = scores(n - 1, s1_ref)
    m, a0 = softmax(s0_ref, p0_ref, m, c0)
    values(n - 3, p1_ref, a_prev)
    m, a1 = softmax(s1_ref, p1_ref, m, c1)
    values(n - 2, p0_ref, a0)
    values(n - 1, p1_ref, a1)
    o_ref[0, 0] = (acc_ref[0:V_HEAD_DIM, :] / acc_ref[V_HEAD_DIM:V_HEAD_DIM + 1, :]).astype(bf16)


def _attention(qt, k, vt):
    return pl.pallas_call(
        _attn_kernel,
        grid=(BATCH, N_HEADS, SEQ // TQ),
        in_specs=[pl.BlockSpec((1, 1, QK_HEAD_DIM, TQ), lambda b, h, i: (b, h, 0, i)),
                  pl.BlockSpec((1, 1, SEQ, QK_HEAD_DIM), lambda b, h, i: (b, h, 0, 0)),
                  pl.BlockSpec((1, 1, V_ROWS, SEQ), lambda b, h, i: (b, h, 0, 0))],
        out_specs=pl.BlockSpec((1, 1, V_HEAD_DIM, TQ), lambda b, h, i: (b, h, 0, i)),
        out_shape=jax.ShapeDtypeStruct((BATCH, N_HEADS, V_HEAD_DIM, SEQ), bf16),
        scratch_shapes=[pltpu.VMEM((V_ROWS, TQ), f32),
                        pltpu.VMEM((TK, TQ), f32), pltpu.VMEM((TK, TQ), f32),
                        pltpu.VMEM((TK, TQ), bf16), pltpu.VMEM((TK, TQ), bf16)],
        compiler_params=_cparams(("arbitrary", "arbitrary", "arbitrary")),
        name="mla_attention",
    )(qt, k, vt)


def _mla_out_kernel(x_ref, ot_ref, w_ref, *route_refs):
    o_ref = route_refs[len(ROUTE_IN_SPECS)]
    yt = _dot(w_ref[...], ot_ref[0])
    x_new = x_ref[...] + yt.T
    o_ref[...] = x_new
    _route_tile(x_new, *route_refs[:len(ROUTE_IN_SPECS)], *route_refs[len(ROUTE_IN_SPECS) + 1:])


def _mla_out(x, ot, w_out_t, route_ops):
    tps = SEQ // TM
    return pl.pallas_call(
        _mla_out_kernel,
        grid=(N_TOK // TM,),
        in_specs=[pl.BlockSpec((TM, D_MODEL), lambda i: (i, 0)),
                  pl.BlockSpec((1, N_HEADS * V_HEAD_DIM, TM), lambda i: (i // tps, 0, i % tps)),
                  pl.BlockSpec((D_MODEL, N_HEADS * V_HEAD_DIM), lambda i: (0, 0))] + ROUTE_IN_SPECS,
        out_specs=[pl.BlockSpec((TM, D_MODEL), lambda i: (i, 0))] + ROUTE_OUT_SPECS,
        out_shape=[jax.ShapeDtypeStruct((N_TOK, D_MODEL), f32)] + ROUTE_OUT_SHAPES,
        scratch_shapes=ROUTE_SCRATCH,
        compiler_params=_cparams(("arbitrary",)),
        name="mla_out",
    )(x, ot, w_out_t, *route_ops)


def _first_argmax(v, n):
    idx = lax.broadcasted_iota(jnp.int32, v.shape, 0)
    mx = jnp.max(v, axis=0, keepdims=True)
    return jnp.min(jnp.where(v == mx, idx, n), axis=0, keepdims=True)


def _route_tile(x, nw_ref, wt_hi_ref, wt_lo_ref, bias_ref, tri_ref,
                idx_ref, gate_ref, cnt_ref, base_ref):
    i = pl.program_id(0)

    @pl.when(i == 0)
    def _():
        base_ref[...] = jnp.zeros_like(base_ref)

    h = _rms(x, nw_ref[...])
    h_hi = h.astype(bf16)
    h_lo = (h - h_hi.astype(f32)).astype(bf16)
    logits = (_dot_nt(wt_hi_ref[...], h_hi) + _dot_nt(wt_lo_ref[...], h_hi)
              + _dot_nt(wt_hi_ref[...], h_lo))
    g_logits = logits[0:N_GROUPS, :]
    g_exp = jnp.exp(g_logits - jnp.max(g_logits, axis=0, keepdims=True))
    g_prob = g_exp / jnp.sum(g_exp, axis=0, keepdims=True)
    g_sel = _first_argmax(g_logits + bias_ref[0:N_GROUPS, :], N_GROUPS)
    gidx = lax.broadcasted_iota(jnp.int32, g_logits.shape, 0)
    g_gate = jnp.sum(jnp.where(gidx == g_sel, g_prob, 0.0), axis=0, keepdims=True)

    e_logits = jnp.zeros((EXPERTS_PER_GROUP, TM), f32)
    e_bias = jnp.zeros((EXPERTS_PER_GROUP, TM), f32)
    for g in range(N_GROUPS):
        lo = N_GROUPS + g * EXPERTS_PER_GROUP
        e_logits = jnp.where(g_sel == g, logits[lo:lo + EXPERTS_PER_GROUP, :], e_logits)
        e_bias = jnp.where(g_sel == g, bias_ref[lo:lo + EXPERTS_PER_GROUP, :], e_bias)
    scored = e_logits + e_bias
    eidx = lax.broadcasted_iota(jnp.int32, scored.shape, 0)
    top0 = _first_argmax(scored, EXPERTS_PER_GROUP)
    top1 = _first_argmax(jnp.where(eidx == top0, -jnp.inf, scored), EXPERTS_PER_GROUP)
    e_exp = jnp.exp(e_logits - jnp.max(e_logits, axis=0, keepdims=True))
    e_prob = e_exp / jnp.sum(e_exp, axis=0, keepdims=True)
    p0 = jnp.sum(jnp.where(eidx == top0, e_prob, 0.0), axis=0, keepdims=True)
    p1 = jnp.sum(jnp.where(eidx == top1, e_prob, 0.0), axis=0, keepdims=True)
    psum = p0 + p1
    gate0 = g_gate * (p0 / psum)
    gate1 = g_gate * (p1 / psum)
    eid0 = g_sel * EXPERTS_PER_GROUP + top0
    eid1 = g_sel * EXPERTS_PER_GROUP + top1

    xidx = lax.broadcasted_iota(jnp.int32, (N_EXPERTS, TM), 0)
    oh0 = xidx == eid0
    oh1 = xidx == eid1
    pre0 = _dot(jnp.where(oh0, 1.0, 0.0).astype(bf16), tri_ref[...])
    pre1 = _dot(jnp.where(oh1, 1.0, 0.0).astype(bf16), tri_ref[...])
    tot0 = jnp.sum(jnp.where(oh0, 1.0, 0.0), axis=1, keepdims=True)
    tot1 = jnp.sum(jnp.where(oh1, 1.0, 0.0), axis=1, keepdims=True)
    base = base_ref[:, 0:1]
    rank0 = jnp.sum(jnp.where(oh0, base + pre0, 0.0), axis=0, keepdims=True)
    rank1 = jnp.sum(jnp.where(oh1, base + tot0 + pre1, 0.0), axis=0, keepdims=True)
    new_base = base + tot0 + tot1
    base_ref[...] = jnp.broadcast_to(new_base, base_ref.shape)
    cnt_ref[...] = jnp.broadcast_to(new_base, cnt_ref.shape)

    zi = jnp.zeros((4, TM), jnp.int32)
    idx_ref[...] = jnp.concatenate(
        [eid0, eid1, rank0.astype(jnp.int32), rank1.astype(jnp.int32), zi], axis=0)
    gate_ref[...] = jnp.concatenate([gate0, gate1, jnp.zeros((6, TM), f32)], axis=0)


def _const2(i):
    return (0, 0)


ROUTE_IN_SPECS = [pl.BlockSpec((1, D_MODEL), _const2),
                  pl.BlockSpec((LANES, D_MODEL), _const2),
                  pl.BlockSpec((LANES, D_MODEL), _const2),
                  pl.BlockSpec((LANES, 1), _const2),
                  pl.BlockSpec((TM, TM), _const2)]
ROUTE_OUT_SPECS = [pl.BlockSpec((8, TM), lambda i: (0, i)),
                   pl.BlockSpec((8, TM), lambda i: (0, i)),
                   pl.BlockSpec((N_EXPERTS, LANES), _const2)]
ROUTE_OUT_SHAPES = [jax.ShapeDtypeStruct((8, N_TOK), jnp.int32),
                    jax.ShapeDtypeStruct((8, N_TOK), f32),
                    jax.ShapeDtypeStruct((N_EXPERTS, LANES), f32)]
ROUTE_SCRATCH = [pltpu.VMEM((N_EXPERTS, LANES), f32)]


def _router_operands(nw, w_group, b_group, w_expert, b_expert, tri):
    wt = jnp.concatenate([w_group, w_expert], axis=1).T
    wt = jnp.pad(wt, ((0, LANES - wt.shape[0]), (0, 0)))
    wt_hi = wt.astype(bf16)
    wt_lo = (wt - wt_hi.astype(f32)).astype(bf16)
    bias_col = jnp.pad(jnp.concatenate([b_group, b_expert]), (0, LANES - N_GROUPS - N_EXPERTS))[:, None]
    return nw, wt_hi, wt_lo, bias_col, tri


N_ROW_TILES = N_TOK // TM_ROW
ROW_TILE = 8
IDX_GROUPS = TM_ROW // LANES
IDX_TRIPS = LANES // ROW_UNROLL
IDX_PER_TRIP = IDX_GROUPS * ROW_UNROLL * TOP_K


def _issue_rows(idx_smem, s, make_copy):
    def body(r8, c):
        for g in range(IDX_GROUPS):
            for u in range(ROW_UNROLL):
                for k in range(TOP_K):
                    d = pl.multiple_of(idx_smem[s, r8, (g * ROW_UNROLL + u) * TOP_K + k] * ROW_TILE,
                                       ROW_TILE)
                    r = pl.multiple_of((g * LANES + r8 * ROW_UNROLL + u) * ROW_TILE, ROW_TILE)
                    make_copy(k, r, d).start(priority=k)
        return c
    lax.fori_loop(0, IDX_TRIPS, body, 0)


def _dest_tiles(dest):
    d = dest.reshape(TOP_K, N_ROW_TILES, IDX_GROUPS, IDX_TRIPS, ROW_UNROLL)
    return d.transpose(1, 3, 2, 4, 0).reshape(N_ROW_TILES, IDX_TRIPS, IDX_PER_TRIP)


def _dispatch_kernel(last_ref, x_ref, nw_ref, dest_hbm, xb_hbm, hbuf, zbuf, idx_smem, idx_sem,
                     row_sem, zero_sem):
    i = pl.program_id(0)
    slot = i % 2

    def idx_copy(tile, s):
        return pltpu.make_async_copy(dest_hbm.at[tile], idx_smem.at[s], idx_sem.at[s])

    @pl.when(i == 0)
    def _():
        zbuf[...] = jnp.zeros_like(zbuf)

        def zero_copy(row):
            row0 = pl.multiple_of(row * ROW_TILE, ROW_TILE)
            return pltpu.make_async_copy(zbuf, xb_hbm.at[pl.ds(row0, MOE_ROWS * ROW_TILE), :], zero_sem)

        def start(e, c):
            @pl.when(last_ref[e] >= 0)
            def _():
                zero_copy(last_ref[e]).start()
            return c

        def wait(e, c):
            @pl.when(last_ref[e] >= 0)
            def _():
                zero_copy(last_ref[e]).wait()
            return c

        def start_tail(b, c):
            zero_copy(b * MOE_ROWS).start()
            return c

        def wait_tail(b, c):
            zero_copy(b * MOE_ROWS).wait()
            return c

        n_used = last_ref[N_EXPERTS]
        lax.fori_loop(0, N_EXPERTS, start, 0)
        lax.fori_loop(n_used, N_MOE_BLOCKS, start_tail, 0)
        lax.fori_loop(0, N_EXPERTS, wait, 0)
        lax.fori_loop(n_used, N_MOE_BLOCKS, wait_tail, 0)

    def drain(s):
        for _ in range(TOP_K):
            pltpu.make_async_copy(hbuf.at[s], xb_hbm.at[pl.ds(0, TM_ROW * ROW_TILE), :],
                                  row_sem.at[s]).wait()

    @pl.when(i == 0)
    def _():
        idx_copy(0, 0).start()

    idx_copy(i, slot).wait()

    @pl.when(i + 1 < N_ROW_TILES)
    def _():
        idx_copy(i + 1, 1 - slot).start()

    h = _rms(x_ref[...], nw_ref[...])
    for s in range(2):
        @pl.when(slot == s)
        def _(s=s):
            @pl.when(i >= 2)
            def _():
                drain(s)
            for c in range(ROW_TILE):
                hbuf[s, pl.ds(c, TM_ROW, stride=ROW_TILE), :] = h[:, c * LANES:(c + 1) * LANES]
            _issue_rows(idx_smem, s, lambda k, r, d: pltpu.make_async_copy(
                hbuf.at[s, pl.ds(r, ROW_TILE), :], xb_hbm.at[pl.ds(d, ROW_TILE), :], row_sem.at[s]))

    @pl.when(i == N_ROW_TILES - 1)
    def _():
        drain(0)
        drain(1)


def _dispatch(x, nw, dest_tiles, zero_plan):
    grid_spec = pltpu.PrefetchScalarGridSpec(
        num_scalar_prefetch=1,
        grid=(N_ROW_TILES,),
        in_specs=[pl.BlockSpec((TM_ROW, D_MODEL), lambda i, lb: (i, 0)),
                  pl.BlockSpec((1, D_MODEL), lambda i, lb: (0, 0)),
                  pl.BlockSpec(memory_space=pl.ANY)],
        out_specs=pl.BlockSpec(memory_space=pl.ANY),
        scratch_shapes=[pltpu.VMEM((2, TM_ROW * ROW_TILE, LANES), f32),
                        pltpu.VMEM((MOE_ROWS * ROW_TILE, LANES), f32),
                        pltpu.SMEM((2, IDX_TRIPS, IDX_PER_TRIP), jnp.int32),
                        pltpu.SemaphoreType.DMA((2,)),
                        pltpu.SemaphoreType.DMA((2,)),
                        pltpu.SemaphoreType.DMA])
    return pl.pallas_call(
        _dispatch_kernel,
        grid_spec=grid_spec,
        out_shape=jax.ShapeDtypeStruct((N_MOE_ROWS * ROW_TILE, LANES), f32),
        compiler_params=_cparams(("arbitrary",)),
        name="moe_dispatch",
    )(zero_plan, x, nw, dest_tiles)


def _expert_kernel(be_ref, nused_ref, xb_ref, wg_ref, wu_ref, wd_ref, yb_ref, wg_b, wu_b, wd_b):
    j = pl.program_id(0)
    used = j < nused_ref[0]
    changed = jnp.logical_or(j == 0, be_ref[j] != be_ref[jnp.maximum(j - 1, 0)])

    @pl.when(jnp.logical_and(used, changed))
    def _():
        wg_b[...] = wg_ref[...].astype(bf16)
        wu_b[...] = wu_ref[...].astype(bf16)
        wd_b[...] = wd_ref[...].astype(bf16)

    @pl.when(used)
    def _():
        xb = jnp.concatenate([xb_ref[pl.ds(c, MOE_ROWS, stride=ROW_TILE), :] for c in range(ROW_TILE)],
                             axis=1).astype(bf16)
        hid = jax.nn.silu(_dot(xb, wg_b[...])) * _dot(xb, wu_b[...])
        y = _dot(hid.astype(bf16), wd_b[...])
        for c in range(ROW_TILE):
            yb_ref[pl.ds(c, MOE_ROWS, stride=ROW_TILE), :] = y[:, c * LANES:(c + 1) * LANES]

    @pl.when(jnp.logical_not(used))
    def _():
        yb_ref[...] = jnp.zeros_like(yb_ref)


def _experts(layer, block_expert, n_used, xb, w_gate, w_up, w_down):
    grid_spec = pltpu.PrefetchScalarGridSpec(
        num_scalar_prefetch=2,
        grid=(N_MOE_BLOCKS,),
        in_specs=[pl.BlockSpec((MOE_ROWS * ROW_TILE, LANES),
                               lambda j, be, nu: (jnp.minimum(j, nu[0] - 1), 0)),
                  pl.BlockSpec((None, None, D_MODEL, D_FF_EXPERT), lambda j, be, nu: (layer, be[j], 0, 0)),
                  pl.BlockSpec((None, None, D_MODEL, D_FF_EXPERT), lambda j, be, nu: (layer, be[j], 0, 0)),
                  pl.BlockSpec((None, None, D_FF_EXPERT, D_MODEL), lambda j, be, nu: (layer, be[j], 0, 0))],
        out_specs=pl.BlockSpec((MOE_ROWS * ROW_TILE, LANES), lambda j, be, nu: (j, 0)),
        scratch_shapes=[pltpu.VMEM((D_MODEL, D_FF_EXPERT), bf16),
                        pltpu.VMEM((D_MODEL, D_FF_EXPERT), bf16),
                        pltpu.VMEM((D_FF_EXPERT, D_MODEL), bf16)])
    return pl.pallas_call(
        _expert_kernel,
        grid_spec=grid_spec,
        out_shape=jax.ShapeDtypeStruct((N_MOE_ROWS * ROW_TILE, LANES), f32),
        compiler_params=_cparams(("arbitrary",)),
        name="moe_experts",
    )(block_expert, n_used, xb, w_gate, w_up, w_down)


def _combine_kernel(x_ref, gate_ref, dest_hbm, yb_hbm, o_ref, ybuf, idx_smem, idx_sem, row_sem):
    i = pl.program_id(0)
    slot = i % 2

    def idx_copy(tile, s):
        return pltpu.make_async_copy(dest_hbm.at[tile], idx_smem.at[s], idx_sem.at[s])

    def gather(s):
        _issue_rows(idx_smem, s, lambda k, r, d: pltpu.make_async_copy(
            yb_hbm.at[pl.ds(d, ROW_TILE), :], ybuf.at[s, k, pl.ds(r, ROW_TILE), :], row_sem.at[s]))

    @pl.when(i == 0)
    def _():
        idx_copy(0, 0).start()
        idx_copy(0, 0).wait()
        gather(0)
        idx_copy(1, 1).start()

    for s in range(2):
        @pl.when(jnp.logical_and(i + 1 < N_ROW_TILES, slot == 1 - s))
        def _(s=s):
            idx_copy(i + 1, s).wait()
            gather(s)

    @pl.when(i + 2 < N_ROW_TILES)
    def _():
        idx_copy(i + 2, slot).start()

    g0 = gate_ref[:, 0:1]
    g1 = gate_ref[:, 1:2]
    for s in range(2):
        @pl.when(slot == s)
        def _(s=s):
            for k in range(TOP_K):
                pltpu.make_async_copy(yb_hbm.at[pl.ds(0, TM_ROW * ROW_TILE), :], ybuf.at[s, k],
                                      row_sem.at[s]).wait()
            for c in range(ROW_TILE):
                y0 = ybuf[s, 0, pl.ds(c, TM_ROW, stride=ROW_TILE), :]
                y1 = ybuf[s, 1, pl.ds(c, TM_ROW, stride=ROW_TILE), :]
                lanes = slice(c * LANES, (c + 1) * LANES)
                o_ref[:, lanes] = x_ref[:, lanes] + (y0 * g0 + y1 * g1)


def _combine(x, gate_cols, dest_tiles, yb):
    return pl.pallas_call(
        _combine_kernel,
        grid=(N_ROW_TILES,),
        in_specs=[pl.BlockSpec((TM_ROW, D_MODEL), lambda i: (i, 0)),
                  pl.BlockSpec((TM_ROW, TOP_K), lambda i: (i, 0)),
                  pl.BlockSpec(memory_space=pl.ANY),
                  pl.BlockSpec(memory_space=pl.ANY)],
        out_specs=pl.BlockSpec((TM_ROW, D_MODEL), lambda i: (i, 0)),
        out_shape=jax.ShapeDtypeStruct((N_TOK, D_MODEL), f32),
        scratch_shapes=[pltpu.VMEM((2, TOP_K, TM_ROW * ROW_TILE, LANES), f32),
                        pltpu.SMEM((2, IDX_TRIPS, IDX_PER_TRIP), jnp.int32),
                        pltpu.SemaphoreType.DMA((2,)),
                        pltpu.SemaphoreType.DMA((2,))],
        compiler_params=_cparams(("arbitrary",)),
        name="moe_combine",
    )(x, gate_cols, dest_tiles, yb)


def _hier_moe(layer, x, nw, idx, gates, cnt, w_gate, w_up, w_down):
    counts = cnt[:, 0].astype(jnp.int32)
    padded = (counts + MOE_ROWS - 1) // MOE_ROWS * MOE_ROWS
    pends = jnp.cumsum(padded)
    pstarts = pends - padded
    eids = jnp.arange(N_EXPERTS, dtype=jnp.int32)[:, None, None]
    seg_start = jnp.sum(jnp.where(idx[None, 0:2] == eids, pstarts[:, None, None], 0), axis=0)
    dest = seg_start + idx[2:4]
    dest_tiles = _dest_tiles(dest)
    block_row0 = jnp.arange(N_MOE_BLOCKS, dtype=jnp.int32) * MOE_ROWS
    block_expert = jnp.minimum(
        jnp.sum((pends[None, :] <= block_row0[:, None]).astype(jnp.int32), axis=1), N_EXPERTS - 1)
    n_used = (pends[-1:] // MOE_ROWS).astype(jnp.int32)

    zero_plan = jnp.concatenate([jnp.where(padded > 0, pends - MOE_ROWS, -1), n_used]).astype(jnp.int32)
    xb = _dispatch(x, nw, dest_tiles, zero_plan)
    yb = _experts(layer, block_expert, n_used, xb, w_gate, w_up, w_down)
    return _combine(x, gates[0:2].T, dest_tiles, yb)


def kernel(x, positions, conv_norm_w, conv_w_in, conv_k, conv_w_out, mla_norm_w, mla_w_down, mla_q_lat_norm, mla_kv_lat_norm, mla_w_q_up, mla_w_kv_up, mla_q_head_norm, mla_k_head_norm, mla_w_out, ffn_norm_w, router_w_group, router_b_group, router_w_expert, router_b_expert, exp_w_gate, exp_w_up, exp_w_down):
    half = QK_ROPE_DIM // 2
    inv_freq = ROPE_THETA ** (-jnp.arange(0, QK_ROPE_DIM, 2, dtype=f32) / QK_ROPE_DIM)
    ang_t = inv_freq[None, :, None] * positions.astype(f32)[:, None, :]
    cos_t = jnp.cos(ang_t)
    sin_t = jnp.sin(ang_t)
    cos = cos_t.transpose(0, 2, 1)
    sin = sin_t.transpose(0, 2, 1)
    cs = jnp.concatenate([cos, cos, -sin, sin], axis=-1).reshape(N_TOK, LANES)
    row = lax.broadcasted_iota(jnp.int32, (TM, TM), 0)
    col = lax.broadcasted_iota(jnp.int32, (TM, TM), 1)
    tri = (row < col).astype(bf16)

    xt = x.reshape(N_TOK, D_MODEL)
    for i in range(DEPTH):
        j = i // N_MIXERS
        route_ops = _router_operands(ffn_norm_w[i][None, :], router_w_group[i], router_b_group[i],
                                     router_w_expert[i], router_b_expert[i], tri)
        if i % N_MIXERS == 0:
            z, gb = _conv_in(xt, conv_norm_w[j][None, :], conv_w_in[j].astype(bf16))
            xt, idx, gates, cnt = _conv_out(xt, z, gb, conv_k[j], conv_w_out[j].astype(bf16), route_ops)
        else:
            wd = mla_w_down[j]
            ro = Q_LORA_RANK + KV_LORA_RANK
            wd_aug = jnp.concatenate([wd, wd[:, ro + half:ro + 2 * half], wd[:, ro:ro + half]],
                                     axis=1).astype(bf16)
            wkv = mla_w_kv_up[j].reshape(KV_LORA_RANK, N_HEADS, QK_NOPE_DIM + V_HEAD_DIM)
            wk = wkv[:, :, :QK_NOPE_DIM].reshape(KV_LORA_RANK, N_HEADS * QK_NOPE_DIM).astype(bf16)
            wvt = wkv[:, :, QK_NOPE_DIM:].reshape(KV_LORA_RANK, N_HEADS * V_HEAD_DIM).T.astype(bf16)
            wqt = mla_w_q_up[j].T.astype(bf16)
            khn = mla_k_head_norm[j]
            kw1 = khn[QK_NOPE_DIM:QK_NOPE_DIM + half]
            kw2 = khn[QK_NOPE_DIM + half:]
            g128 = jnp.concatenate([kw1, kw2, kw2, kw1])[None, :]
            qt, k, vt = _mla_proj(xt, mla_norm_w[j][None, :], wd_aug, mla_q_lat_norm[j][None, :],
                                  mla_kv_lat_norm[j][None, :], wqt, wk, wvt,
                                  mla_q_head_norm[j][:, None], khn[None, :], g128, cs, cos_t, sin_t)
            ot = _attention(qt, k, vt)
            xt, idx, gates, cnt = _mla_out(xt, ot.reshape(BATCH, N_HEADS * V_HEAD_DIM, SEQ),
                                           mla_w_out[j].T.astype(bf16), route_ops)
        xt = _hier_moe(i, xt, ffn_norm_w[i][None, :], idx, gates, cnt, exp_w_gate, exp_w_up, exp_w_down)
    return xt.reshape(BATCH, SEQ, D_MODEL)
```

```python
import math

import jax
import jax.numpy as jnp
from jax import lax
from jax.experimental import pallas as pl
from jax.experimental.pallas import tpu as pltpu

D_MODEL = 1024
BATCH = 2
SEQ = 16384
DEPTH = 4
N_TOK = BATCH * SEQ
N_MIXERS = 2
N_HEADS = 8
Q_LORA_RANK = 384
KV_LORA_RANK = 256
QK_NOPE_DIM = 128
QK_ROPE_DIM = 64
QK_HEAD_DIM = QK_NOPE_DIM + QK_ROPE_DIM
V_HEAD_DIM = 128
V_ROWS = V_HEAD_DIM + 16
LATENT_DIM = Q_LORA_RANK + KV_LORA_RANK + QK_ROPE_DIM
LATENT_AUG = LATENT_DIM + QK_ROPE_DIM
ROPE_THETA = 10000.0
N_GROUPS = 8
EXPERTS_PER_GROUP = 8
N_EXPERTS = N_GROUPS * EXPERTS_PER_GROUP
TOP_K = 2
D_FF_EXPERT = 384
RMS_EPS = 1e-6

LANES = 128
VMEM_LIMIT = 56 * 1024 * 1024

TM = 1024
TM_ROW = 512
ROW_UNROLL = 8
MOE_ROWS = 256
N_MOE_BLOCKS = N_TOK * TOP_K // MOE_ROWS + N_EXPERTS
N_MOE_ROWS = N_MOE_BLOCKS * MOE_ROWS
TQ = 2048
TK = 512
QK_SCALE_LOG2E = (QK_HEAD_DIM ** -0.5) * math.log2(math.e)

bf16 = jnp.bfloat16
f32 = jnp.float32


def _cparams(sem):
    return pltpu.CompilerParams(dimension_semantics=sem, vmem_limit_bytes=VMEM_LIMIT)


def _rms(x, w):
    return x * lax.rsqrt(jnp.mean(x * x, axis=-1, keepdims=True) + RMS_EPS) * w


def _dot(a, b):
    return jnp.dot(a, b, preferred_element_type=f32)


def _dot_nt(a, b):
    return lax.dot_general(a, b, (((1,), (1,)), ((), ())), preferred_element_type=f32)


def _conv_in_kernel(x_ref, nw_ref, w_ref, z_ref, gb_ref):
    h = _rms(x_ref[...], nw_ref[...]).astype(bf16)
    gb = _dot(h, w_ref[:, 0:D_MODEL])
    gc = _dot(h, w_ref[:, D_MODEL:2 * D_MODEL])
    u = _dot(h, w_ref[:, 2 * D_MODEL:3 * D_MODEL])
    z_ref[...] = (gc * u).astype(bf16)
    gb_ref[...] = gb.astype(bf16)


def _conv_in(x, nw, w_in):
    return pl.pallas_call(
        _conv_in_kernel,
        grid=(N_TOK // TM,),
        in_specs=[pl.BlockSpec((TM, D_MODEL), lambda i: (i, 0)),
                  pl.BlockSpec((1, D_MODEL), lambda i: (0, 0)),
                  pl.BlockSpec((D_MODEL, 3 * D_MODEL), lambda i: (0, 0))],
        out_specs=[pl.BlockSpec((TM, D_MODEL), lambda i: (i, 0)),
                   pl.BlockSpec((TM, D_MODEL), lambda i: (i, 0))],
        out_shape=[jax.ShapeDtypeStruct((N_TOK, D_MODEL), bf16),
                   jax.ShapeDtypeStruct((N_TOK, D_MODEL), bf16)],
        compiler_params=_cparams(("arbitrary",)),
        name="conv_in",
    )(x, nw, w_in)


def _conv_out_kernel(x_ref, z_ref, zp_ref, zn_ref, gb_ref, k_ref, w_ref, *route_refs):
    o_ref = route_refs[len(ROUTE_IN_SPECS)]
    i = pl.program_id(0)
    tiles_per_seq = SEQ // TM
    first = (i % tiles_per_seq) == 0
    last = (i % tiles_per_seq) == tiles_per_seq - 1
    z = z_ref[...].astype(f32)
    prev_row = jnp.where(first, 0.0, zp_ref[7:8, :].astype(f32))
    next_row = jnp.where(last, 0.0, zn_ref[0:1, :].astype(f32))
    row = lax.broadcasted_iota(jnp.int32, z.shape, 0)
    z_m1 = jnp.where(row == 0, prev_row, pltpu.roll(z, 1, 0))
    z_p1 = jnp.where(row == TM - 1, next_row, pltpu.roll(z, TM - 1, 0))
    conv = z_m1 * k_ref[0:1, :] + z * k_ref[1:2, :] + z_p1 * k_ref[2:3, :]
    y = _dot((gb_ref[...].astype(f32) * conv).astype(bf16), w_ref[...])
    x_new = x_ref[...] + y
    o_ref[...] = x_new
    _route_tile(x_new, *route_refs[:len(ROUTE_IN_SPECS)], *route_refs[len(ROUTE_IN_SPECS) + 1:])


def _conv_out(x, z, gb, conv_k, w_out, route_ops):
    n_halo = N_TOK // 8
    per = TM // 8
    return pl.pallas_call(
        _conv_out_kernel,
        grid=(N_TOK // TM,),
        in_specs=[pl.BlockSpec((TM, D_MODEL), lambda i: (i, 0)),
                  pl.BlockSpec((TM, D_MODEL), lambda i: (i, 0)),
                  pl.BlockSpec((8, D_MODEL), lambda i: (jnp.maximum(i * per - 1, 0), 0)),
                  pl.BlockSpec((8, D_MODEL), lambda i: (jnp.minimum((i + 1) * per, n_halo - 1), 0)),
                  pl.BlockSpec((TM, D_MODEL), lambda i: (i, 0)),
                  pl.BlockSpec((3, D_MODEL), lambda i: (0, 0)),
                  pl.BlockSpec((D_MODEL, D_MODEL), lambda i: (0, 0))] + ROUTE_IN_SPECS,
        out_specs=[pl.BlockSpec((TM, D_MODEL), lambda i: (i, 0))] + ROUTE_OUT_SPECS,
        out_shape=[jax.ShapeDtypeStruct((N_TOK, D_MODEL), f32)] + ROUTE_OUT_SHAPES,
        scratch_shapes=ROUTE_SCRATCH,
        compiler_params=_cparams(("arbitrary",)),
        name="conv_out",
    )(x, z, z, z, gb, conv_k, w_out, *route_ops)


def _mla_proj_kernel(x_ref, nw_ref, wd_ref, qln_ref, kvln_ref, wqt_ref, wk_ref, wvt_ref,
                     qhn_ref, khn_ref, g128_ref, cs_ref, cost_ref, sint_ref,
                     qt_ref, k_ref, vt_ref):
    h = _rms(x_ref[...], nw_ref[...]).astype(bf16)
    lat = _dot(h, wd_ref[...])
    q_lat = _rms(lat[:, 0:Q_LORA_RANK], qln_ref[...]).astype(bf16)
    kv_lat = _rms(lat[:, Q_LORA_RANK:Q_LORA_RANK + KV_LORA_RANK], kvln_ref[...]).astype(bf16)
    kr_blk = lat[:, Q_LORA_RANK + KV_LORA_RANK:LATENT_AUG]
    lane = lax.broadcasted_iota(jnp.int32, kr_blk.shape, 1)
    kr_ss = jnp.sum(jnp.where(lane < QK_ROPE_DIM, kr_blk * kr_blk, 0.0), axis=-1, keepdims=True)
    t = kr_blk * g128_ref[...] * cs_ref[...]
    kr_roped = t + pltpu.roll(t, QK_ROPE_DIM, 1)

    k_nope = _dot(kv_lat, wk_ref[...])
    cos_t = cost_ref[0]
    sin_t = sint_ref[0]
    for hd in range(N_HEADS):
        kn = k_nope[:, hd * QK_NOPE_DIM:(hd + 1) * QK_NOPE_DIM]
        ss = jnp.sum(kn * kn, axis=-1, keepdims=True) + kr_ss
        r = lax.rsqrt(ss * (1.0 / QK_HEAD_DIM) + RMS_EPS)
        k_ref[0, hd, :, 0:QK_NOPE_DIM] = (kn * r * khn_ref[:, 0:QK_NOPE_DIM]).astype(bf16)
        k_ref[0, hd, :, QK_NOPE_DIM:QK_HEAD_DIM] = (kr_roped[:, 0:QK_ROPE_DIM] * r).astype(bf16)

        qt = _dot_nt(wqt_ref[hd * QK_HEAD_DIM:(hd + 1) * QK_HEAD_DIM, :], q_lat)
        rq = lax.rsqrt(jnp.mean(qt * qt, axis=0, keepdims=True) + RMS_EPS)
        qn = qt * rq * qhn_ref[...] * QK_SCALE_LOG2E
        half = QK_ROPE_DIM // 2
        x1 = qn[QK_NOPE_DIM:QK_NOPE_DIM + half, :]
        x2 = qn[QK_NOPE_DIM + half:QK_HEAD_DIM, :]
        qt_ref[0, hd, 0:QK_NOPE_DIM, :] = qn[0:QK_NOPE_DIM, :].astype(bf16)
        qt_ref[0, hd, QK_NOPE_DIM:QK_NOPE_DIM + half, :] = (x1 * cos_t - x2 * sin_t).astype(bf16)
        qt_ref[0, hd, QK_NOPE_DIM + half:QK_HEAD_DIM, :] = (x2 * cos_t + x1 * sin_t).astype(bf16)

        vt = _dot_nt(wvt_ref[hd * V_HEAD_DIM:(hd + 1) * V_HEAD_DIM, :], kv_lat)
        vt_ref[0, hd, 0:V_HEAD_DIM, :] = vt.astype(bf16)
        vt_ref[0, hd, V_HEAD_DIM:V_ROWS, :] = jnp.ones((V_ROWS - V_HEAD_DIM, TM), bf16)


def _mla_proj(x, nw, wd_aug, qln, kvln, wqt, wk, wvt, qhn_col, khn_row, g128, cs, cos_t, sin_t):
    tps = SEQ // TM
    const = lambda i: (0, 0)
    return pl.pallas_call(
        _mla_proj_kernel,
        grid=(N_TOK // TM,),
        in_specs=[pl.BlockSpec((TM, D_MODEL), lambda i: (i, 0)),
                  pl.BlockSpec((1, D_MODEL), const),
                  pl.BlockSpec((D_MODEL, LATENT_AUG), const),
                  pl.BlockSpec((1, Q_LORA_RANK), const),
                  pl.BlockSpec((1, KV_LORA_RANK), const),
                  pl.BlockSpec((N_HEADS * QK_HEAD_DIM, Q_LORA_RANK), const),
                  pl.BlockSpec((KV_LORA_RANK, N_HEADS * QK_NOPE_DIM), const),
                  pl.BlockSpec((N_HEADS * V_HEAD_DIM, KV_LORA_RANK), const),
                  pl.BlockSpec((QK_HEAD_DIM, 1), const),
                  pl.BlockSpec((1, QK_HEAD_DIM), const),
                  pl.BlockSpec((1, LANES), const),
                  pl.BlockSpec((TM, LANES), lambda i: (i, 0)),
                  pl.BlockSpec((1, QK_ROPE_DIM // 2, TM), lambda i: (i // tps, 0, i % tps)),
                  pl.BlockSpec((1, QK_ROPE_DIM // 2, TM), lambda i: (i // tps, 0, i % tps))],
        out_specs=[pl.BlockSpec((1, N_HEADS, QK_HEAD_DIM, TM), lambda i: (i // tps, 0, 0, i % tps)),
                   pl.BlockSpec((1, N_HEADS, TM, QK_HEAD_DIM), lambda i: (i // tps, 0, i % tps, 0)),
                   pl.BlockSpec((1, N_HEADS, V_ROWS, TM), lambda i: (i // tps, 0, 0, i % tps))],
        out_shape=[jax.ShapeDtypeStruct((BATCH, N_HEADS, QK_HEAD_DIM, SEQ), bf16),
                   jax.ShapeDtypeStruct((BATCH, N_HEADS, SEQ, QK_HEAD_DIM), bf16),
                   jax.ShapeDtypeStruct((BATCH, N_HEADS, V_ROWS, SEQ), bf16)],
        compiler_params=_cparams(("arbitrary",)),
        name="mla_proj",
    )(x, nw, wd_aug, qln, kvln, wqt, wk, wvt, qhn_col, khn_row, g128, cs, cos_t, sin_t)


def _attn_kernel(qt_ref, k_ref, vt_ref, o_ref, acc_ref, s0_ref, s1_ref, p0_ref, p1_ref):
    qt = qt_ref[0, 0]
    n = SEQ // TK
    acc_ref[...] = jnp.zeros_like(acc_ref)

    def scores(j, s_ref):
        off = pl.multiple_of(j * TK, TK)
        s = _dot(k_ref[0, 0, pl.ds(off, TK), :], qt)
        s_ref[...] = s
        return jnp.max(s, axis=0, keepdims=True)

    def softmax(s_ref, p_ref, m, cmax):
        m_new = jnp.maximum(m, cmax)
        alpha = jnp.exp2(m - m_new)
        p_ref[...] = jnp.exp2((s_ref[...] - m_new).astype(bf16))
        return m_new, alpha

    def values(j, p_ref, alpha):
        off = pl.multiple_of(j * TK, TK)
        pv = _dot(vt_ref[0, 0, :, pl.ds(off, TK)], p_ref[...])
        acc_ref[...] = alpha * acc_ref[...] + pv

    m = jnp.full((1, TQ), -jnp.inf, f32)
    c0 = scores(0, s0_ref)
    c1 = scores(1, s1_ref)
    m, a0 = softmax(s0_ref, p0_ref, m, c0)
    c0 = scores(2, s0_ref)
    m, a1 = softmax(s1_ref, p1_ref, m, c1)
    values(0, p0_ref, a0)

    def pair(jj, carry):
        m, a_prev, c0 = carry
        j = 2 * jj
        c1 = scores(j + 1, s1_ref)
        m, a0 = softmax(s0_ref, p0_ref, m, c0)
        values(j - 1, p1_ref, a_prev)
        c0 = scores(j + 2, s0_ref)
        m, a1 = softmax(s1_ref, p1_ref, m, c1)
        values(j, p0_ref, a0)
        return m, a1, c0

    m, a_prev, c0 = lax.fori_loop(1, n // 2 - 1, pair, (m, a1, c0))
    c1 = scores(n - 1, s1_ref)
    m, a0 = softmax(s0_ref, p0_ref, m, c0)
    values(n - 3, p1_ref, a_prev)
    m, a1 = softmax(s1_ref, p1_ref, m, c1)
    values(n - 2, p0_ref, a0)
    values(n - 1, p1_ref, a1)
    o_ref[0, 0] = (acc_ref[0:V_HEAD_DIM, :] / acc_ref[V_HEAD_DIM:V_HEAD_DIM + 1, :]).astype(bf16)


def _attention(qt, k, vt):
    return pl.pallas_call(
        _attn_kernel,
        grid=(BATCH, N_HEADS, SEQ // TQ),
        in_specs=[pl.BlockSpec((1, 1, QK_HEAD_DIM, TQ), lambda b, h, i: (b, h, 0, i)),
                  pl.BlockSpec((1, 1, SEQ, QK_HEAD_DIM), lambda b, h, i: (b, h, 0, 0)),
                  pl.BlockSpec((1, 1, V_ROWS, SEQ), lambda b, h, i: (b, h, 0, 0))],
        out_specs=pl.BlockSpec((1, 1, V_HEAD_DIM, TQ), lambda b, h, i: (b, h, 0, i)),
        out_shape=jax.ShapeDtypeStruct((BATCH, N_HEADS, V_HEAD_DIM, SEQ), bf16),
        scratch_shapes=[pltpu.VMEM((V_ROWS, TQ), f32),
                        pltpu.VMEM((TK, TQ), f32), pltpu.VMEM((TK, TQ), f32),
                        pltpu.VMEM((TK, TQ), bf16), pltpu.VMEM((TK, TQ), bf16)],
        compiler_params=_cparams(("arbitrary", "arbitrary", "arbitrary")),
        name="mla_attention",
    )(qt, k, vt)


def _mla_out_kernel(x_ref, ot_ref, w_ref, *route_refs):
    o_ref = route_refs[len(ROUTE_IN_SPECS)]
    yt = _dot(w_ref[...], ot_ref[0])
    x_new = x_ref[...] + yt.T
    o_ref[...] = x_new
    _route_tile(x_new, *route_refs[:len(ROUTE_IN_SPECS)], *route_refs[len(ROUTE_IN_SPECS) + 1:])


def _mla_out(x, ot, w_out_t, route_ops):
    tps = SEQ // TM
    return pl.pallas_call(
        _mla_out_kernel,
        grid=(N_TOK // TM,),
        in_specs=[pl.BlockSpec((TM, D_MODEL), lambda i: (i, 0)),
                  pl.BlockSpec((1, N_HEADS * V_HEAD_DIM, TM), lambda i: (i // tps, 0, i % tps)),
                  pl.BlockSpec((D_MODEL, N_HEADS * V_HEAD_DIM), lambda i: (0, 0))] + ROUTE_IN_SPECS,
        out_specs=[pl.BlockSpec((TM, D_MODEL), lambda i: (i, 0))] + ROUTE_OUT_SPECS,
        out_shape=[jax.ShapeDtypeStruct((N_TOK, D_MODEL), f32)] + ROUTE_OUT_SHAPES,
        scratch_shapes=ROUTE_SCRATCH,
        compiler_params=_cparams(("arbitrary",)),
        name="mla_out",
    )(x, ot, w_out_t, *route_ops)


def _first_argmax(v, n):
    idx = lax.broadcasted_iota(jnp.int32, v.shape, 0)
    mx = jnp.max(v, axis=0, keepdims=True)
    return jnp.min(jnp.where(v == mx, idx, n), axis=0, keepdims=True)


def _route_tile(x, nw_ref, wt_hi_ref, wt_lo_ref, bias_ref, tri_ref,
                idx_ref, gate_ref, cnt_ref, base_ref):
    i = pl.program_id(0)

    @pl.when(i == 0)
    def _():
        base_ref[...] = jnp.zeros_like(base_ref)

    h = _rms(x, nw_ref[...])
    h_hi = h.astype(bf16)
    h_lo = (h - h_hi.astype(f32)).astype(bf16)
    logits = (_dot_nt(wt_hi_ref[...], h_hi) + _dot_nt(wt_lo_ref[...], h_hi)
              + _dot_nt(wt_hi_ref[...], h_lo))
    g_logits = logits[0:N_GROUPS, :]
    g_exp = jnp.exp(g_logits - jnp.max(g_logits, axis=0, keepdims=True))
    g_prob = g_exp / jnp.sum(g_exp, axis=0, keepdims=True)
    g_sel = _first_argmax(g_logits + bias_ref[0:N_GROUPS, :], N_GROUPS)
    gidx = lax.broadcasted_iota(jnp.int32, g_logits.shape, 0)
    g_gate = jnp.sum(jnp.where(gidx == g_sel, g_prob, 0.0), axis=0, keepdims=True)

    e_logits = jnp.zeros((EXPERTS_PER_GROUP, TM), f32)
    e_bias = jnp.zeros((EXPERTS_PER_GROUP, TM), f32)
    for g in range(N_GROUPS):
        lo = N_GROUPS + g * EXPERTS_PER_GROUP
        e_logits = jnp.where(g_sel == g, logits[lo:lo + EXPERTS_PER_GROUP, :], e_logits)
        e_bias = jnp.where(g_sel == g, bias_ref[lo:lo + EXPERTS_PER_GROUP, :], e_bias)
    scored = e_logits + e_bias
    eidx = lax.broadcasted_iota(jnp.int32, scored.shape, 0)
    top0 = _first_argmax(scored, EXPERTS_PER_GROUP)
    top1 = _first_argmax(jnp.where(eidx == top0, -jnp.inf, scored), EXPERTS_PER_GROUP)
    e_exp = jnp.exp(e_logits - jnp.max(e_logits, axis=0, keepdims=True))
    e_prob = e_exp / jnp.sum(e_exp, axis=0, keepdims=True)
    p0 = jnp.sum(jnp.where(eidx == top0, e_prob, 0.0), axis=0, keepdims=True)
    p1 = jnp.sum(jnp.where(eidx == top1, e_prob, 0.0), axis=0, keepdims=True)
    psum = p0 + p1
    gate0 = g_gate * (p0 / psum)
    gate1 = g_gate * (p1 / psum)
    eid0 = g_sel * EXPERTS_PER_GROUP + top0
    eid1 = g_sel * EXPERTS_PER_GROUP + top1

    xidx = lax.broadcasted_iota(jnp.int32, (N_EXPERTS, TM), 0)
    oh0 = xidx == eid0
    oh1 = xidx == eid1
    pre0 = _dot(jnp.where(oh0, 1.0, 0.0).astype(bf16), tri_ref[...])
    pre1 = _dot(jnp.where(oh1, 1.0, 0.0).astype(bf16), tri_ref[...])
    tot0 = jnp.sum(jnp.where(oh0, 1.0, 0.0), axis=1, keepdims=True)
    tot1 = jnp.sum(jnp.where(oh1, 1.0, 0.0), axis=1, keepdims=True)
    base = base_ref[:, 0:1]
    rank0 = jnp.sum(jnp.where(oh0, base + pre0, 0.0), axis=0, keepdims=True)
    rank1 = jnp.sum(jnp.where(oh1, base + tot0 + pre1, 0.0), axis=0, keepdims=True)
    new_base = base + tot0 + tot1
    base_ref[...] = jnp.broadcast_to(new_base, base_ref.shape)
    cnt_ref[...] = jnp.broadcast_to(new_base, cnt_ref.shape)

    zi = jnp.zeros((4, TM), jnp.int32)
    idx_ref[...] = jnp.concatenate(
        [eid0, eid1, rank0.astype(jnp.int32), rank1.astype(jnp.int32), zi], axis=0)
    gate_ref[...] = jnp.concatenate([gate0, gate1, jnp.zeros((6, TM), f32)], axis=0)


def _const2(i):
    return (0, 0)


ROUTE_IN_SPECS = [pl.BlockSpec((1, D_MODEL), _const2),
                  pl.BlockSpec((LANES, D_MODEL), _const2),
                  pl.BlockSpec((LANES, D_MODEL), _const2),
                  pl.BlockSpec((LANES, 1), _const2),
                  pl.BlockSpec((TM, TM), _const2)]
ROUTE_OUT_SPECS = [pl.BlockSpec((8, TM), lambda i: (0, i)),
                   pl.BlockSpec((8, TM), lambda i: (0, i)),
                   pl.BlockSpec((N_EXPERTS, LANES), _const2)]
ROUTE_OUT_SHAPES = [jax.ShapeDtypeStruct((8, N_TOK), jnp.int32),
                    jax.ShapeDtypeStruct((8, N_TOK), f32),
                    jax.ShapeDtypeStruct((N_EXPERTS, LANES), f32)]
ROUTE_SCRATCH = [pltpu.VMEM((N_EXPERTS, LANES), f32)]


def _router_operands(nw, w_group, b_group, w_expert, b_expert, tri):
    wt = jnp.concatenate([w_group, w_expert], axis=1).T
    wt = jnp.pad(wt, ((0, LANES - wt.shape[0]), (0, 0)))
    wt_hi = wt.astype(bf16)
    wt_lo = (wt - wt_hi.astype(f32)).astype(bf16)
    bias_col = jnp.pad(jnp.concatenate([b_group, b_expert]), (0, LANES - N_GROUPS - N_EXPERTS))[:, None]
    return nw, wt_hi, wt_lo, bias_col, tri


N_ROW_TILES = N_TOK // TM_ROW
ROW_TILE = 8
IDX_GROUPS = TM_ROW // LANES
IDX_TRIPS = LANES // ROW_UNROLL
IDX_PER_TRIP = IDX_GROUPS * ROW_UNROLL * TOP_K


def _issue_rows(idx_smem, s, make_copy):
    for r8 in range(IDX_TRIPS):
        for g in range(IDX_GROUPS):
            for u in range(ROW_UNROLL):
                for k in range(TOP_K):
                    d = pl.multiple_of(idx_smem[s, r8, (g * ROW_UNROLL + u) * TOP_K + k] * ROW_TILE,
                                       ROW_TILE)
                    r = (g * LANES + r8 * ROW_UNROLL + u) * ROW_TILE
                    make_copy(k, r, d).start(priority=k)


def _dest_tiles(dest):
    d = dest.reshape(TOP_K, N_ROW_TILES, IDX_GROUPS, IDX_TRIPS, ROW_UNROLL)
    return d.transpose(1, 3, 2, 4, 0).reshape(N_ROW_TILES, IDX_TRIPS, IDX_PER_TRIP)


def _dispatch_kernel(last_ref, x_ref, nw_ref, dest_hbm, xb_hbm, hbuf, zbuf, idx_smem, idx_sem,
                     row_sem, zero_sem):
    i = pl.program_id(0)
    slot = i % 2

    def idx_copy(tile, s):
        return pltpu.make_async_copy(dest_hbm.at[tile], idx_smem.at[s], idx_sem.at[s])

    @pl.when(i == 0)
    def _():
        zbuf[...] = jnp.zeros_like(zbuf)

        def zero_copy(row):
            row0 = pl.multiple_of(row * ROW_TILE, ROW_TILE)
            return pltpu.make_async_copy(zbuf, xb_hbm.at[pl.ds(row0, MOE_ROWS * ROW_TILE), :], zero_sem)

        def start(e, c):
            @pl.when(last_ref[e] >= 0)
            def _():
                zero_copy(last_ref[e]).start()
            return c

        def wait(e, c):
            @pl.when(last_ref[e] >= 0)
            def _():
                zero_copy(last_ref[e]).wait()
            return c

        def start_tail(b, c):
            zero_copy(b * MOE_ROWS).start()
            return c

        def wait_tail(b, c):
            zero_copy(b * MOE_ROWS).wait()
            return c

        n_used = last_ref[N_EXPERTS]
        lax.fori_loop(0, N_EXPERTS, start, 0)
        lax.fori_loop(n_used, N_MOE_BLOCKS, start_tail, 0)
        lax.fori_loop(0, N_EXPERTS, wait, 0)
        lax.fori_loop(n_used, N_MOE_BLOCKS, wait_tail, 0)

    def drain(s):
        for _ in range(TOP_K):
            pltpu.make_async_copy(hbuf.at[s], xb_hbm.at[pl.ds(0, TM_ROW * ROW_TILE), :],
                                  row_sem.at[s]).wait()

    @pl.when(i == 0)
    def _():
        idx_copy(0, 0).start()

    idx_copy(i, slot).wait()

    @pl.when(i + 1 < N_ROW_TILES)
    def _():
        idx_copy(i + 1, 1 - slot).start()

    h = _rms(x_ref[...], nw_ref[...])
    for s in range(2):
        @pl.when(slot == s)
        def _(s=s):
            @pl.when(i >= 2)
            def _():
                drain(s)
            for c in range(ROW_TILE):
                hbuf[s, pl.ds(c, TM_ROW, stride=ROW_TILE), :] = h[:, c * LANES:(c + 1) * LANES]
            _issue_rows(idx_smem, s, lambda k, r, d: pltpu.make_async_copy(
                hbuf.at[s, pl.ds(r, ROW_TILE), :], xb_hbm.at[pl.ds(d, ROW_TILE), :], row_sem.at[s]))

    @pl.when(i == N_ROW_TILES - 1)
    def _():
        drain(0)
        drain(1)


def _dispatch(x, nw, dest_tiles, zero_plan):
    grid_spec = pltpu.PrefetchScalarGridSpec(
        num_scalar_prefetch=1,
        grid=(N_ROW_TILES,),
        in_specs=[pl.BlockSpec((TM_ROW, D_MODEL), lambda i, lb: (i, 0)),
                  pl.BlockSpec((1, D_MODEL), lambda i, lb: (0, 0)),
                  pl.BlockSpec(memory_space=pl.ANY)],
        out_specs=pl.BlockSpec(memory_space=pl.ANY),
        scratch_shapes=[pltpu.VMEM((2, TM_ROW * ROW_TILE, LANES), f32),
                        pltpu.VMEM((MOE_ROWS * ROW_TILE, LANES), f32),
                        pltpu.SMEM((2, IDX_TRIPS, IDX_PER_TRIP), jnp.int32),
                        pltpu.SemaphoreType.DMA((2,)),
                        pltpu.SemaphoreType.DMA((2,)),
                        pltpu.SemaphoreType.DMA])
    return pl.pallas_call(
        _dispatch_kernel,
        grid_spec=grid_spec,
        out_shape=jax.ShapeDtypeStruct((N_MOE_ROWS * ROW_TILE, LANES), f32),
        compiler_params=_cparams(("arbitrary",)),
        name="moe_dispatch",
    )(zero_plan, x, nw, dest_tiles)


def _expert_kernel(be_ref, nused_ref, xb_ref, wg_ref, wu_ref, wd_ref, yb_ref, wg_b, wu_b, wd_b):
    j = pl.program_id(0)
    used = j < nused_ref[0]
    changed = jnp.logical_or(j == 0, be_ref[j] != be_ref[jnp.maximum(j - 1, 0)])

    @pl.when(jnp.logical_and(used, changed))
    def _():
        wg_b[...] = wg_ref[...].astype(bf16)
        wu_b[...] = wu_ref[...].astype(bf16)
        wd_b[...] = wd_ref[...].astype(bf16)

    @pl.when(used)
    def _():
        xb = jnp.concatenate([xb_ref[pl.ds(c, MOE_ROWS, stride=ROW_TILE), :] for c in range(ROW_TILE)],
                             axis=1).astype(bf16)
        hid = jax.nn.silu(_dot(xb, wg_b[...])) * _dot(xb, wu_b[...])
        y = _dot(hid.astype(bf16), wd_b[...])
        for c in range(ROW_TILE):
            yb_ref[pl.ds(c, MOE_ROWS, stride=ROW_TILE), :] = y[:, c * LANES:(c + 1) * LANES]

    @pl.when(jnp.logical_not(used))
    def _():
        yb_ref[...] = jnp.zeros_like(yb_ref)


def _experts(layer, block_expert, n_used, xb, w_gate, w_up, w_down):
    grid_spec = pltpu.PrefetchScalarGridSpec(
        num_scalar_prefetch=2,
        grid=(N_MOE_BLOCKS,),
        in_specs=[pl.BlockSpec((MOE_ROWS * ROW_TILE, LANES),
                               lambda j, be, nu: (jnp.minimum(j, nu[0] - 1), 0)),
                  pl.BlockSpec((None, None, D_MODEL, D_FF_EXPERT), lambda j, be, nu: (layer, be[j], 0, 0)),
                  pl.BlockSpec((None, None, D_MODEL, D_FF_EXPERT), lambda j, be, nu: (layer, be[j], 0, 0)),
                  pl.BlockSpec((None, None, D_FF_EXPERT, D_MODEL), lambda j, be, nu: (layer, be[j], 0, 0))],
        out_specs=pl.BlockSpec((MOE_ROWS * ROW_TILE, LANES), lambda j, be, nu: (j, 0)),
        scratch_shapes=[pltpu.VMEM((D_MODEL, D_FF_EXPERT), bf16),
                        pltpu.VMEM((D_MODEL, D_FF_EXPERT), bf16),
                        pltpu.VMEM((D_FF_EXPERT, D_MODEL), bf16)])
    return pl.pallas_call(
        _expert_kernel,
        grid_spec=grid_spec,
        out_shape=jax.ShapeDtypeStruct((N_MOE_ROWS * ROW_TILE, LANES), f32),
        compiler_params=_cparams(("arbitrary",)),
        name="moe_experts",
    )(block_expert, n_used, xb, w_gate, w_up, w_down)


def _combine_kernel(x_ref, gate_ref, dest_hbm, yb_hbm, o_ref, ybuf, idx_smem, idx_sem, row_sem):
    i = pl.program_id(0)
    slot = i % 2

    def idx_copy(tile, s):
        return pltpu.make_async_copy(dest_hbm.at[tile], idx_smem.at[s], idx_sem.at[s])

    def gather(s):
        _issue_rows(idx_smem, s, lambda k, r, d: pltpu.make_async_copy(
            yb_hbm.at[pl.ds(d, ROW_TILE), :], ybuf.at[s, k, pl.ds(r, ROW_TILE), :], row_sem.at[s]))

    @pl.when(i == 0)
    def _():
        idx_copy(0, 0).start()
        idx_copy(0, 0).wait()
        gather(0)
        idx_copy(1, 1).start()

    for s in range(2):
        @pl.when(jnp.logical_and(i + 1 < N_ROW_TILES, slot == 1 - s))
        def _(s=s):
            idx_copy(i + 1, s).wait()
            gather(s)

    @pl.when(i + 2 < N_ROW_TILES)
    def _():
        idx_copy(i + 2, slot).start()

    g0 = gate_ref[:, 0:1]
    g1 = gate_ref[:, 1:2]
    for s in range(2):
        @pl.when(slot == s)
        def _(s=s):
            for k in range(TOP_K):
                pltpu.make_async_copy(yb_hbm.at[pl.ds(0, TM_ROW * ROW_TILE), :], ybuf.at[s, k],
                                      row_sem.at[s]).wait()
            for c in range(ROW_TILE):
                y0 = ybuf[s, 0, pl.ds(c, TM_ROW, stride=ROW_TILE), :]
                y1 = ybuf[s, 1, pl.ds(c, TM_ROW, stride=ROW_TILE), :]
                lanes = slice(c * LANES, (c + 1) * LANES)
                o_ref[:, lanes] = x_ref[:, lanes] + (y0 * g0 + y1 * g1)


def _combine(x, gate_cols, dest_tiles, yb):
    return pl.pallas_call(
        _combine_kernel,
        grid=(N_ROW_TILES,),
        in_specs=[pl.BlockSpec((TM_ROW, D_MODEL), lambda i: (i, 0)),
                  pl.BlockSpec((TM_ROW, TOP_K), lambda i: (i, 0)),
                  pl.BlockSpec(memory_space=pl.ANY),
                  pl.BlockSpec(memory_space=pl.ANY)],
        out_specs=pl.BlockSpec((TM_ROW, D_MODEL), lambda i: (i, 0)),
        out_shape=jax.ShapeDtypeStruct((N_TOK, D_MODEL), f32),
        scratch_shapes=[pltpu.VMEM((2, TOP_K, TM_ROW * ROW_TILE, LANES), f32),
                        pltpu.SMEM((2, IDX_TRIPS, IDX_PER_TRIP), jnp.int32),
                        pltpu.SemaphoreType.DMA((2,)),
                        pltpu.SemaphoreType.DMA((2,))],
        compiler_params=_cparams(("arbitrary",)),
        name="moe_combine",
    )(x, gate_cols, dest_tiles, yb)


def _hier_moe(layer, x, nw, idx, gates, cnt, w_gate, w_up, w_down):
    counts = cnt[:, 0].astype(jnp.int32)
    padded = (counts + MOE_ROWS - 1) // MOE_ROWS * MOE_ROWS
    pends = jnp.cumsum(padded)
    pstarts = pends - padded
    eids = jnp.arange(N_EXPERTS, dtype=jnp.int32)[:, None, None]
    seg_start = jnp.sum(jnp.where(idx[None, 0:2] == eids, pstarts[:, None, None], 0), axis=0)
    dest = seg_start + idx[2:4]
    dest_tiles = _dest_tiles(dest)
    block_row0 = jnp.arange(N_MOE_BLOCKS, dtype=jnp.int32) * MOE_ROWS
    block_expert = jnp.minimum(
        jnp.sum((pends[None, :] <= block_row0[:, None]).astype(jnp.int32), axis=1), N_EXPERTS - 1)
    n_used = (pends[-1:] // MOE_ROWS).astype(jnp.int32)

    zero_plan = jnp.concatenate([jnp.where(padded > 0, pends - MOE_ROWS, -1), n_used]).astype(jnp.int32)
    xb = _dispatch(x, nw, dest_tiles, zero_plan)
    yb = _experts(layer, block_expert, n_used, xb, w_gate, w_up, w_down)
    return _combine(x, gates[0:2].T, dest_tiles, yb)


def kernel(x, positions, conv_norm_w, conv_w_in, conv_k, conv_w_out, mla_norm_w, mla_w_down, mla_q_lat_norm, mla_kv_lat_norm, mla_w_q_up, mla_w_kv_up, mla_q_head_norm, mla_k_head_norm, mla_w_out, ffn_norm_w, router_w_group, router_b_group, router_w_expert, router_b_expert, exp_w_gate, exp_w_up, exp_w_down):
    half = QK_ROPE_DIM // 2
    inv_freq = ROPE_THETA ** (-jnp.arange(0, QK_ROPE_DIM, 2, dtype=f32) / QK_ROPE_DIM)
    ang_t = inv_freq[None, :, None] * positions.astype(f32)[:, None, :]
    cos_t = jnp.cos(ang_t)
    sin_t = jnp.sin(ang_t)
    cos = cos_t.transpose(0, 2, 1)
    sin = sin_t.transpose(0, 2, 1)
    cs = jnp.concatenate([cos, cos, -sin, sin], axis=-1).reshape(N_TOK, LANES)
    row = lax.broadcasted_iota(jnp.int32, (TM, TM), 0)
    col = lax.broadcasted_iota(jnp.int32, (TM, TM), 1)
    tri = (row < col).astype(bf16)

    xt = x.reshape(N_TOK, D_MODEL)
    for i in range(DEPTH):
        j = i // N_MIXERS
        route_ops = _router_operands(ffn_norm_w[i][None, :], router_w_group[i], router_b_group[i],
                                     router_w_expert[i], router_b_expert[i], tri)
        if i % N_MIXERS == 0:
            z, gb = _conv_in(xt, conv_norm_w[j][None, :], conv_w_in[j].astype(bf16))
            xt, idx, gates, cnt = _conv_out(xt, z, gb, conv_k[j], conv_w_out[j].astype(bf16), route_ops)
        else:
            wd = mla_w_down[j]
            ro = Q_LORA_RANK + KV_LORA_RANK
            wd_aug = jnp.concatenate([wd, wd[:, ro + half:ro + 2 * half], wd[:, ro:ro + half]],
                                     axis=1).astype(bf16)
            wkv = mla_w_kv_up[j].reshape(KV_LORA_RANK, N_HEADS, QK_NOPE_DIM + V_HEAD_DIM)
            wk = wkv[:, :, :QK_NOPE_DIM].reshape(KV_LORA_RANK, N_HEADS * QK_NOPE_DIM).astype(bf16)
            wvt = wkv[:, :, QK_NOPE_DIM:].reshape(KV_LORA_RANK, N_HEADS * V_HEAD_DIM).T.astype(bf16)
            wqt = mla_w_q_up[j].T.astype(bf16)
            khn = mla_k_head_norm[j]
            kw1 = khn[QK_NOPE_DIM:QK_NOPE_DIM + half]
            kw2 = khn[QK_NOPE_DIM + half:]
            g128 = jnp.concatenate([kw1, kw2, kw2, kw1])[None, :]
            qt, k, vt = _mla_proj(xt, mla_norm_w[j][None, :], wd_aug, mla_q_lat_norm[j][None, :],
                                  mla_kv_lat_norm[j][None, :], wqt, wk, wvt,
                                  mla_q_head_norm[j][:, None], khn[None, :], g128, cs, cos_t, sin_t)
            ot = _attention(qt, k, vt)
            xt, idx, gates, cnt = _mla_out(xt, ot.reshape(BATCH, N_HEADS * V_HEAD_DIM, SEQ),
                                           mla_w_out[j].T.astype(bf16), route_ops)
        xt = _hier_moe(i, xt, ffn_norm_w[i][None, :], idx, gates, cnt, exp_w_gate, exp_w_up, exp_w_down)
    return xt.reshape(BATCH, SEQ, D_MODEL)
```

```python
import math

import jax
import jax.numpy as jnp
from jax import lax
from jax.experimental import pallas as pl
from jax.experimental.pallas import tpu as pltpu

D_MODEL = 1024
BATCH = 2
SEQ = 16384
DEPTH = 4
N_TOK = BATCH * SEQ
N_MIXERS = 2
N_HEADS = 8
Q_LORA_RANK = 384
KV_LORA_RANK = 256
QK_NOPE_DIM = 128
QK_ROPE_DIM = 64
QK_HEAD_DIM = QK_NOPE_DIM + QK_ROPE_DIM
V_HEAD_DIM = 128
V_ROWS = V_HEAD_DIM + 16
LATENT_DIM = Q_LORA_RANK + KV_LORA_RANK + QK_ROPE_DIM
LATENT_AUG = LATENT_DIM + QK_ROPE_DIM
ROPE_THETA = 10000.0
N_GROUPS = 8
EXPERTS_PER_GROUP = 8
N_EXPERTS = N_GROUPS * EXPERTS_PER_GROUP
TOP_K = 2
D_FF_EXPERT = 384
RMS_EPS = 1e-6

LANES = 128
VMEM_LIMIT = 56 * 1024 * 1024

TM = 1024
TM_ROW = 512
ROW_UNROLL = 8
MOE_ROWS = 256
N_MOE_BLOCKS = N_TOK * TOP_K // MOE_ROWS + N_EXPERTS
N_MOE_ROWS = N_MOE_BLOCKS * MOE_ROWS
TQ = 2048
TK = 512
QK_SCALE_LOG2E = (QK_HEAD_DIM ** -0.5) * math.log2(math.e)

bf16 = jnp.bfloat16
f32 = jnp.float32


def _cparams(sem):
    return pltpu.CompilerParams(dimension_semantics=sem, vmem_limit_bytes=VMEM_LIMIT)


def _rms(x, w):
    return x * lax.rsqrt(jnp.mean(x * x, axis=-1, keepdims=True) + RMS_EPS) * w


def _dot(a, b):
    return jnp.dot(a, b, preferred_element_type=f32)


def _dot_nt(a, b):
    return lax.dot_general(a, b, (((1,), (1,)), ((), ())), preferred_element_type=f32)


def _conv_in_kernel(x_ref, nw_ref, w_ref, z_ref, gb_ref):
    h = _rms(x_ref[...], nw_ref[...]).astype(bf16)
    gb = _dot(h, w_ref[:, 0:D_MODEL])
    gc = _dot(h, w_ref[:, D_MODEL:2 * D_MODEL])
    u = _dot(h, w_ref[:, 2 * D_MODEL:3 * D_MODEL])
    z_ref[...] = (gc * u).astype(bf16)
    gb_ref[...] = gb.astype(bf16)


def _conv_in(x, nw, w_in):
    return pl.pallas_call(
        _conv_in_kernel,
        grid=(N_TOK // TM,),
        in_specs=[pl.BlockSpec((TM, D_MODEL), lambda i: (i, 0)),
                  pl.BlockSpec((1, D_MODEL), lambda i: (0, 0)),
                  pl.BlockSpec((D_MODEL, 3 * D_MODEL), lambda i: (0, 0))],
        out_specs=[pl.BlockSpec((TM, D_MODEL), lambda i: (i, 0)),
                   pl.BlockSpec((TM, D_MODEL), lambda i: (i, 0))],
        out_shape=[jax.ShapeDtypeStruct((N_TOK, D_MODEL), bf16),
                   jax.ShapeDtypeStruct((N_TOK, D_MODEL), bf16)],
        compiler_params=_cparams(("arbitrary",)),
        name="conv_in",
    )(x, nw, w_in)


def _conv_out_kernel(x_ref, z_ref, zp_ref, zn_ref, gb_ref, k_ref, w_ref, *route_refs):
    o_ref = route_refs[len(ROUTE_IN_SPECS)]
    i = pl.program_id(0)
    tiles_per_seq = SEQ // TM
    first = (i % tiles_per_seq) == 0
    last = (i % tiles_per_seq) == tiles_per_seq - 1
    z = z_ref[...].astype(f32)
    prev_row = jnp.where(first, 0.0, zp_ref[7:8, :].astype(f32))
    next_row = jnp.where(last, 0.0, zn_ref[0:1, :].astype(f32))
    row = lax.broadcasted_iota(jnp.int32, z.shape, 0)
    z_m1 = jnp.where(row == 0, prev_row, pltpu.roll(z, 1, 0))
    z_p1 = jnp.where(row == TM - 1, next_row, pltpu.roll(z, TM - 1, 0))
    conv = z_m1 * k_ref[0:1, :] + z * k_ref[1:2, :] + z_p1 * k_ref[2:3, :]
    y = _dot((gb_ref[...].astype(f32) * conv).astype(bf16), w_ref[...])
    x_new = x_ref[...] + y
    o_ref[...] = x_new
    _route_tile(x_new, *route_refs[:len(ROUTE_IN_SPECS)], *route_refs[len(ROUTE_IN_SPECS) + 1:])


def _conv_out(x, z, gb, conv_k, w_out, route_ops):
    n_halo = N_TOK // 8
    per = TM // 8
    return pl.pallas_call(
        _conv_out_kernel,
        grid=(N_TOK // TM,),
        in_specs=[pl.BlockSpec((TM, D_MODEL), lambda i: (i, 0)),
                  pl.BlockSpec((TM, D_MODEL), lambda i: (i, 0)),
                  pl.BlockSpec((8, D_MODEL), lambda i: (jnp.maximum(i * per - 1, 0), 0)),
                  pl.BlockSpec((8, D_MODEL), lambda i: (jnp.minimum((i + 1) * per, n_halo - 1), 0)),
                  pl.BlockSpec((TM, D_MODEL), lambda i: (i, 0)),
                  pl.BlockSpec((3, D_MODEL), lambda i: (0, 0)),
                  pl.BlockSpec((D_MODEL, D_MODEL), lambda i: (0, 0))] + ROUTE_IN_SPECS,
        out_specs=[pl.BlockSpec((TM, D_MODEL), lambda i: (i, 0))] + ROUTE_OUT_SPECS,
        out_shape=[jax.ShapeDtypeStruct((N_TOK, D_MODEL), f32)] + ROUTE_OUT_SHAPES,
        scratch_shapes=ROUTE_SCRATCH,
        compiler_params=_cparams(("arbitrary",)),
        name="conv_out",
    )(x, z, z, z, gb, conv_k, w_out, *route_ops)


def _mla_proj_kernel(x_ref, nw_ref, wd_ref, qln_ref, kvln_ref, wqt_ref, wk_ref, wvt_ref,
                     qhn_ref, khn_ref, g128_ref, cs_ref, cost_ref, sint_ref,
                     qt_ref, k_ref, vt_ref):
    h = _rms(x_ref[...], nw_ref[...]).astype(bf16)
    lat = _dot(h, wd_ref[...])
    q_lat = _rms(lat[:, 0:Q_LORA_RANK], qln_ref[...]).astype(bf16)
    kv_lat = _rms(lat[:, Q_LORA_RANK:Q_LORA_RANK + KV_LORA_RANK], kvln_ref[...]).astype(bf16)
    kr_blk = lat[:, Q_LORA_RANK + KV_LORA_RANK:LATENT_AUG]
    lane = lax.broadcasted_iota(jnp.int32, kr_blk.shape, 1)
    kr_ss = jnp.sum(jnp.where(lane < QK_ROPE_DIM, kr_blk * kr_blk, 0.0), axis=-1, keepdims=True)
    t = kr_blk * g128_ref[...] * cs_ref[...]
    kr_roped = t + pltpu.roll(t, QK_ROPE_DIM, 1)

    k_nope = _dot(kv_lat, wk_ref[...])
    cos_t = cost_ref[0]
    sin_t = sint_ref[0]
    for hd in range(N_HEADS):
        kn = k_nope[:, hd * QK_NOPE_DIM:(hd + 1) * QK_NOPE_DIM]
        ss = jnp.sum(kn * kn, axis=-1, keepdims=True) + kr_ss
        r = lax.rsqrt(ss * (1.0 / QK_HEAD_DIM) + RMS_EPS)
        k_ref[0, hd, :, 0:QK_NOPE_DIM] = (kn * r * khn_ref[:, 0:QK_NOPE_DIM]).astype(bf16)
        k_ref[0, hd, :, QK_NOPE_DIM:QK_HEAD_DIM] = (kr_roped[:, 0:QK_ROPE_DIM] * r).astype(bf16)

        qt = _dot_nt(wqt_ref[hd * QK_HEAD_DIM:(hd + 1) * QK_HEAD_DIM, :], q_lat)
        rq = lax.rsqrt(jnp.mean(qt * qt, axis=0, keepdims=True) + RMS_EPS)
        qn = qt * rq * qhn_ref[...] * QK_SCALE_LOG2E
        half = QK_ROPE_DIM // 2
        x1 = qn[QK_NOPE_DIM:QK_NOPE_DIM + half, :]
        x2 = qn[QK_NOPE_DIM + half:QK_HEAD_DIM, :]
        qt_ref[0, hd, 0:QK_NOPE_DIM, :] = qn[0:QK_NOPE_DIM, :].astype(bf16)
        qt_ref[0, hd, QK_NOPE_DIM:QK_NOPE_DIM + half, :] = (x1 * cos_t - x2 * sin_t).astype(bf16)
        qt_ref[0, hd, QK_NOPE_DIM + half:QK_HEAD_DIM, :] = (x2 * cos_t + x1 * sin_t).astype(bf16)

        vt = _dot_nt(wvt_ref[hd * V_HEAD_DIM:(hd + 1) * V_HEAD_DIM, :], kv_lat)
        vt_ref[0, hd, 0:V_HEAD_DIM, :] = vt.astype(bf16)
        vt_ref[0, hd, V_HEAD_DIM:V_ROWS, :] = jnp.ones((V_ROWS - V_HEAD_DIM, TM), bf16)


def _mla_proj(x, nw, wd_aug, qln, kvln, wqt, wk, wvt, qhn_col, khn_row, g128, cs, cos_t, sin_t):
    tps = SEQ // TM
    const = lambda i: (0, 0)
    return pl.pallas_call(
        _mla_proj_kernel,
        grid=(N_TOK // TM,),
        in_specs=[pl.BlockSpec((TM, D_MODEL), lambda i: (i, 0)),
                  pl.BlockSpec((1, D_MODEL), const),
                  pl.BlockSpec((D_MODEL, LATENT_AUG), const),
                  pl.BlockSpec((1, Q_LORA_RANK), const),
                  pl.BlockSpec((1, KV_LORA_RANK), const),
                  pl.BlockSpec((N_HEADS * QK_HEAD_DIM, Q_LORA_RANK), const),
                  pl.BlockSpec((KV_LORA_RANK, N_HEADS * QK_NOPE_DIM), const),
                  pl.BlockSpec((N_HEADS * V_HEAD_DIM, KV_LORA_RANK), const),
                  pl.BlockSpec((QK_HEAD_DIM, 1), const),
                  pl.BlockSpec((1, QK_HEAD_DIM), const),
                  pl.BlockSpec((1, LANES), const),
                  pl.BlockSpec((TM, LANES), lambda i: (i, 0)),
                  pl.BlockSpec((1, QK_ROPE_DIM // 2, TM), lambda i: (i // tps, 0, i % tps)),
                  pl.BlockSpec((1, QK_ROPE_DIM // 2, TM), lambda i: (i // tps, 0, i % tps))],
        out_specs=[pl.BlockSpec((1, N_HEADS, QK_HEAD_DIM, TM), lambda i: (i // tps, 0, 0, i % tps)),
                   pl.BlockSpec((1, N_HEADS, TM, QK_HEAD_DIM), lambda i: (i // tps, 0, i % tps, 0)),
                   pl.BlockSpec((1, N_HEADS, V_ROWS, TM), lambda i: (i // tps, 0, 0, i % tps))],
        out_shape=[jax.ShapeDtypeStruct((BATCH, N_HEADS, QK_HEAD_DIM, SEQ), bf16),
                   jax.ShapeDtypeStruct((BATCH, N_HEADS, SEQ, QK_HEAD_DIM), bf16),
                   jax.ShapeDtypeStruct((BATCH, N_HEADS, V_ROWS, SEQ), bf16)],
        compiler_params=_cparams(("arbitrary",)),
        name="mla_proj",
    )(x, nw, wd_aug, qln, kvln, wqt, wk, wvt, qhn_col, khn_row, g128, cs, cos_t, sin_t)


def _attn_kernel(qt_ref, k_ref, vt_ref, o_ref, acc_ref, s0_ref, s1_ref, p0_ref, p1_ref):
    qt = qt_ref[0, 0]
    n = SEQ // TK
    acc_ref[...] = jnp.zeros_like(acc_ref)

    def scores(j, s_ref):
        off = pl.multiple_of(j * TK, TK)
        s = _dot(k_ref[0, 0, pl.ds(off, TK), :], qt)
        s_ref[...] = s
        return jnp.max(s, axis=0, keepdims=True)

    def softmax(s_ref, p_ref, m, cmax):
        m_new = jnp.maximum(m, cmax)
        alpha = jnp.exp2(m - m_new)
        p_ref[...] = jnp.exp2((s_ref[...] - m_new).astype(bf16))
        return m_new, alpha

    def values(j, p_ref, alpha):
        off = pl.multiple_of(j * TK, TK)
        pv = _dot(vt_ref[0, 0, :, pl.ds(off, TK)], p_ref[...])
        acc_ref[...] = alpha * acc_ref[...] + pv

    m = jnp.full((1, TQ), -jnp.inf, f32)
    c0 = scores(0, s0_ref)
    c1 = scores(1, s1_ref)
    m, a0 = softmax(s0_ref, p0_ref, m, c0)
    c0 = scores(2, s0_ref)
    m, a1 = softmax(s1_ref, p1_ref, m, c1)
    values(0, p0_ref, a0)

    def pair(jj, carry):
        m, a_prev, c0 = carry
        j = 2 * jj
        c1 = scores(j + 1, s1_ref)
        m, a0 = softmax(s0_ref, p0_ref, m, c0)
        values(j - 1, p1_ref, a_prev)
        c0 = scores(j + 2, s0_ref)
        m, a1 = softmax(s1_ref, p1_ref, m, c1)
        values(j, p0_ref, a0)
        return m, a1, c0

    m, a_prev, c0 = lax.fori_loop(1, n // 2 - 1, pair, (m, a1, c0))
    c1 = scores(n - 1, s1_ref)
    m, a0 = softmax(s0_ref, p0_ref, m, c0)
    values(n - 3, p1_ref, a_prev)
    m, a1 = softmax(s1_ref, p1_ref, m, c1)
    values(n - 2, p0_ref, a0)
    values(n - 1, p1_ref, a1)
    o_ref[0, 0] = (acc_ref[0:V_HEAD_DIM, :] / acc_ref[V_HEAD_DIM:V_HEAD_DIM + 1, :]).astype(bf16)


def _attention(qt, k, vt):
    return pl.pallas_call(
        _attn_kernel,
        grid=(BATCH, N_HEADS, SEQ // TQ),
        in_specs=[pl.BlockSpec((1, 1, QK_HEAD_DIM, TQ), lambda b, h, i: (b, h, 0, i)),
                  pl.BlockSpec((1, 1, SEQ, QK_HEAD_DIM), lambda b, h, i: (b, h, 0, 0)),
                  pl.BlockSpec((1, 1, V_ROWS, SEQ), lambda b, h, i: (b, h, 0, 0))],
        out_specs=pl.BlockSpec((1, 1, V_HEAD_DIM, TQ), lambda b, h, i: (b, h, 0, i)),
        out_shape=jax.ShapeDtypeStruct((BATCH, N_HEADS, V_HEAD_DIM, SEQ), bf16),
        scratch_shapes=[pltpu.VMEM((V_ROWS, TQ), f32),
                        pltpu.VMEM((TK, TQ), f32), pltpu.VMEM((TK, TQ), f32),
                        pltpu.VMEM((TK, TQ), bf16), pltpu.VMEM((TK, TQ), bf16)],
        compiler_params=_cparams(("arbitrary", "arbitrary", "arbitrary")),
        name="mla_attention",
    )(qt, k, vt)


def _mla_out_kernel(x_ref, ot_ref, w_ref, *route_refs):
    o_ref = route_refs[len(ROUTE_IN_SPECS)]
    yt = _dot(w_ref[...], ot_ref[0])
    x_new = x_ref[...] + yt.T
    o_ref[...] = x_new
    _route_tile(x_new, *route_refs[:len(ROUTE_IN_SPECS)], *route_refs[len(ROUTE_IN_SPECS) + 1:])


def _mla_out(x, ot, w_out_t, route_ops):
    tps = SEQ // TM
    return pl.pallas_call(
        _mla_out_kernel,
        grid=(N_TOK // TM,),
        in_specs=[pl.BlockSpec((TM, D_MODEL), lambda i: (i, 0)),
                  pl.BlockSpec((1, N_HEADS * V_HEAD_DIM, TM), lambda i: (i // tps, 0, i % tps)),
                  pl.BlockSpec((D_MODEL, N_HEADS * V_HEAD_DIM), lambda i: (0, 0))] + ROUTE_IN_SPECS,
        out_specs=[pl.BlockSpec((TM, D_MODEL), lambda i: (i, 0))] + ROUTE_OUT_SPECS,
        out_shape=[jax.ShapeDtypeStruct((N_TOK, D_MODEL), f32)] + ROUTE_OUT_SHAPES,
        scratch_shapes=ROUTE_SCRATCH,
        compiler_params=_cparams(("arbitrary",)),
        name="mla_out",
    )(x, ot, w_out_t, *route_ops)


def _first_argmax(v, n):
    idx = lax.broadcasted_iota(jnp.int32, v.shape, 0)
    mx = jnp.max(v, axis=0, keepdims=True)
    return jnp.min(jnp.where(v == mx, idx, n), axis=0, keepdims=True)


def _route_tile(x, nw_ref, wt_hi_ref, wt_lo_ref, bias_ref, tri_ref,
                idx_ref, gate_ref, cnt_ref, base_ref):
    i = pl.program_id(0)

    @pl.when(i == 0)
    def _():
        base_ref[...] = jnp.zeros_like(base_ref)

    h = _rms(x, nw_ref[...])
    h_hi = h.astype(bf16)
    h_lo = (h - h_hi.astype(f32)).astype(bf16)
    logits = (_dot_nt(wt_hi_ref[...], h_hi) + _dot_nt(wt_lo_ref[...], h_hi)
              + _dot_nt(wt_hi_ref[...], h_lo))
    g_logits = logits[0:N_GROUPS, :]
    g_exp = jnp.exp(g_logits - jnp.max(g_logits, axis=0, keepdims=True))
    g_prob = g_exp / jnp.sum(g_exp, axis=0, keepdims=True)
    g_sel = _first_argmax(g_logits + bias_ref[0:N_GROUPS, :], N_GROUPS)
    gidx = lax.broadcasted_iota(jnp.int32, g_logits.shape, 0)
    g_gate = jnp.sum(jnp.where(gidx == g_sel, g_prob, 0.0), axis=0, keepdims=True)

    e_logits = jnp.zeros((EXPERTS_PER_GROUP, TM), f32)
    e_bias = jnp.zeros((EXPERTS_PER_GROUP, TM), f32)
    for g in range(N_GROUPS):
        lo = N_GROUPS + g * EXPERTS_PER_GROUP
        e_logits = jnp.where(g_sel == g, logits[lo:lo + EXPERTS_PER_GROUP, :], e_logits)
        e_bias = jnp.where(g_sel == g, bias_ref[lo:lo + EXPERTS_PER_GROUP, :], e_bias)
    scored = e_logits + e_bias
    eidx = lax.broadcasted_iota(jnp.int32, scored.shape, 0)
    top0 = _first_argmax(scored, EXPERTS_PER_GROUP)
    top1 = _first_argmax(jnp.where(eidx == top0, -jnp.inf, scored), EXPERTS_PER_GROUP)
    e_exp = jnp.exp(e_logits - jnp.max(e_logits, axis=0, keepdims=True))
    e_prob = e_exp / jnp.sum(e_exp, axis=0, keepdims=True)
    p0 = jnp.sum(jnp.where(eidx == top0, e_prob, 0.0), axis=0, keepdims=True)
    p1 = jnp.sum(jnp.where(eidx == top1, e_prob, 0.0), axis=0, keepdims=True)
    psum = p0 + p1
    gate0 = g_gate * (p0 / psum)
    gate1 = g_gate * (p1 / psum)
    eid0 = g_sel * EXPERTS_PER_GROUP + top0
    eid1 = g_sel * EXPERTS_PER_GROUP + top1

    xidx = lax.broadcasted_iota(jnp.int32, (N_EXPERTS, TM), 0)
    oh0 = xidx == eid0
    oh1 = xidx == eid1
    pre0 = _dot(jnp.where(oh0, 1.0, 0.0).astype(bf16), tri_ref[...])
    pre1 = _dot(jnp.where(oh1, 1.0, 0.0).astype(bf16), tri_ref[...])
    tot0 = jnp.sum(jnp.where(oh0, 1.0, 0.0), axis=1, keepdims=True)
    tot1 = jnp.sum(jnp.where(oh1, 1.0, 0.0), axis=1, keepdims=True)
    base = base_ref[:, 0:1]
    rank0 = jnp.sum(jnp.where(oh0, base + pre0, 0.0), axis=0, keepdims=True)
    rank1 = jnp.sum(jnp.where(oh1, base + tot0 + pre1, 0.0), axis=0, keepdims=True)
    new_base = base + tot0 + tot1
    base_ref[...] = jnp.broadcast_to(new_base, base_ref.shape)
    cnt_ref[...] = jnp.broadcast_to(new_base, cnt_ref.shape)

    zi = jnp.zeros((4, TM), jnp.int32)
    idx_ref[...] = jnp.concatenate(
        [eid0, eid1, rank0.astype(jnp.int32), rank1.astype(jnp.int32), zi], axis=0)
    gate_ref[...] = jnp.concatenate([gate0, gate1, jnp.zeros((6, TM), f32)], axis=0)


def _const2(i):
    return (0, 0)


ROUTE_IN_SPECS = [pl.BlockSpec((1, D_MODEL), _const2),
                  pl.BlockSpec((LANES, D_MODEL), _const2),
                  pl.BlockSpec((LANES, D_MODEL), _const2),
                  pl.BlockSpec((LANES, 1), _const2),
                  pl.BlockSpec((TM, TM), _const2)]
ROUTE_OUT_SPECS = [pl.BlockSpec((8, TM), lambda i: (0, i)),
                   pl.BlockSpec((8, TM), lambda i: (0, i)),
                   pl.BlockSpec((N_EXPERTS, LANES), _const2)]
ROUTE_OUT_SHAPES = [jax.ShapeDtypeStruct((8, N_TOK), jnp.int32),
                    jax.ShapeDtypeStruct((8, N_TOK), f32),
                    jax.ShapeDtypeStruct((N_EXPERTS, LANES), f32)]
ROUTE_SCRATCH = [pltpu.VMEM((N_EXPERTS, LANES), f32)]


def _router_operands(nw, w_group, b_group, w_expert, b_expert, tri):
    wt = jnp.concatenate([w_group, w_expert], axis=1).T
    wt = jnp.pad(wt, ((0, LANES - wt.shape[0]), (0, 0)))
    wt_hi = wt.astype(bf16)
    wt_lo = (wt - wt_hi.astype(f32)).astype(bf16)
    bias_col = jnp.pad(jnp.concatenate([b_group, b_expert]), (0, LANES - N_GROUPS - N_EXPERTS))[:, None]
    return nw, wt_hi, wt_lo, bias_col, tri


N_ROW_TILES = N_TOK // TM_ROW
ROW_TILE = 8
IDX_GROUPS = TM_ROW // LANES
IDX_TRIPS = LANES // ROW_UNROLL
IDX_PER_TRIP = IDX_GROUPS * ROW_UNROLL * TOP_K


def _issue_rows(idx_smem, s, make_copy):
    for r8 in range(IDX_TRIPS):
        for g in range(IDX_GROUPS):
            for u in range(ROW_UNROLL):
                for k in range(TOP_K):
                    d = pl.multiple_of(idx_smem[s, r8, (g * ROW_UNROLL + u) * TOP_K + k] * ROW_TILE,
                                       ROW_TILE)
                    r = (g * LANES + r8 * ROW_UNROLL + u) * ROW_TILE
                    make_copy(k, r, d).start(priority=k)


def _dest_tiles(dest):
    d = dest.reshape(TOP_K, N_ROW_TILES, IDX_GROUPS, IDX_TRIPS, ROW_UNROLL)
    return d.transpose(1, 3, 2, 4, 0).reshape(N_ROW_TILES, IDX_TRIPS, IDX_PER_TRIP)


def _dispatch_kernel(last_ref, x_ref, nw_ref, dest_hbm, xb_hbm, hbuf, zbuf, idx_smem, idx_sem,
                     row_sem, zero_sem):
    i = pl.program_id(0)
    slot = i % 2

    def idx_copy(tile, s):
        return pltpu.make_async_copy(dest_hbm.at[tile], idx_smem.at[s], idx_sem.at[s])

    @pl.when(i == 0)
    def _():
        zbuf[...] = jnp.zeros_like(zbuf)

        def zero_copy(row):
            row0 = pl.multiple_of(row * ROW_TILE, ROW_TILE)
            return pltpu.make_async_copy(zbuf, xb_hbm.at[pl.ds(row0, MOE_ROWS * ROW_TILE), :], zero_sem)

        def start(e, c):
            @pl.when(last_ref[e] >= 0)
            def _():
                zero_copy(last_ref[e]).start()
            return c

        def wait(e, c):
            @pl.when(last_ref[e] >= 0)
            def _():
                zero_copy(last_ref[e]).wait()
            return c

        def start_tail(b, c):
            zero_copy(b * MOE_ROWS).start()
            return c

        def wait_tail(b, c):
            zero_copy(b * MOE_ROWS).wait()
            return c

        n_used = last_ref[N_EXPERTS]
        lax.fori_loop(0, N_EXPERTS, start, 0)
        lax.fori_loop(n_used, N_MOE_BLOCKS, start_tail, 0)
        lax.fori_loop(0, N_EXPERTS, wait, 0)
        lax.fori_loop(n_used, N_MOE_BLOCKS, wait_tail, 0)

    def drain(s):
        for _ in range(TOP_K):
            pltpu.make_async_copy(hbuf.at[s], xb_hbm.at[pl.ds(0, TM_ROW * ROW_TILE), :],
                                  row_sem.at[s]).wait()

    @pl.when(i == 0)
    def _():
        idx_copy(0, 0).start()

    idx_copy(i, slot).wait()

    @pl.when(i + 1 < N_ROW_TILES)
    def _():
        idx_copy(i + 1, 1 - slot).start()

    h = _rms(x_ref[...], nw_ref[...])
    for s in range(2):
        @pl.when(slot == s)
        def _(s=s):
            @pl.when(i >= 2)
            def _():
                drain(s)
            for c in range(ROW_TILE):
                hbuf[s, pl.ds(c, TM_ROW, stride=ROW_TILE), :] = h[:, c * LANES:(c + 1) * LANES]
            _issue_rows(idx_smem, s, lambda k, r, d: pltpu.make_async_copy(
                hbuf.at[s, pl.ds(r, ROW_TILE), :], xb_hbm.at[pl.ds(d, ROW_TILE), :], row_sem.at[s]))

    @pl.when(i == N_ROW_TILES - 1)
    def _():
        drain(0)
        drain(1)


def _dispatch(x, nw, dest_tiles, zero_plan):
    grid_spec = pltpu.PrefetchScalarGridSpec(
        num_scalar_prefetch=1,
        grid=(N_ROW_TILES,),
        in_specs=[pl.BlockSpec((TM_ROW, D_MODEL), lambda i, lb: (i, 0)),
                  pl.BlockSpec((1, D_MODEL), lambda i, lb: (0, 0)),
                  pl.BlockSpec(memory_space=pl.ANY)],
        out_specs=pl.BlockSpec(memory_space=pl.ANY),
        scratch_shapes=[pltpu.VMEM((2, TM_ROW * ROW_TILE, LANES), f32),
                        pltpu.VMEM((MOE_ROWS * ROW_TILE, LANES), f32),
                        pltpu.SMEM((2, IDX_TRIPS, IDX_PER_TRIP), jnp.int32),
                        pltpu.SemaphoreType.DMA((2,)),
                        pltpu.SemaphoreType.DMA((2,)),
                        pltpu.SemaphoreType.DMA])
    return pl.pallas_call(
        _dispatch_kernel,
        grid_spec=grid_spec,
        out_shape=jax.ShapeDtypeStruct((N_MOE_ROWS * ROW_TILE, LANES), f32),
        compiler_params=_cparams(("arbitrary",)),
        name="moe_dispatch",
    )(zero_plan, x, nw, dest_tiles)


MOE_BLOCK_SUBLANES = MOE_ROWS * ROW_TILE


def _expert_kernel(first_ref, nblk_ref, nused_ref, xb_hbm, wg_ref, wu_ref, wd_ref, yb_hbm,
                   xbuf, ybuf, wg_b, wu_b, wd_b, in_sem, out_sem, zero_sem):
    e = pl.program_id(0)
    n_used = nused_ref[0]

    def block_rows(g):
        return pl.ds(pl.multiple_of(g * MOE_BLOCK_SUBLANES, MOE_BLOCK_SUBLANES), MOE_BLOCK_SUBLANES)

    def x_copy(g, s):
        return pltpu.make_async_copy(xb_hbm.at[block_rows(g), :], xbuf.at[s], in_sem.at[s])

    def y_copy(g, s):
        return pltpu.make_async_copy(ybuf.at[s], yb_hbm.at[block_rows(g), :], out_sem.at[s])

    @pl.when(e == 0)
    def _():
        x_copy(0, 0).start()

    @pl.when(nblk_ref[e] > 0)
    def _():
        wg_b[...] = wg_ref[...].astype(bf16)
        wu_b[...] = wu_ref[...].astype(bf16)
        wd_b[...] = wd_ref[...].astype(bf16)

    def block(i, c):
        g = first_ref[e] + i
        s = g % 2
        x_copy(g, s).wait()

        @pl.when(g + 1 < n_used)
        def _():
            x_copy(g + 1, 1 - s).start()

        @pl.when(g >= 2)
        def _():
            y_copy(g - 2, s).wait()

        xb = jnp.concatenate([xbuf[s, pl.ds(k, MOE_ROWS, stride=ROW_TILE), :] for k in range(ROW_TILE)],
                             axis=1).astype(bf16)
        hid = jax.nn.silu(_dot(xb, wg_b[...])) * _dot(xb, wu_b[...])
        y = _dot(hid.astype(bf16), wd_b[...])
        for k in range(ROW_TILE):
            ybuf[s, pl.ds(k, MOE_ROWS, stride=ROW_TILE), :] = y[:, k * LANES:(k + 1) * LANES]
        y_copy(g, s).start()
        return c

    lax.fori_loop(0, nblk_ref[e], block, 0)

    @pl.when(e == N_EXPERTS - 1)
    def _():
        @pl.when(n_used >= 2)
        def _():
            y_copy(n_used - 2, n_used % 2).wait()
        y_copy(n_used - 1, (n_used + 1) % 2).wait()
        ybuf[0] = jnp.zeros((MOE_BLOCK_SUBLANES, LANES), f32)

        def zero_copy(g):
            return pltpu.make_async_copy(ybuf.at[0], yb_hbm.at[block_rows(g), :], zero_sem)

        def start_tail(g, c):
            zero_copy(g).start()
            return c

        def wait_tail(g, c):
            zero_copy(g).wait()
            return c

        lax.fori_loop(n_used, N_MOE_BLOCKS, start_tail, 0)
        lax.fori_loop(n_used, N_MOE_BLOCKS, wait_tail, 0)


def _experts(layer, first_block, n_blocks, n_used, xb, w_gate, w_up, w_down):
    grid_spec = pltpu.PrefetchScalarGridSpec(
        num_scalar_prefetch=3,
        grid=(N_EXPERTS,),
        in_specs=[pl.BlockSpec(memory_space=pl.ANY),
                  pl.BlockSpec((None, None, D_MODEL, D_FF_EXPERT), lambda e, fb, nb, nu: (layer, e, 0, 0)),
                  pl.BlockSpec((None, None, D_MODEL, D_FF_EXPERT), lambda e, fb, nb, nu: (layer, e, 0, 0)),
                  pl.BlockSpec((None, None, D_FF_EXPERT, D_MODEL), lambda e, fb, nb, nu: (layer, e, 0, 0))],
        out_specs=pl.BlockSpec(memory_space=pl.ANY),
        scratch_shapes=[pltpu.VMEM((2, MOE_BLOCK_SUBLANES, LANES), f32),
                        pltpu.VMEM((2, MOE_BLOCK_SUBLANES, LANES), f32),
                        pltpu.VMEM((D_MODEL, D_FF_EXPERT), bf16),
                        pltpu.VMEM((D_MODEL, D_FF_EXPERT), bf16),
                        pltpu.VMEM((D_FF_EXPERT, D_MODEL), bf16),
                        pltpu.SemaphoreType.DMA((2,)),
                        pltpu.SemaphoreType.DMA((2,)),
                        pltpu.SemaphoreType.DMA])
    return pl.pallas_call(
        _expert_kernel,
        grid_spec=grid_spec,
        out_shape=jax.ShapeDtypeStruct((N_MOE_ROWS * ROW_TILE, LANES), f32),
        compiler_params=_cparams(("arbitrary",)),
        name="moe_experts",
    )(first_block, n_blocks, n_used, xb, w_gate, w_up, w_down)


def _combine_kernel(x_ref, gate_ref, dest_hbm, yb_hbm, o_ref, ybuf, idx_smem, idx_sem, row_sem):
    i = pl.program_id(0)
    slot = i % 2

    def idx_copy(tile, s):
        return pltpu.make_async_copy(dest_hbm.at[tile], idx_smem.at[s], idx_sem.at[s])

    def gather(s):
        _issue_rows(idx_smem, s, lambda k, r, d: pltpu.make_async_copy(
            yb_hbm.at[pl.ds(d, ROW_TILE), :], ybuf.at[s, k, pl.ds(r, ROW_TILE), :], row_sem.at[s]))

    @pl.when(i == 0)
    def _():
        idx_copy(0, 0).start()
        idx_copy(0, 0).wait()
        gather(0)
        idx_copy(1, 1).start()

    for s in range(2):
        @pl.when(jnp.logical_and(i + 1 < N_ROW_TILES, slot == 1 - s))
        def _(s=s):
            idx_copy(i + 1, s).wait()
            gather(s)

    @pl.when(i + 2 < N_ROW_TILES)
    def _():
        idx_copy(i + 2, slot).start()

    g0 = gate_ref[:, 0:1]
    g1 = gate_ref[:, 1:2]
    for s in range(2):
        @pl.when(slot == s)
        def _(s=s):
            for k in range(TOP_K):
                pltpu.make_async_copy(yb_hbm.at[pl.ds(0, TM_ROW * ROW_TILE), :], ybuf.at[s, k],
                                      row_sem.at[s]).wait()
            for c in range(ROW_TILE):
                y0 = ybuf[s, 0, pl.ds(c, TM_ROW, stride=ROW_TILE), :]
                y1 = ybuf[s, 1, pl.ds(c, TM_ROW, stride=ROW_TILE), :]
                lanes = slice(c * LANES, (c + 1) * LANES)
                o_ref[:, lanes] = x_ref[:, lanes] + (y0 * g0 + y1 * g1)


def _combine(x, gate_cols, dest_tiles, yb):
    return pl.pallas_call(
        _combine_kernel,
        grid=(N_ROW_TILES,),
        in_specs=[pl.BlockSpec((TM_ROW, D_MODEL), lambda i: (i, 0)),
                  pl.BlockSpec((TM_ROW, TOP_K), lambda i: (i, 0)),
                  pl.BlockSpec(memory_space=pl.ANY),
                  pl.BlockSpec(memory_space=pl.ANY)],
        out_specs=pl.BlockSpec((TM_ROW, D_MODEL), lambda i: (i, 0)),
        out_shape=jax.ShapeDtypeStruct((N_TOK, D_MODEL), f32),
        scratch_shapes=[pltpu.VMEM((2, TOP_K, TM_ROW * ROW_TILE, LANES), f32),
                        pltpu.SMEM((2, IDX_TRIPS, IDX_PER_TRIP), jnp.int32),
                        pltpu.SemaphoreType.DMA((2,)),
                        pltpu.SemaphoreType.DMA((2,))],
        compiler_params=_cparams(("arbitrary",)),
        name="moe_combine",
    )(x, gate_cols, dest_tiles, yb)


def _hier_moe(layer, x, nw, idx, gates, cnt, w_gate, w_up, w_down):
    counts = cnt[:, 0].astype(jnp.int32)
    padded = (counts + MOE_ROWS - 1) // MOE_ROWS * MOE_ROWS
    pends = jnp.cumsum(padded)
    pstarts = pends - padded
    eids = jnp.arange(N_EXPERTS, dtype=jnp.int32)[:, None, None]
    seg_start = jnp.sum(jnp.where(idx[None, 0:2] == eids, pstarts[:, None, None], 0), axis=0)
    dest = seg_start + idx[2:4]
    dest_tiles = _dest_tiles(dest)
    n_used = (pends[-1:] // MOE_ROWS).astype(jnp.int32)

    zero_plan = jnp.concatenate([jnp.where(padded > 0, pends - MOE_ROWS, -1), n_used]).astype(jnp.int32)
    xb = _dispatch(x, nw, dest_tiles, zero_plan)
    yb = _experts(layer, (pstarts // MOE_ROWS).astype(jnp.int32), (padded // MOE_ROWS).astype(jnp.int32),
                  n_used, xb, w_gate, w_up, w_down)
    return _combine(x, gates[0:2].T, dest_tiles, yb)


def kernel(x, positions, conv_norm_w, conv_w_in, conv_k, conv_w_out, mla_norm_w, mla_w_down, mla_q_lat_norm, mla_kv_lat_norm, mla_w_q_up, mla_w_kv_up, mla_q_head_norm, mla_k_head_norm, mla_w_out, ffn_norm_w, router_w_group, router_b_group, router_w_expert, router_b_expert, exp_w_gate, exp_w_up, exp_w_down):
    half = QK_ROPE_DIM // 2
    inv_freq = ROPE_THETA ** (-jnp.arange(0, QK_ROPE_DIM, 2, dtype=f32) / QK_ROPE_DIM)
    ang_t = inv_freq[None, :, None] * positions.astype(f32)[:, None, :]
    cos_t = jnp.cos(ang_t)
    sin_t = jnp.sin(ang_t)
    cos = cos_t.transpose(0, 2, 1)
    sin = sin_t.transpose(0, 2, 1)
    cs = jnp.concatenate([cos, cos, -sin, sin], axis=-1).reshape(N_TOK, LANES)
    row = lax.broadcasted_iota(jnp.int32, (TM, TM), 0)
    col = lax.broadcasted_iota(jnp.int32, (TM, TM), 1)
    tri = (row < col).astype(bf16)

    xt = x.reshape(N_TOK, D_MODEL)
    for i in range(DEPTH):
        j = i // N_MIXERS
        route_ops = _router_operands(ffn_norm_w[i][None, :], router_w_group[i], router_b_group[i],
                                     router_w_expert[i], router_b_expert[i], tri)
        if i % N_MIXERS == 0:
            z, gb = _conv_in(xt, conv_norm_w[j][None, :], conv_w_in[j].astype(bf16))
            xt, idx, gates, cnt = _conv_out(xt, z, gb, conv_k[j], conv_w_out[j].astype(bf16), route_ops)
        else:
            wd = mla_w_down[j]
            ro = Q_LORA_RANK + KV_LORA_RANK
            wd_aug = jnp.concatenate([wd, wd[:, ro + half:ro + 2 * half], wd[:, ro:ro + half]],
                                     axis=1).astype(bf16)
            wkv = mla_w_kv_up[j].reshape(KV_LORA_RANK, N_HEADS, QK_NOPE_DIM + V_HEAD_DIM)
            wk = wkv[:, :, :QK_NOPE_DIM].reshape(KV_LORA_RANK, N_HEADS * QK_NOPE_DIM).astype(bf16)
            wvt = wkv[:, :, QK_NOPE_DIM:].reshape(KV_LORA_RANK, N_HEADS * V_HEAD_DIM).T.astype(bf16)
            wqt = mla_w_q_up[j].T.astype(bf16)
            khn = mla_k_head_norm[j]
            kw1 = khn[QK_NOPE_DIM:QK_NOPE_DIM + half]
            kw2 = khn[QK_NOPE_DIM + half:]
            g128 = jnp.concatenate([kw1, kw2, kw2, kw1])[None, :]
            qt, k, vt = _mla_proj(xt, mla_norm_w[j][None, :], wd_aug, mla_q_lat_norm[j][None, :],
                                  mla_kv_lat_norm[j][None, :], wqt, wk, wvt,
                                  mla_q_head_norm[j][:, None], khn[None, :], g128, cs, cos_t, sin_t)
            ot = _attention(qt, k, vt)
            xt, idx, gates, cnt = _mla_out(xt, ot.reshape(BATCH, N_HEADS * V_HEAD_DIM, SEQ),
                                           mla_w_out[j].T.astype(bf16), route_ops)
        xt = _hier_moe(i, xt, ffn_norm_w[i][None, :], idx, gates, cnt, exp_w_gate, exp_w_up, exp_w_down)
    return xt.reshape(BATCH, SEQ, D_MODEL)
```

```python
import math

import jax
import jax.numpy as jnp
from jax import lax
from jax.experimental import pallas as pl
from jax.experimental.pallas import tpu as pltpu

D_MODEL = 1024
BATCH = 2
SEQ = 16384
DEPTH = 4
N_TOK = BATCH * SEQ
N_MIXERS = 2
N_HEADS = 8
Q_LORA_RANK = 384
KV_LORA_RANK = 256
QK_NOPE_DIM = 128
QK_ROPE_DIM = 64
QK_HEAD_DIM = QK_NOPE_DIM + QK_ROPE_DIM
V_HEAD_DIM = 128
V_ROWS = V_HEAD_DIM + 16
LATENT_DIM = Q_LORA_RANK + KV_LORA_RANK + QK_ROPE_DIM
LATENT_AUG = LATENT_DIM + QK_ROPE_DIM
ROPE_THETA = 10000.0
N_GROUPS = 8
EXPERTS_PER_GROUP = 8
N_EXPERTS = N_GROUPS * EXPERTS_PER_GROUP
TOP_K = 2
D_FF_EXPERT = 384
RMS_EPS = 1e-6

LANES = 128
VMEM_LIMIT = 56 * 1024 * 1024

TM = 1024
TM_ROW = 512
ROW_UNROLL = 8
MOE_ROWS = 256
N_MOE_BLOCKS = N_TOK * TOP_K // MOE_ROWS + N_EXPERTS
N_MOE_ROWS = N_MOE_BLOCKS * MOE_ROWS
TQ = 2048
TK = 512
QK_SCALE_LOG2E = (QK_HEAD_DIM ** -0.5) * math.log2(math.e)

bf16 = jnp.bfloat16
f32 = jnp.float32


def _cparams(sem):
    return pltpu.CompilerParams(dimension_semantics=sem, vmem_limit_bytes=VMEM_LIMIT)


def _rms(x, w):
    return x * lax.rsqrt(jnp.mean(x * x, axis=-1, keepdims=True) + RMS_EPS) * w


def _dot(a, b):
    return jnp.dot(a, b, preferred_element_type=f32)


def _dot_nt(a, b):
    return lax.dot_general(a, b, (((1,), (1,)), ((), ())), preferred_element_type=f32)


def _conv_in_kernel(x_ref, nw_ref, w_ref, z_ref, gb_ref):
    h = _rms(x_ref[...], nw_ref[...]).astype(bf16)
    gb = _dot(h, w_ref[:, 0:D_MODEL])
    gc = _dot(h, w_ref[:, D_MODEL:2 * D_MODEL])
    u = _dot(h, w_ref[:, 2 * D_MODEL:3 * D_MODEL])
    z_ref[...] = (gc * u).astype(bf16)
    gb_ref[...] = gb.astype(bf16)


def _conv_in(x, nw, w_in):
    return pl.pallas_call(
        _conv_in_kernel,
        grid=(N_TOK // TM,),
        in_specs=[pl.BlockSpec((TM, D_MODEL), lambda i: (i, 0)),
                  pl.BlockSpec((1, D_MODEL), lambda i: (0, 0)),
                  pl.BlockSpec((D_MODEL, 3 * D_MODEL), lambda i: (0, 0))],
        out_specs=[pl.BlockSpec((TM, D_MODEL), lambda i: (i, 0)),
                   pl.BlockSpec((TM, D_MODEL), lambda i: (i, 0))],
        out_shape=[jax.ShapeDtypeStruct((N_TOK, D_MODEL), bf16),
                   jax.ShapeDtypeStruct((N_TOK, D_MODEL), bf16)],
        compiler_params=_cparams(("arbitrary",)),
        name="conv_in",
    )(x, nw, w_in)


def _conv_out_kernel(x_ref, z_ref, zp_ref, zn_ref, gb_ref, k_ref, w_ref, *route_refs):
    o_ref = route_refs[len(ROUTE_IN_SPECS)]
    i = pl.program_id(0)
    tiles_per_seq = SEQ // TM
    first = (i % tiles_per_seq) == 0
    last = (i % tiles_per_seq) == tiles_per_seq - 1
    z = z_ref[...].astype(f32)
    prev_row = jnp.where(first, 0.0, zp_ref[7:8, :].astype(f32))
    next_row = jnp.where(last, 0.0, zn_ref[0:1, :].astype(f32))
    row = lax.broadcasted_iota(jnp.int32, z.shape, 0)
    z_m1 = jnp.where(row == 0, prev_row, pltpu.roll(z, 1, 0))
    z_p1 = jnp.where(row == TM - 1, next_row, pltpu.roll(z, TM - 1, 0))
    conv = z_m1 * k_ref[0:1, :] + z * k_ref[1:2, :] + z_p1 * k_ref[2:3, :]
    y = _dot((gb_ref[...].astype(f32) * conv).astype(bf16), w_ref[...])
    x_new = x_ref[...] + y
    o_ref[...] = x_new
    _route_tile(x_new, *route_refs[:len(ROUTE_IN_SPECS)], *route_refs[len(ROUTE_IN_SPECS) + 1:])


def _conv_out(x, z, gb, conv_k, w_out, route_ops):
    n_halo = N_TOK // 8
    per = TM // 8
    return pl.pallas_call(
        _conv_out_kernel,
        grid=(N_TOK // TM,),
        in_specs=[pl.BlockSpec((TM, D_MODEL), lambda i: (i, 0)),
                  pl.BlockSpec((TM, D_MODEL), lambda i: (i, 0)),
                  pl.BlockSpec((8, D_MODEL), lambda i: (jnp.maximum(i * per - 1, 0), 0)),
                  pl.BlockSpec((8, D_MODEL), lambda i: (jnp.minimum((i + 1) * per, n_halo - 1), 0)),
                  pl.BlockSpec((TM, D_MODEL), lambda i: (i, 0)),
                  pl.BlockSpec((3, D_MODEL), lambda i: (0, 0)),
                  pl.BlockSpec((D_MODEL, D_MODEL), lambda i: (0, 0))] + ROUTE_IN_SPECS,
        out_specs=[pl.BlockSpec((TM, D_MODEL), lambda i: (i, 0))] + ROUTE_OUT_SPECS,
        out_shape=[jax.ShapeDtypeStruct((N_TOK, D_MODEL), f32)] + ROUTE_OUT_SHAPES,
        scratch_shapes=ROUTE_SCRATCH,
        compiler_params=_cparams(("arbitrary",)),
        name="conv_out",
    )(x, z, z, z, gb, conv_k, w_out, *route_ops)


def _mla_proj_kernel(x_ref, nw_ref, wd_ref, qln_ref, kvln_ref, wqt_ref, wk_ref, wvt_ref,
                     qhn_ref, khn_ref, g128_ref, cs_ref, cost_ref, sint_ref,
                     qt_ref, k_ref, vt_ref):
    h = _rms(x_ref[...], nw_ref[...]).astype(bf16)
    lat = _dot(h, wd_ref[...])
    q_lat = _rms(lat[:, 0:Q_LORA_RANK], qln_ref[...]).astype(bf16)
    kv_lat = _rms(lat[:, Q_LORA_RANK:Q_LORA_RANK + KV_LORA_RANK], kvln_ref[...]).astype(bf16)
    kr_blk = lat[:, Q_LORA_RANK + KV_LORA_RANK:LATENT_AUG]
    lane = lax.broadcasted_iota(jnp.int32, kr_blk.shape, 1)
    kr_ss = jnp.sum(jnp.where(lane < QK_ROPE_DIM, kr_blk * kr_blk, 0.0), axis=-1, keepdims=True)
    t = kr_blk * g128_ref[...] * cs_ref[...]
    kr_roped = t + pltpu.roll(t, QK_ROPE_DIM, 1)

    k_nope = _dot(kv_lat, wk_ref[...])
    cos_t = cost_ref[0]
    sin_t = sint_ref[0]
    for hd in range(N_HEADS):
        kn = k_nope[:, hd * QK_NOPE_DIM:(hd + 1) * QK_NOPE_DIM]
        ss = jnp.sum(kn * kn, axis=-1, keepdims=True) + kr_ss
        r = lax.rsqrt(ss * (1.0 / QK_HEAD_DIM) + RMS_EPS)
        k_ref[0, hd, :, 0:QK_NOPE_DIM] = (kn * r * khn_ref[:, 0:QK_NOPE_DIM]).astype(bf16)
        k_ref[0, hd, :, QK_NOPE_DIM:QK_HEAD_DIM] = (kr_roped[:, 0:QK_ROPE_DIM] * r).astype(bf16)

        qt = _dot_nt(wqt_ref[hd * QK_HEAD_DIM:(hd + 1) * QK_HEAD_DIM, :], q_lat)
        rq = lax.rsqrt(jnp.mean(qt * qt, axis=0, keepdims=True) + RMS_EPS)
        qn = qt * rq * qhn_ref[...] * QK_SCALE_LOG2E
        half = QK_ROPE_DIM // 2
        x1 = qn[QK_NOPE_DIM:QK_NOPE_DIM + half, :]
        x2 = qn[QK_NOPE_DIM + half:QK_HEAD_DIM, :]
        qt_ref[0, hd, 0:QK_NOPE_DIM, :] = qn[0:QK_NOPE_DIM, :].astype(bf16)
        qt_ref[0, hd, QK_NOPE_DIM:QK_NOPE_DIM + half, :] = (x1 * cos_t - x2 * sin_t).astype(bf16)
        qt_ref[0, hd, QK_NOPE_DIM + half:QK_HEAD_DIM, :] = (x2 * cos_t + x1 * sin_t).astype(bf16)

        vt = _dot_nt(wvt_ref[hd * V_HEAD_DIM:(hd + 1) * V_HEAD_DIM, :], kv_lat)
        vt_ref[0, hd, 0:V_HEAD_DIM, :] = vt.astype(bf16)
        vt_ref[0, hd, V_HEAD_DIM:V_ROWS, :] = jnp.ones((V_ROWS - V_HEAD_DIM, TM), bf16)


def _mla_proj(x, nw, wd_aug, qln, kvln, wqt, wk, wvt, qhn_col, khn_row, g128, cs, cos_t, sin_t):
    tps = SEQ // TM
    const = lambda i: (0, 0)
    return pl.pallas_call(
        _mla_proj_kernel,
        grid=(N_TOK // TM,),
        in_specs=[pl.BlockSpec((TM, D_MODEL), lambda i: (i, 0)),
                  pl.BlockSpec((1, D_MODEL), const),
                  pl.BlockSpec((D_MODEL, LATENT_AUG), const),
                  pl.BlockSpec((1, Q_LORA_RANK), const),
                  pl.BlockSpec((1, KV_LORA_RANK), const),
                  pl.BlockSpec((N_HEADS * QK_HEAD_DIM, Q_LORA_RANK), const),
                  pl.BlockSpec((KV_LORA_RANK, N_HEADS * QK_NOPE_DIM), const),
                  pl.BlockSpec((N_HEADS * V_HEAD_DIM, KV_LORA_RANK), const),
                  pl.BlockSpec((QK_HEAD_DIM, 1), const),
                  pl.BlockSpec((1, QK_HEAD_DIM), const),
                  pl.BlockSpec((1, LANES), const),
                  pl.BlockSpec((TM, LANES), lambda i: (i, 0)),
                  pl.BlockSpec((1, QK_ROPE_DIM // 2, TM), lambda i: (i // tps, 0, i % tps)),
                  pl.BlockSpec((1, QK_ROPE_DIM // 2, TM), lambda i: (i // tps, 0, i % tps))],
        out_specs=[pl.BlockSpec((1, N_HEADS, QK_HEAD_DIM, TM), lambda i: (i // tps, 0, 0, i % tps)),
                   pl.BlockSpec((1, N_HEADS, TM, QK_HEAD_DIM), lambda i: (i // tps, 0, i % tps, 0)),
                   pl.BlockSpec((1, N_HEADS, V_ROWS, TM), lambda i: (i // tps, 0, 0, i % tps))],
        out_shape=[jax.ShapeDtypeStruct((BATCH, N_HEADS, QK_HEAD_DIM, SEQ), bf16),
                   jax.ShapeDtypeStruct((BATCH, N_HEADS, SEQ, QK_HEAD_DIM), bf16),
                   jax.ShapeDtypeStruct((BATCH, N_HEADS, V_ROWS, SEQ), bf16)],
        compiler_params=_cparams(("arbitrary",)),
        name="mla_proj",
    )(x, nw, wd_aug, qln, kvln, wqt, wk, wvt, qhn_col, khn_row, g128, cs, cos_t, sin_t)


def _attn_kernel(qt_ref, k_ref, vt_ref, o_ref, acc_ref, s0_ref, s1_ref, p0_ref, p1_ref):
    qt = qt_ref[0, 0]
    n = SEQ // TK
    acc_ref[...] = jnp.zeros_like(acc_ref)

    def scores(j, s_ref):
        off = pl.multiple_of(j * TK, TK)
        s = _dot(k_ref[0, 0, pl.ds(off, TK), :], qt)
        s_ref[...] = s
        return jnp.max(s, axis=0, keepdims=True)

    def softmax(s_ref, p_ref, m, cmax):
        m_new = jnp.maximum(m, cmax)
        alpha = jnp.exp2(m - m_new)
        p_ref[...] = jnp.exp2((s_ref[...] - m_new).astype(bf16))
        return m_new, alpha

    def values(j, p_ref, alpha):
        off = pl.multiple_of(j * TK, TK)
        pv = _dot(vt_ref[0, 0, :, pl.ds(off, TK)], p_ref[...])
        acc_ref[...] = alpha * acc_ref[...] + pv

    m = jnp.full((1, TQ), -jnp.inf, f32)
    c0 = scores(0, s0_ref)
    c1 = scores(1, s1_ref)
    m, a0 = softmax(s0_ref, p0_ref, m, c0)
    c0 = scores(2, s0_ref)
    m, a1 = softmax(s1_ref, p1_ref, m, c1)
    values(0, p0_ref, a0)

    def pair(jj, carry):
        m, a_prev, c0 = carry
        j = 2 * jj
        c1 = scores(j + 1, s1_ref)
        m, a0 = softmax(s0_ref, p0_ref, m, c0)
        values(j - 1, p1_ref, a_prev)
        c0 = scores(j + 2, s0_ref)
        m, a1 = softmax(s1_ref, p1_ref, m, c1)
        values(j, p0_ref, a0)
        return m, a1, c0

    m, a_prev, c0 = lax.fori_loop(1, n // 2 - 1, pair, (m, a1, c0))
    c1 = scores(n - 1, s1_ref)
    m, a0 = softmax(s0_ref, p0_ref, m, c0)
    values(n - 3, p1_ref, a_prev)
    m, a1 = softmax(s1_ref, p1_ref, m, c1)
    values(n - 2, p0_ref, a0)
    values(n - 1, p1_ref, a1)
    o_ref[0, 0] = (acc_ref[0:V_HEAD_DIM, :] / acc_ref[V_HEAD_DIM:V_HEAD_DIM + 1, :]).astype(bf16)


def _attention(qt, k, vt):
    return pl.pallas_call(
        _attn_kernel,
        grid=(BATCH, N_HEADS, SEQ // TQ),
        in_specs=[pl.BlockSpec((1, 1, QK_HEAD_DIM, TQ), lambda b, h, i: (b, h, 0, i)),
                  pl.BlockSpec((1, 1, SEQ, QK_HEAD_DIM), lambda b, h, i: (b, h, 0, 0)),
                  pl.BlockSpec((1, 1, V_ROWS, SEQ), lambda b, h, i: (b, h, 0, 0))],
        out_specs=pl.BlockSpec((1, 1, V_HEAD_DIM, TQ), lambda b, h, i: (b, h, 0, i)),
        out_shape=jax.ShapeDtypeStruct((BATCH, N_HEADS, V_HEAD_DIM, SEQ), bf16),
        scratch_shapes=[pltpu.VMEM((V_ROWS, TQ), f32),
                        pltpu.VMEM((TK, TQ), f32), pltpu.VMEM((TK, TQ), f32),
                        pltpu.VMEM((TK, TQ), bf16), pltpu.VMEM((TK, TQ), bf16)],
        compiler_params=_cparams(("arbitrary", "arbitrary", "arbitrary")),
        name="mla_attention",
    )(qt, k, vt)


def _mla_out_kernel(x_ref, ot_ref, w_ref, *route_refs):
    o_ref = route_refs[len(ROUTE_IN_SPECS)]
    yt = _dot(w_ref[...], ot_ref[0])
    x_new = x_ref[...] + yt.T
    o_ref[...] = x_new
    _route_tile(x_new, *route_refs[:len(ROUTE_IN_SPECS)], *route_refs[len(ROUTE_IN_SPECS) + 1:])


def _mla_out(x, ot, w_out_t, route_ops):
    tps = SEQ // TM
    return pl.pallas_call(
        _mla_out_kernel,
        grid=(N_TOK // TM,),
        in_specs=[pl.BlockSpec((TM, D_MODEL), lambda i: (i, 0)),
                  pl.BlockSpec((1, N_HEADS * V_HEAD_DIM, TM), lambda i: (i // tps, 0, i % tps)),
                  pl.BlockSpec((D_MODEL, N_HEADS * V_HEAD_DIM), lambda i: (0, 0))] + ROUTE_IN_SPECS,
        out_specs=[pl.BlockSpec((TM, D_MODEL), lambda i: (i, 0))] + ROUTE_OUT_SPECS,
        out_shape=[jax.ShapeDtypeStruct((N_TOK, D_MODEL), f32)] + ROUTE_OUT_SHAPES,
        scratch_shapes=ROUTE_SCRATCH,
        compiler_params=_cparams(("arbitrary",)),
        name="mla_out",
    )(x, ot, w_out_t, *route_ops)


def _first_argmax(v, n):
    idx = lax.broadcasted_iota(jnp.int32, v.shape, 0)
    mx = jnp.max(v, axis=0, keepdims=True)
    return jnp.min(jnp.where(v == mx, idx, n), axis=0, keepdims=True)


def _route_tile(x, nw_ref, wt_hi_ref, wt_lo_ref, bias_ref, tri_ref,
                idx_ref, gate_ref, cnt_ref, base_ref):
    i = pl.program_id(0)

    @pl.when(i == 0)
    def _():
        base_ref[...] = jnp.zeros_like(base_ref)

    h = _rms(x, nw_ref[...])
    h_hi = h.astype(bf16)
    h_lo = (h - h_hi.astype(f32)).astype(bf16)
    logits = (_dot_nt(wt_hi_ref[...], h_hi) + _dot_nt(wt_lo_ref[...], h_hi)
              + _dot_nt(wt_hi_ref[...], h_lo))
    g_logits = logits[0:N_GROUPS, :]
    g_exp = jnp.exp(g_logits - jnp.max(g_logits, axis=0, keepdims=True))
    g_prob = g_exp / jnp.sum(g_exp, axis=0, keepdims=True)
    g_sel = _first_argmax(g_logits + bias_ref[0:N_GROUPS, :], N_GROUPS)
    gidx = lax.broadcasted_iota(jnp.int32, g_logits.shape, 0)
    g_gate = jnp.sum(jnp.where(gidx == g_sel, g_prob, 0.0), axis=0, keepdims=True)

    e_logits = jnp.zeros((EXPERTS_PER_GROUP, TM), f32)
    e_bias = jnp.zeros((EXPERTS_PER_GROUP, TM), f32)
    for g in range(N_GROUPS):
        lo = N_GROUPS + g * EXPERTS_PER_GROUP
        e_logits = jnp.where(g_sel == g, logits[lo:lo + EXPERTS_PER_GROUP, :], e_logits)
        e_bias = jnp.where(g_sel == g, bias_ref[lo:lo + EXPERTS_PER_GROUP, :], e_bias)
    scored = e_logits + e_bias
    eidx = lax.broadcasted_iota(jnp.int32, scored.shape, 0)
    top0 = _first_argmax(scored, EXPERTS_PER_GROUP)
    top1 = _first_argmax(jnp.where(eidx == top0, -jnp.inf, scored), EXPERTS_PER_GROUP)
    e_exp = jnp.exp(e_logits - jnp.max(e_logits, axis=0, keepdims=True))
    e_prob = e_exp / jnp.sum(e_exp, axis=0, keepdims=True)
    p0 = jnp.sum(jnp.where(eidx == top0, e_prob, 0.0), axis=0, keepdims=True)
    p1 = jnp.sum(jnp.where(eidx == top1, e_prob, 0.0), axis=0, keepdims=True)
    psum = p0 + p1
    gate0 = g_gate * (p0 / psum)
    gate1 = g_gate * (p1 / psum)
    eid0 = g_sel * EXPERTS_PER_GROUP + top0
    eid1 = g_sel * EXPERTS_PER_GROUP + top1

    xidx = lax.broadcasted_iota(jnp.int32, (N_EXPERTS, TM), 0)
    oh0 = xidx == eid0
    oh1 = xidx == eid1
    pre0 = _dot(jnp.where(oh0, 1.0, 0.0).astype(bf16), tri_ref[...])
    pre1 = _dot(jnp.where(oh1, 1.0, 0.0).astype(bf16), tri_ref[...])
    tot0 = jnp.sum(jnp.where(oh0, 1.0, 0.0), axis=1, keepdims=True)
    tot1 = jnp.sum(jnp.where(oh1, 1.0, 0.0), axis=1, keepdims=True)
    base = base_ref[:, 0:1]
    rank0 = jnp.sum(jnp.where(oh0, base + pre0, 0.0), axis=0, keepdims=True)
    rank1 = jnp.sum(jnp.where(oh1, base + tot0 + pre1, 0.0), axis=0, keepdims=True)
    new_base = base + tot0 + tot1
    base_ref[...] = jnp.broadcast_to(new_base, base_ref.shape)
    cnt_ref[...] = jnp.broadcast_to(new_base, cnt_ref.shape)

    zi = jnp.zeros((4, TM), jnp.int32)
    idx_ref[...] = jnp.concatenate(
        [eid0, eid1, rank0.astype(jnp.int32), rank1.astype(jnp.int32), zi], axis=0)
    gate_ref[...] = jnp.concatenate([gate0, gate1, jnp.zeros((6, TM), f32)], axis=0)


def _const2(i):
    return (0, 0)


ROUTE_IN_SPECS = [pl.BlockSpec((1, D_MODEL), _const2),
                  pl.BlockSpec((LANES, D_MODEL), _const2),
                  pl.BlockSpec((LANES, D_MODEL), _const2),
                  pl.BlockSpec((LANES, 1), _const2),
                  pl.BlockSpec((TM, TM), _const2)]
ROUTE_OUT_SPECS = [pl.BlockSpec((8, TM), lambda i: (0, i)),
                   pl.BlockSpec((8, TM), lambda i: (0, i)),
                   pl.BlockSpec((N_EXPERTS, LANES), _const2)]
ROUTE_OUT_SHAPES = [jax.ShapeDtypeStruct((8, N_TOK), jnp.int32),
                    jax.ShapeDtypeStruct((8, N_TOK), f32),
                    jax.ShapeDtypeStruct((N_EXPERTS, LANES), f32)]
ROUTE_SCRATCH = [pltpu.VMEM((N_EXPERTS, LANES), f32)]


def _router_operands(nw, w_group, b_group, w_expert, b_expert, tri):
    wt = jnp.concatenate([w_group, w_expert], axis=1).T
    wt = jnp.pad(wt, ((0, LANES - wt.shape[0]), (0, 0)))
    wt_hi = wt.astype(bf16)
    wt_lo = (wt - wt_hi.astype(f32)).astype(bf16)
    bias_col = jnp.pad(jnp.concatenate([b_group, b_expert]), (0, LANES - N_GROUPS - N_EXPERTS))[:, None]
    return nw, wt_hi, wt_lo, bias_col, tri


N_ROW_TILES = N_TOK // TM_ROW
ROW_TILE = 8
IDX_GROUPS = TM_ROW // LANES
IDX_TRIPS = LANES // ROW_UNROLL
IDX_PER_TRIP = IDX_GROUPS * ROW_UNROLL * TOP_K


def _issue_rows(idx_smem, s, make_copy):
    for r8 in range(IDX_TRIPS):
        for g in range(IDX_GROUPS):
            for u in range(ROW_UNROLL):
                for k in range(TOP_K):
                    d = pl.multiple_of(idx_smem[s, r8, (g * ROW_UNROLL + u) * TOP_K + k] * ROW_TILE,
                                       ROW_TILE)
                    r = (g * LANES + r8 * ROW_UNROLL + u) * ROW_TILE
                    make_copy(k, r, d).start(priority=k)


def _dest_tiles(dest):
    d = dest.reshape(TOP_K, N_ROW_TILES, IDX_GROUPS, IDX_TRIPS, ROW_UNROLL)
    return d.transpose(1, 3, 2, 4, 0).reshape(N_ROW_TILES, IDX_TRIPS, IDX_PER_TRIP)


def _dispatch_kernel(last_ref, x_ref, nw_ref, dest_hbm, xb_hbm, hbuf, zbuf, idx_smem, idx_sem,
                     row_sem, zero_sem):
    i = pl.program_id(0)
    slot = i % 2

    def idx_copy(tile, s):
        return pltpu.make_async_copy(dest_hbm.at[tile], idx_smem.at[s], idx_sem.at[s])

    @pl.when(i == 0)
    def _():
        zbuf[...] = jnp.zeros_like(zbuf)

        def zero_copy(row):
            row0 = pl.multiple_of(row * ROW_TILE, ROW_TILE)
            return pltpu.make_async_copy(zbuf, xb_hbm.at[pl.ds(row0, MOE_ROWS * ROW_TILE), :], zero_sem)

        def start(e, c):
            @pl.when(last_ref[e] >= 0)
            def _():
                zero_copy(last_ref[e]).start()
            return c

        def wait(e, c):
            @pl.when(last_ref[e] >= 0)
            def _():
                zero_copy(last_ref[e]).wait()
            return c

        def start_tail(b, c):
            zero_copy(b * MOE_ROWS).start()
            return c

        def wait_tail(b, c):
            zero_copy(b * MOE_ROWS).wait()
            return c

        n_used = last_ref[N_EXPERTS]
        lax.fori_loop(0, N_EXPERTS, start, 0)
        lax.fori_loop(n_used, N_MOE_BLOCKS, start_tail, 0)
        lax.fori_loop(0, N_EXPERTS, wait, 0)
        lax.fori_loop(n_used, N_MOE_BLOCKS, wait_tail, 0)

    def drain(s):
        for _ in range(TOP_K):
            pltpu.make_async_copy(hbuf.at[s], xb_hbm.at[pl.ds(0, TM_ROW * ROW_TILE), :],
                                  row_sem.at[s]).wait()

    @pl.when(i == 0)
    def _():
        idx_copy(0, 0).start()

    idx_copy(i, slot).wait()

    @pl.when(i + 1 < N_ROW_TILES)
    def _():
        idx_copy(i + 1, 1 - slot).start()

    h = _rms(x_ref[...], nw_ref[...])
    for s in range(2):
        @pl.when(slot == s)
        def _(s=s):
            @pl.when(i >= 2)
            def _():
                drain(s)
            for c in range(ROW_TILE):
                hbuf[s, pl.ds(c, TM_ROW, stride=ROW_TILE), :] = h[:, c * LANES:(c + 1) * LANES]
            _issue_rows(idx_smem, s, lambda k, r, d: pltpu.make_async_copy(
                hbuf.at[s, pl.ds(r, ROW_TILE), :], xb_hbm.at[pl.ds(d, ROW_TILE), :], row_sem.at[s]))

    @pl.when(i == N_ROW_TILES - 1)
    def _():
        drain(0)
        drain(1)


def _dispatch(x, nw, dest_tiles, zero_plan):
    grid_spec = pltpu.PrefetchScalarGridSpec(
        num_scalar_prefetch=1,
        grid=(N_ROW_TILES,),
        in_specs=[pl.BlockSpec((TM_ROW, D_MODEL), lambda i, lb: (i, 0)),
                  pl.BlockSpec((1, D_MODEL), lambda i, lb: (0, 0)),
                  pl.BlockSpec(memory_space=pl.ANY)],
        out_specs=pl.BlockSpec(memory_space=pl.ANY),
        scratch_shapes=[pltpu.VMEM((2, TM_ROW * ROW_TILE, LANES), f32),
                        pltpu.VMEM((MOE_ROWS * ROW_TILE, LANES), f32),
                        pltpu.SMEM((2, IDX_TRIPS, IDX_PER_TRIP), jnp.int32),
                        pltpu.SemaphoreType.DMA((2,)),
                        pltpu.SemaphoreType.DMA((2,)),
                        pltpu.SemaphoreType.DMA])
    return pl.pallas_call(
        _dispatch_kernel,
        grid_spec=grid_spec,
        out_shape=jax.ShapeDtypeStruct((N_MOE_ROWS * ROW_TILE, LANES), f32),
        compiler_params=_cparams(("arbitrary",)),
        name="moe_dispatch",
    )(zero_plan, x, nw, dest_tiles)


MOE_BLOCK_SUBLANES = MOE_ROWS * ROW_TILE


def _expert_kernel(first_ref, nblk_ref, nused_ref, xb_hbm, wg_ref, wu_ref, wd_ref, yb_hbm,
                   xbuf, ybuf, wg_b, wu_b, wd_b, in_sem, out_sem, zero_sem):
    e = pl.program_id(0)
    n_used = nused_ref[0]

    def block_rows(g):
        return pl.ds(pl.multiple_of(g * MOE_BLOCK_SUBLANES, MOE_BLOCK_SUBLANES), MOE_BLOCK_SUBLANES)

    def x_copy(g, s):
        return pltpu.make_async_copy(xb_hbm.at[block_rows(g), :], xbuf.at[s], in_sem.at[s])

    def y_copy(g, s):
        return pltpu.make_async_copy(ybuf.at[s], yb_hbm.at[block_rows(g), :], out_sem.at[s])

    @pl.when(e == 0)
    def _():
        x_copy(0, 0).start(priority=1)

    @pl.when(nblk_ref[e] > 0)
    def _():
        wg_b[...] = wg_ref[...].astype(bf16)
        wu_b[...] = wu_ref[...].astype(bf16)
        wd_b[...] = wd_ref[...].astype(bf16)

    def block(i, c):
        g = first_ref[e] + i
        s = g % 2
        x_copy(g, s).wait()

        @pl.when(g + 1 < n_used)
        def _():
            x_copy(g + 1, 1 - s).start(priority=1)

        @pl.when(g >= 2)
        def _():
            y_copy(g - 2, s).wait()

        xb = jnp.concatenate([xbuf[s, pl.ds(k, MOE_ROWS, stride=ROW_TILE), :] for k in range(ROW_TILE)],
                             axis=1).astype(bf16)
        hid = jax.nn.silu(_dot(xb, wg_b[...])) * _dot(xb, wu_b[...])
        y = _dot(hid.astype(bf16), wd_b[...])
        for k in range(ROW_TILE):
            ybuf[s, pl.ds(k, MOE_ROWS, stride=ROW_TILE), :] = y[:, k * LANES:(k + 1) * LANES]
        y_copy(g, s).start(priority=1)
        return c

    lax.fori_loop(0, nblk_ref[e], block, 0)

    @pl.when(e == N_EXPERTS - 1)
    def _():
        @pl.when(n_used >= 2)
        def _():
            y_copy(n_used - 2, n_used % 2).wait()
        y_copy(n_used - 1, (n_used + 1) % 2).wait()
        ybuf[0] = jnp.zeros((MOE_BLOCK_SUBLANES, LANES), f32)

        def zero_copy(g):
            return pltpu.make_async_copy(ybuf.at[0], yb_hbm.at[block_rows(g), :], zero_sem)

        def start_tail(g, c):
            zero_copy(g).start()
            return c

        def wait_tail(g, c):
            zero_copy(g).wait()
            return c

        lax.fori_loop(n_used, N_MOE_BLOCKS, start_tail, 0)
        lax.fori_loop(n_used, N_MOE_BLOCKS, wait_tail, 0)


def _experts(layer, first_block, n_blocks, n_used, xb, w_gate, w_up, w_down):
    grid_spec = pltpu.PrefetchScalarGridSpec(
        num_scalar_prefetch=3,
        grid=(N_EXPERTS,),
        in_specs=[pl.BlockSpec(memory_space=pl.ANY),
                  pl.BlockSpec((None, None, D_MODEL, D_FF_EXPERT), lambda e, fb, nb, nu: (layer, e, 0, 0)),
                  pl.BlockSpec((None, None, D_MODEL, D_FF_EXPERT), lambda e, fb, nb, nu: (layer, e, 0, 0)),
                  pl.BlockSpec((None, None, D_FF_EXPERT, D_MODEL), lambda e, fb, nb, nu: (layer, e, 0, 0))],
        out_specs=pl.BlockSpec(memory_space=pl.ANY),
        scratch_shapes=[pltpu.VMEM((2, MOE_BLOCK_SUBLANES, LANES), f32),
                        pltpu.VMEM((2, MOE_BLOCK_SUBLANES, LANES), f32),
                        pltpu.VMEM((D_MODEL, D_FF_EXPERT), bf16),
                        pltpu.VMEM((D_MODEL, D_FF_EXPERT), bf16),
                        pltpu.VMEM((D_FF_EXPERT, D_MODEL), bf16),
                        pltpu.SemaphoreType.DMA((2,)),
                        pltpu.SemaphoreType.DMA((2,)),
                        pltpu.SemaphoreType.DMA])
    return pl.pallas_call(
        _expert_kernel,
        grid_spec=grid_spec,
        out_shape=jax.ShapeDtypeStruct((N_MOE_ROWS * ROW_TILE, LANES), f32),
        compiler_params=_cparams(("arbitrary",)),
        name="moe_experts",
    )(first_block, n_blocks, n_used, xb, w_gate, w_up, w_down)


def _combine_kernel(x_ref, gate_ref, dest_hbm, yb_hbm, o_ref, ybuf, idx_smem, idx_sem, row_sem):
    i = pl.program_id(0)
    slot = i % 2

    def idx_copy(tile, s):
        return pltpu.make_async_copy(dest_hbm.at[tile], idx_smem.at[s], idx_sem.at[s])

    def gather(s):
        _issue_rows(idx_smem, s, lambda k, r, d: pltpu.make_async_copy(
            yb_hbm.at[pl.ds(d, ROW_TILE), :], ybuf.at[s, k, pl.ds(r, ROW_TILE), :], row_sem.at[s]))

    @pl.when(i == 0)
    def _():
        idx_copy(0, 0).start()
        idx_copy(0, 0).wait()
        gather(0)
        idx_copy(1, 1).start()

    for s in range(2):
        @pl.when(jnp.logical_and(i + 1 < N_ROW_TILES, slot == 1 - s))
        def _(s=s):
            idx_copy(i + 1, s).wait()
            gather(s)

    @pl.when(i + 2 < N_ROW_TILES)
    def _():
        idx_copy(i + 2, slot).start()

    g0 = gate_ref[:, 0:1]
    g1 = gate_ref[:, 1:2]
    for s in range(2):
        @pl.when(slot == s)
        def _(s=s):
            for k in range(TOP_K):
                pltpu.make_async_copy(yb_hbm.at[pl.ds(0, TM_ROW * ROW_TILE), :], ybuf.at[s, k],
                                      row_sem.at[s]).wait()
            for c in range(ROW_TILE):
                y0 = ybuf[s, 0, pl.ds(c, TM_ROW, stride=ROW_TILE), :]
                y1 = ybuf[s, 1, pl.ds(c, TM_ROW, stride=ROW_TILE), :]
                lanes = slice(c * LANES, (c + 1) * LANES)
                o_ref[:, lanes] = x_ref[:, lanes] + (y0 * g0 + y1 * g1)


def _combine(x, gate_cols, dest_tiles, yb):
    return pl.pallas_call(
        _combine_kernel,
        grid=(N_ROW_TILES,),
        in_specs=[pl.BlockSpec((TM_ROW, D_MODEL), lambda i: (i, 0)),
                  pl.BlockSpec((TM_ROW, TOP_K), lambda i: (i, 0)),
                  pl.BlockSpec(memory_space=pl.ANY),
                  pl.BlockSpec(memory_space=pl.ANY)],
        out_specs=pl.BlockSpec((TM_ROW, D_MODEL), lambda i: (i, 0)),
        out_shape=jax.ShapeDtypeStruct((N_TOK, D_MODEL), f32),
        scratch_shapes=[pltpu.VMEM((2, TOP_K, TM_ROW * ROW_TILE, LANES), f32),
                        pltpu.SMEM((2, IDX_TRIPS, IDX_PER_TRIP), jnp.int32),
                        pltpu.SemaphoreType.DMA((2,)),
                        pltpu.SemaphoreType.DMA((2,))],
        compiler_params=_cparams(("arbitrary",)),
        name="moe_combine",
    )(x, gate_cols, dest_tiles, yb)


def _hier_moe(layer, x, nw, idx, gates, cnt, w_gate, w_up, w_down):
    counts = cnt[:, 0].astype(jnp.int32)
    padded = (counts + MOE_ROWS - 1) // MOE_ROWS * MOE_ROWS
    pends = jnp.cumsum(padded)
    pstarts = pends - padded
    eids = jnp.arange(N_EXPERTS, dtype=jnp.int32)[:, None, None]
    seg_start = jnp.sum(jnp.where(idx[None, 0:2] == eids, pstarts[:, None, None], 0), axis=0)
    dest = seg_start + idx[2:4]
    dest_tiles = _dest_tiles(dest)
    n_used = (pends[-1:] // MOE_ROWS).astype(jnp.int32)

    zero_plan = jnp.concatenate([jnp.where(padded > 0, pends - MOE_ROWS, -1), n_used]).astype(jnp.int32)
    xb = _dispatch(x, nw, dest_tiles, zero_plan)
    yb = _experts(layer, (pstarts // MOE_ROWS).astype(jnp.int32), (padded // MOE_ROWS).astype(jnp.int32),
                  n_used, xb, w_gate, w_up, w_down)
    return _combine(x, gates[0:2].T, dest_tiles, yb)


def kernel(x, positions, conv_norm_w, conv_w_in, conv_k, conv_w_out, mla_norm_w, mla_w_down, mla_q_lat_norm, mla_kv_lat_norm, mla_w_q_up, mla_w_kv_up, mla_q_head_norm, mla_k_head_norm, mla_w_out, ffn_norm_w, router_w_group, router_b_group, router_w_expert, router_b_expert, exp_w_gate, exp_w_up, exp_w_down):
    half = QK_ROPE_DIM // 2
    inv_freq = ROPE_THETA ** (-jnp.arange(0, QK_ROPE_DIM, 2, dtype=f32) / QK_ROPE_DIM)
    ang_t = inv_freq[None, :, None] * positions.astype(f32)[:, None, :]
    cos_t = jnp.cos(ang_t)
    sin_t = jnp.sin(ang_t)
    cos = cos_t.transpose(0, 2, 1)
    sin = sin_t.transpose(0, 2, 1)
    cs = jnp.concatenate([cos, cos, -sin, sin], axis=-1).reshape(N_TOK, LANES)
    row = lax.broadcasted_iota(jnp.int32, (TM, TM), 0)
    col = lax.broadcasted_iota(jnp.int32, (TM, TM), 1)
    tri = (row < col).astype(bf16)

    xt = x.reshape(N_TOK, D_MODEL)
    for i in range(DEPTH):
        j = i // N_MIXERS
        route_ops = _router_operands(ffn_norm_w[i][None, :], router_w_group[i], router_b_group[i],
                                     router_w_expert[i], router_b_expert[i], tri)
        if i % N_MIXERS == 0:
            z, gb = _conv_in(xt, conv_norm_w[j][None, :], conv_w_in[j].astype(bf16))
            xt, idx, gates, cnt = _conv_out(xt, z, gb, conv_k[j], conv_w_out[j].astype(bf16), route_ops)
        else:
            wd = mla_w_down[j]
            ro = Q_LORA_RANK + KV_LORA_RANK
            wd_aug = jnp.concatenate([wd, wd[:, ro + half:ro + 2 * half], wd[:, ro:ro + half]],
                                     axis=1).astype(bf16)
            wkv = mla_w_kv_up[j].reshape(KV_LORA_RANK, N_HEADS, QK_NOPE_DIM + V_HEAD_DIM)
            wk = wkv[:, :, :QK_NOPE_DIM].reshape(KV_LORA_RANK, N_HEADS * QK_NOPE_DIM).astype(bf16)
            wvt = wkv[:, :, QK_NOPE_DIM:].reshape(KV_LORA_RANK, N_HEADS * V_HEAD_DIM).T.astype(bf16)
            wqt = mla_w_q_up[j].T.astype(bf16)
            khn = mla_k_head_norm[j]
            kw1 = khn[QK_NOPE_DIM:QK_NOPE_DIM + half]
            kw2 = khn[QK_NOPE_DIM + half:]
            g128 = jnp.concatenate([kw1, kw2, kw2, kw1])[None, :]
            qt, k, vt = _mla_proj(xt, mla_norm_w[j][None, :], wd_aug, mla_q_lat_norm[j][None, :],
                                  mla_kv_lat_norm[j][None, :], wqt, wk, wvt,
                                  mla_q_head_norm[j][:, None], khn[None, :], g128, cs, cos_t, sin_t)
            ot = _attention(qt, k, vt)
            xt, idx, gates, cnt = _mla_out(xt, ot.reshape(BATCH, N_HEADS * V_HEAD_DIM, SEQ),
                                           mla_w_out[j].T.astype(bf16), route_ops)
        xt = _hier_moe(i, xt, ffn_norm_w[i][None, :], idx, gates, cnt, exp_w_gate, exp_w_up, exp_w_down)
    return xt.reshape(BATCH, SEQ, D_MODEL)
```

```python
import math

import jax
import jax.numpy as jnp
from jax import lax
from jax.experimental import pallas as pl
from jax.experimental.pallas import tpu as pltpu

D_MODEL = 1024
BATCH = 2
SEQ = 16384
DEPTH = 4
N_TOK = BATCH * SEQ
N_MIXERS = 2
N_HEADS = 8
Q_LORA_RANK = 384
KV_LORA_RANK = 256
QK_NOPE_DIM = 128
QK_ROPE_DIM = 64
QK_HEAD_DIM = QK_NOPE_DIM + QK_ROPE_DIM
V_HEAD_DIM = 128
V_ROWS = V_HEAD_DIM + 16
LATENT_DIM = Q_LORA_RANK + KV_LORA_RANK + QK_ROPE_DIM
LATENT_AUG = LATENT_DIM + QK_ROPE_DIM
ROPE_THETA = 10000.0
N_GROUPS = 8
EXPERTS_PER_GROUP = 8
N_EXPERTS = N_GROUPS * EXPERTS_PER_GROUP
TOP_K = 2
D_FF_EXPERT = 384
RMS_EPS = 1e-6

LANES = 128
VMEM_LIMIT = 56 * 1024 * 1024

TM = 1024
TM_ROW = 512
ROW_UNROLL = 8
MOE_ROWS = 256
N_MOE_BLOCKS = N_TOK * TOP_K // MOE_ROWS + N_EXPERTS
N_MOE_ROWS = N_MOE_BLOCKS * MOE_ROWS
TQ = 2048
TK = 512
QK_SCALE_LOG2E = (QK_HEAD_DIM ** -0.5) * math.log2(math.e)

bf16 = jnp.bfloat16
f32 = jnp.float32


def _cparams(sem):
    return pltpu.CompilerParams(dimension_semantics=sem, vmem_limit_bytes=VMEM_LIMIT)


def _rms(x, w):
    return x * lax.rsqrt(jnp.mean(x * x, axis=-1, keepdims=True) + RMS_EPS) * w


def _dot(a, b):
    return jnp.dot(a, b, preferred_element_type=f32)


def _dot_nt(a, b):
    return lax.dot_general(a, b, (((1,), (1,)), ((), ())), preferred_element_type=f32)


def _conv_in_kernel(x_ref, nw_ref, w_ref, z_ref, gb_ref):
    h = _rms(x_ref[...], nw_ref[...]).astype(bf16)
    gb = _dot(h, w_ref[:, 0:D_MODEL])
    gc = _dot(h, w_ref[:, D_MODEL:2 * D_MODEL])
    u = _dot(h, w_ref[:, 2 * D_MODEL:3 * D_MODEL])
    z_ref[...] = (gc * u).astype(bf16)
    gb_ref[...] = gb.astype(bf16)


def _conv_in(x, nw, w_in):
    return pl.pallas_call(
        _conv_in_kernel,
        grid=(N_TOK // TM,),
        in_specs=[pl.BlockSpec((TM, D_MODEL), lambda i: (i, 0)),
                  pl.BlockSpec((1, D_MODEL), lambda i: (0, 0)),
                  pl.BlockSpec((D_MODEL, 3 * D_MODEL), lambda i: (0, 0))],
        out_specs=[pl.BlockSpec((TM, D_MODEL), lambda i: (i, 0)),
                   pl.BlockSpec((TM, D_MODEL), lambda i: (i, 0))],
        out_shape=[jax.ShapeDtypeStruct((N_TOK, D_MODEL), bf16),
                   jax.ShapeDtypeStruct((N_TOK, D_MODEL), bf16)],
        compiler_params=_cparams(("arbitrary",)),
        name="conv_in",
    )(x, nw, w_in)


def _conv_out_kernel(x_ref, z_ref, zp_ref, zn_ref, gb_ref, k_ref, w_ref, *route_refs):
    o_ref = route_refs[len(ROUTE_IN_SPECS)]
    i = pl.program_id(0)
    tiles_per_seq = SEQ // TM
    first = (i % tiles_per_seq) == 0
    last = (i % tiles_per_seq) == tiles_per_seq - 1
    z = z_ref[...].astype(f32)
    prev_row = jnp.where(first, 0.0, zp_ref[7:8, :].astype(f32))
    next_row = jnp.where(last, 0.0, zn_ref[0:1, :].astype(f32))
    row = lax.broadcasted_iota(jnp.int32, z.shape, 0)
    z_m1 = jnp.where(row == 0, prev_row, pltpu.roll(z, 1, 0))
    z_p1 = jnp.where(row == TM - 1, next_row, pltpu.roll(z, TM - 1, 0))
    conv = z_m1 * k_ref[0:1, :] + z * k_ref[1:2, :] + z_p1 * k_ref[2:3, :]
    y = _dot((gb_ref[...].astype(f32) * conv).astype(bf16), w_ref[...])
    x_new = x_ref[...] + y
    o_ref[...] = x_new
    _route_tile(x_new, *route_refs[:len(ROUTE_IN_SPECS)], *route_refs[len(ROUTE_IN_SPECS) + 1:])


def _conv_out(x, z, gb, conv_k, w_out, route_ops):
    n_halo = N_TOK // 8
    per = TM // 8
    return pl.pallas_call(
        _conv_out_kernel,
        grid=(N_TOK // TM,),
        in_specs=[pl.BlockSpec((TM, D_MODEL), lambda i: (i, 0)),
                  pl.BlockSpec((TM, D_MODEL), lambda i: (i, 0)),
                  pl.BlockSpec((8, D_MODEL), lambda i: (jnp.maximum(i * per - 1, 0), 0)),
                  pl.BlockSpec((8, D_MODEL), lambda i: (jnp.minimum((i + 1) * per, n_halo - 1), 0)),
                  pl.BlockSpec((TM, D_MODEL), lambda i: (i, 0)),
                  pl.BlockSpec((3, D_MODEL), lambda i: (0, 0)),
                  pl.BlockSpec((D_MODEL, D_MODEL), lambda i: (0, 0))] + ROUTE_IN_SPECS,
        out_specs=[pl.BlockSpec((TM, D_MODEL), lambda i: (i, 0))] + ROUTE_OUT_SPECS,
        out_shape=[jax.ShapeDtypeStruct((N_TOK, D_MODEL), f32)] + ROUTE_OUT_SHAPES,
        scratch_shapes=ROUTE_SCRATCH,
        compiler_params=_cparams(("arbitrary",)),
        name="conv_out",
    )(x, z, z, z, gb, conv_k, w_out, *route_ops)


def _mla_proj_kernel(x_ref, nw_ref, wd_ref, qln_ref, kvln_ref, wqt_ref, wk_ref, wvt_ref,
                     qhn_ref, khn_ref, g128_ref, cs_ref, cost_ref, sint_ref,
                     qt_ref, k_ref, vt_ref):
    h = _rms(x_ref[...], nw_ref[...]).astype(bf16)
    lat = _dot(h, wd_ref[...])
    q_lat = _rms(lat[:, 0:Q_LORA_RANK], qln_ref[...]).astype(bf16)
    kv_lat = _rms(lat[:, Q_LORA_RANK:Q_LORA_RANK + KV_LORA_RANK], kvln_ref[...]).astype(bf16)
    kr_blk = lat[:, Q_LORA_RANK + KV_LORA_RANK:LATENT_AUG]
    lane = lax.broadcasted_iota(jnp.int32, kr_blk.shape, 1)
    kr_ss = jnp.sum(jnp.where(lane < QK_ROPE_DIM, kr_blk * kr_blk, 0.0), axis=-1, keepdims=True)
    t = kr_blk * g128_ref[...] * cs_ref[...]
    kr_roped = t + pltpu.roll(t, QK_ROPE_DIM, 1)

    k_nope = _dot(kv_lat, wk_ref[...])
    cos_t = cost_ref[0]
    sin_t = sint_ref[0]
    for hd in range(N_HEADS):
        kn = k_nope[:, hd * QK_NOPE_DIM:(hd + 1) * QK_NOPE_DIM]
        ss = jnp.sum(kn * kn, axis=-1, keepdims=True) + kr_ss
        r = lax.rsqrt(ss * (1.0 / QK_HEAD_DIM) + RMS_EPS)
        k_ref[0, hd, :, 0:QK_NOPE_DIM] = (kn * r * khn_ref[:, 0:QK_NOPE_DIM]).astype(bf16)
        k_ref[0, hd, :, QK_NOPE_DIM:QK_HEAD_DIM] = (kr_roped[:, 0:QK_ROPE_DIM] * r).astype(bf16)

        qt = _dot_nt(wqt_ref[hd * QK_HEAD_DIM:(hd + 1) * QK_HEAD_DIM, :], q_lat)
        rq = lax.rsqrt(jnp.mean(qt * qt, axis=0, keepdims=True) + RMS_EPS)
        qn = qt * rq * qhn_ref[...] * QK_SCALE_LOG2E
        half = QK_ROPE_DIM // 2
        x1 = qn[QK_NOPE_DIM:QK_NOPE_DIM + half, :]
        x2 = qn[QK_NOPE_DIM + half:QK_HEAD_DIM, :]
        qt_ref[0, hd, 0:QK_NOPE_DIM, :] = qn[0:QK_NOPE_DIM, :].astype(bf16)
        qt_ref[0, hd, QK_NOPE_DIM:QK_NOPE_DIM + half, :] = (x1 * cos_t - x2 * sin_t).astype(bf16)
        qt_ref[0, hd, QK_NOPE_DIM + half:QK_HEAD_DIM, :] = (x2 * cos_t + x1 * sin_t).astype(bf16)

        vt = _dot_nt(wvt_ref[hd * V_HEAD_DIM:(hd + 1) * V_HEAD_DIM, :], kv_lat)
        vt_ref[0, hd, 0:V_HEAD_DIM, :] = vt.astype(bf16)
        vt_ref[0, hd, V_HEAD_DIM:V_ROWS, :] = jnp.ones((V_ROWS - V_HEAD_DIM, TM), bf16)


def _mla_proj(x, nw, wd_aug, qln, kvln, wqt, wk, wvt, qhn_col, khn_row, g128, cs, cos_t, sin_t):
    tps = SEQ // TM
    const = lambda i: (0, 0)
    return pl.pallas_call(
        _mla_proj_kernel,
        grid=(N_TOK // TM,),
        in_specs=[pl.BlockSpec((TM, D_MODEL), lambda i: (i, 0)),
                  pl.BlockSpec((1, D_MODEL), const),
                  pl.BlockSpec((D_MODEL, LATENT_AUG), const),
                  pl.BlockSpec((1, Q_LORA_RANK), const),
                  pl.BlockSpec((1, KV_LORA_RANK), const),
                  pl.BlockSpec((N_HEADS * QK_HEAD_DIM, Q_LORA_RANK), const),
                  pl.BlockSpec((KV_LORA_RANK, N_HEADS * QK_NOPE_DIM), const),
                  pl.BlockSpec((N_HEADS * V_HEAD_DIM, KV_LORA_RANK), const),
                  pl.BlockSpec((QK_HEAD_DIM, 1), const),
                  pl.BlockSpec((1, QK_HEAD_DIM), const),
                  pl.BlockSpec((1, LANES), const),
                  pl.BlockSpec((TM, LANES), lambda i: (i, 0)),
                  pl.BlockSpec((1, QK_ROPE_DIM // 2, TM), lambda i: (i // tps, 0, i % tps)),
                  pl.BlockSpec((1, QK_ROPE_DIM // 2, TM), lambda i: (i // tps, 0, i % tps))],
        out_specs=[pl.BlockSpec((1, N_HEADS, QK_HEAD_DIM, TM), lambda i: (i // tps, 0, 0, i % tps)),
                   pl.BlockSpec((1, N_HEADS, TM, QK_HEAD_DIM), lambda i: (i // tps, 0, i % tps, 0)),
                   pl.BlockSpec((1, N_HEADS, V_ROWS, TM), lambda i: (i // tps, 0, 0, i % tps))],
        out_shape=[jax.ShapeDtypeStruct((BATCH, N_HEADS, QK_HEAD_DIM, SEQ), bf16),
                   jax.ShapeDtypeStruct((BATCH, N_HEADS, SEQ, QK_HEAD_DIM), bf16),
                   jax.ShapeDtypeStruct((BATCH, N_HEADS, V_ROWS, SEQ), bf16)],
        compiler_params=_cparams(("arbitrary",)),
        name="mla_proj",
    )(x, nw, wd_aug, qln, kvln, wqt, wk, wvt, qhn_col, khn_row, g128, cs, cos_t, sin_t)


def _attn_kernel(qt_ref, k_ref, vt_ref, o_ref, acc_ref, s0_ref, s1_ref, p0_ref, p1_ref):
    qt = qt_ref[0, 0]
    n = SEQ // TK
    acc_ref[...] = jnp.zeros_like(acc_ref)

    def scores(j, s_ref):
        off = pl.multiple_of(j * TK, TK)
        s = _dot(k_ref[0, 0, pl.ds(off, TK), :], qt)
        s_ref[...] = s
        return jnp.max(s, axis=0, keepdims=True)

    def softmax(s_ref, p_ref, m, cmax):
        m_new = jnp.maximum(m, cmax)
        alpha = jnp.exp2(m - m_new)
        p_ref[...] = jnp.exp2((s_ref[...] - m_new).astype(bf16))
        return m_new, alpha

    def values(j, p_ref, alpha):
        off = pl.multiple_of(j * TK, TK)
        pv = _dot(vt_ref[0, 0, :, pl.ds(off, TK)], p_ref[...])
        acc_ref[...] = alpha * acc_ref[...] + pv

    m = jnp.full((1, TQ), -jnp.inf, f32)
    c0 = scores(0, s0_ref)
    c1 = scores(1, s1_ref)
    m, a0 = softmax(s0_ref, p0_ref, m, c0)
    c0 = scores(2, s0_ref)
    m, a1 = softmax(s1_ref, p1_ref, m, c1)
    values(0, p0_ref, a0)

    def pair(jj, carry):
        m, a_prev, c0 = carry
        j = 2 * jj
        c1 = scores(j + 1, s1_ref)
        m, a0 = softmax(s0_ref, p0_ref, m, c0)
        values(j - 1, p1_ref, a_prev)
        c0 = scores(j + 2, s0_ref)
        m, a1 = softmax(s1_ref, p1_ref, m, c1)
        values(j, p0_ref, a0)
        return m, a1, c0

    m, a_prev, c0 = lax.fori_loop(1, n // 2 - 1, pair, (m, a1, c0))
    c1 = scores(n - 1, s1_ref)
    m, a0 = softmax(s0_ref, p0_ref, m, c0)
    values(n - 3, p1_ref, a_prev)
    m, a1 = softmax(s1_ref, p1_ref, m, c1)
    values(n - 2, p0_ref, a0)
    values(n - 1, p1_ref, a1)
    o_ref[0, 0] = (acc_ref[0:V_HEAD_DIM, :] / acc_ref[V_HEAD_DIM:V_HEAD_DIM + 1, :]).astype(bf16)


def _attention(qt, k, vt):
    return pl.pallas_call(
        _attn_kernel,
        grid=(BATCH, N_HEADS, SEQ // TQ),
        in_specs=[pl.BlockSpec((1, 1, QK_HEAD_DIM, TQ), lambda b, h, i: (b, h, 0, i)),
                  pl.BlockSpec((1, 1, SEQ, QK_HEAD_DIM), lambda b, h, i: (b, h, 0, 0)),
                  pl.BlockSpec((1, 1, V_ROWS, SEQ), lambda b, h, i: (b, h, 0, 0))],
        out_specs=pl.BlockSpec((1, 1, V_HEAD_DIM, TQ), lambda b, h, i: (b, h, 0, i)),
        out_shape=jax.ShapeDtypeStruct((BATCH, N_HEADS, V_HEAD_DIM, SEQ), bf16),
        scratch_shapes=[pltpu.VMEM((V_ROWS, TQ), f32),
                        pltpu.VMEM((TK, TQ), f32), pltpu.VMEM((TK, TQ), f32),
                        pltpu.VMEM((TK, TQ), bf16), pltpu.VMEM((TK, TQ), bf16)],
        compiler_params=_cparams(("arbitrary", "arbitrary", "arbitrary")),
        name="mla_attention",
    )(qt, k, vt)


def _mla_out_kernel(x_ref, ot_ref, w_ref, *route_refs):
    o_ref = route_refs[len(ROUTE_IN_SPECS)]
    yt = _dot(w_ref[...], ot_ref[0])
    x_new = x_ref[...] + yt.T
    o_ref[...] = x_new
    _route_tile(x_new, *route_refs[:len(ROUTE_IN_SPECS)], *route_refs[len(ROUTE_IN_SPECS) + 1:])


def _mla_out(x, ot, w_out_t, route_ops):
    tps = SEQ // TM
    return pl.pallas_call(
        _mla_out_kernel,
        grid=(N_TOK // TM,),
        in_specs=[pl.BlockSpec((TM, D_MODEL), lambda i: (i, 0)),
                  pl.BlockSpec((1, N_HEADS * V_HEAD_DIM, TM), lambda i: (i // tps, 0, i % tps)),
                  pl.BlockSpec((D_MODEL, N_HEADS * V_HEAD_DIM), lambda i: (0, 0))] + ROUTE_IN_SPECS,
        out_specs=[pl.BlockSpec((TM, D_MODEL), lambda i: (i, 0))] + ROUTE_OUT_SPECS,
        out_shape=[jax.ShapeDtypeStruct((N_TOK, D_MODEL), f32)] + ROUTE_OUT_SHAPES,
        scratch_shapes=ROUTE_SCRATCH,
        compiler_params=_cparams(("arbitrary",)),
        name="mla_out",
    )(x, ot, w_out_t, *route_ops)


def _first_argmax(v, n):
    idx = lax.broadcasted_iota(jnp.int32, v.shape, 0)
    mx = jnp.max(v, axis=0, keepdims=True)
    return jnp.min(jnp.where(v == mx, idx, n), axis=0, keepdims=True)


def _route_tile(x, nw_ref, wt_hi_ref, wt_lo_ref, bias_ref, tri_ref,
                idx_ref, gate_ref, cnt_ref, base_ref):
    i = pl.program_id(0)

    @pl.when(i == 0)
    def _():
        base_ref[...] = jnp.zeros_like(base_ref)

    h = _rms(x, nw_ref[...])
    h_hi = h.astype(bf16)
    h_lo = (h - h_hi.astype(f32)).astype(bf16)
    logits = (_dot_nt(wt_hi_ref[...], h_hi) + _dot_nt(wt_lo_ref[...], h_hi)
              + _dot_nt(wt_hi_ref[...], h_lo))
    g_logits = logits[0:N_GROUPS, :]
    g_exp = jnp.exp(g_logits - jnp.max(g_logits, axis=0, keepdims=True))
    g_prob = g_exp / jnp.sum(g_exp, axis=0, keepdims=True)
    g_sel = _first_argmax(g_logits + bias_ref[0:N_GROUPS, :], N_GROUPS)
    gidx = lax.broadcasted_iota(jnp.int32, g_logits.shape, 0)
    g_gate = jnp.sum(jnp.where(gidx == g_sel, g_prob, 0.0), axis=0, keepdims=True)

    e_logits = jnp.zeros((EXPERTS_PER_GROUP, TM), f32)
    e_bias = jnp.zeros((EXPERTS_PER_GROUP, TM), f32)
    for g in range(N_GROUPS):
        lo = N_GROUPS + g * EXPERTS_PER_GROUP
        e_logits = jnp.where(g_sel == g, logits[lo:lo + EXPERTS_PER_GROUP, :], e_logits)
        e_bias = jnp.where(g_sel == g, bias_ref[lo:lo + EXPERTS_PER_GROUP, :], e_bias)
    scored = e_logits + e_bias
    eidx = lax.broadcasted_iota(jnp.int32, scored.shape, 0)
    top0 = _first_argmax(scored, EXPERTS_PER_GROUP)
    top1 = _first_argmax(jnp.where(eidx == top0, -jnp.inf, scored), EXPERTS_PER_GROUP)
    e_exp = jnp.exp(e_logits - jnp.max(e_logits, axis=0, keepdims=True))
    e_prob = e_exp / jnp.sum(e_exp, axis=0, keepdims=True)
    p0 = jnp.sum(jnp.where(eidx == top0, e_prob, 0.0), axis=0, keepdims=True)
    p1 = jnp.sum(jnp.where(eidx == top1, e_prob, 0.0), axis=0, keepdims=True)
    psum = p0 + p1
    gate0 = g_gate * (p0 / psum)
    gate1 = g_gate * (p1 / psum)
    eid0 = g_sel * EXPERTS_PER_GROUP + top0
    eid1 = g_sel * EXPERTS_PER_GROUP + top1

    xidx = lax.broadcasted_iota(jnp.int32, (N_EXPERTS, TM), 0)
    oh0 = xidx == eid0
    oh1 = xidx == eid1
    pre0 = _dot(jnp.where(oh0, 1.0, 0.0).astype(bf16), tri_ref[...])
    pre1 = _dot(jnp.where(oh1, 1.0, 0.0).astype(bf16), tri_ref[...])
    tot0 = jnp.sum(jnp.where(oh0, 1.0, 0.0), axis=1, keepdims=True)
    tot1 = jnp.sum(jnp.where(oh1, 1.0, 0.0), axis=1, keepdims=True)
    base = base_ref[:, 0:1]
    rank0 = jnp.sum(jnp.where(oh0, base + pre0, 0.0), axis=0, keepdims=True)
    rank1 = jnp.sum(jnp.where(oh1, base + tot0 + pre1, 0.0), axis=0, keepdims=True)
    new_base = base + tot0 + tot1
    base_ref[...] = jnp.broadcast_to(new_base, base_ref.shape)
    cnt_ref[...] = jnp.broadcast_to(new_base, cnt_ref.shape)

    zi = jnp.zeros((4, TM), jnp.int32)
    idx_ref[...] = jnp.concatenate(
        [eid0, eid1, rank0.astype(jnp.int32), rank1.astype(jnp.int32), zi], axis=0)
    gate_ref[...] = jnp.concatenate([gate0, gate1, jnp.zeros((6, TM), f32)], axis=0)


def _const2(i):
    return (0, 0)


ROUTE_IN_SPECS = [pl.BlockSpec((1, D_MODEL), _const2),
                  pl.BlockSpec((LANES, D_MODEL), _const2),
                  pl.BlockSpec((LANES, D_MODEL), _const2),
                  pl.BlockSpec((LANES, 1), _const2),
                  pl.BlockSpec((TM, TM), _const2)]
ROUTE_OUT_SPECS = [pl.BlockSpec((8, TM), lambda i: (0, i)),
                   pl.BlockSpec((8, TM), lambda i: (0, i)),
                   pl.BlockSpec((N_EXPERTS, LANES), _const2)]
ROUTE_OUT_SHAPES = [jax.ShapeDtypeStruct((8, N_TOK), jnp.int32),
                    jax.ShapeDtypeStruct((8, N_TOK), f32),
                    jax.ShapeDtypeStruct((N_EXPERTS, LANES), f32)]
ROUTE_SCRATCH = [pltpu.VMEM((N_EXPERTS, LANES), f32)]


def _router_operands(nw, w_group, b_group, w_expert, b_expert, tri):
    wt = jnp.concatenate([w_group, w_expert], axis=1).T
    wt = jnp.pad(wt, ((0, LANES - wt.shape[0]), (0, 0)))
    wt_hi = wt.astype(bf16)
    wt_lo = (wt - wt_hi.astype(f32)).astype(bf16)
    bias_col = jnp.pad(jnp.concatenate([b_group, b_expert]), (0, LANES - N_GROUPS - N_EXPERTS))[:, None]
    return nw, wt_hi, wt_lo, bias_col, tri


N_ROW_TILES = N_TOK // TM_ROW
ROW_TILE = 8
PACK_TILE = 4
IDX_GROUPS = TM_ROW // LANES
IDX_TRIPS = LANES // ROW_UNROLL
IDX_PER_TRIP = IDX_GROUPS * ROW_UNROLL * TOP_K


def _issue_rows(idx_smem, s, row_tile, make_copy):
    for r8 in range(IDX_TRIPS):
        for g in range(IDX_GROUPS):
            for u in range(ROW_UNROLL):
                for k in range(TOP_K):
                    d = pl.multiple_of(idx_smem[s, r8, (g * ROW_UNROLL + u) * TOP_K + k] * row_tile,
                                       row_tile)
                    r = (g * LANES + r8 * ROW_UNROLL + u) * row_tile
                    make_copy(k, r, d).start(priority=k)


def _dest_tiles(dest):
    d = dest.reshape(TOP_K, N_ROW_TILES, IDX_GROUPS, IDX_TRIPS, ROW_UNROLL)
    return d.transpose(1, 3, 2, 4, 0).reshape(N_ROW_TILES, IDX_TRIPS, IDX_PER_TRIP)


def _dispatch_kernel(last_ref, x_ref, nw_ref, dest_hbm, xb_hbm, hbuf, zbuf, idx_smem, idx_sem,
                     row_sem, zero_sem):
    i = pl.program_id(0)
    slot = i % 2

    def idx_copy(tile, s):
        return pltpu.make_async_copy(dest_hbm.at[tile], idx_smem.at[s], idx_sem.at[s])

    @pl.when(i == 0)
    def _():
        zbuf[...] = jnp.zeros_like(zbuf)

        def zero_copy(row):
            row0 = pl.multiple_of(row * PACK_TILE, PACK_TILE)
            return pltpu.make_async_copy(zbuf, xb_hbm.at[pl.ds(row0, MOE_ROWS * PACK_TILE), :], zero_sem)

        def start(e, c):
            @pl.when(last_ref[e] >= 0)
            def _():
                zero_copy(last_ref[e]).start()
            return c

        def wait(e, c):
            @pl.when(last_ref[e] >= 0)
            def _():
                zero_copy(last_ref[e]).wait()
            return c

        def start_tail(b, c):
            zero_copy(b * MOE_ROWS).start()
            return c

        def wait_tail(b, c):
            zero_copy(b * MOE_ROWS).wait()
            return c

        n_used = last_ref[N_EXPERTS]
        lax.fori_loop(0, N_EXPERTS, start, 0)
        lax.fori_loop(n_used, N_MOE_BLOCKS, start_tail, 0)
        lax.fori_loop(0, N_EXPERTS, wait, 0)
        lax.fori_loop(n_used, N_MOE_BLOCKS, wait_tail, 0)

    def drain(s):
        for _ in range(TOP_K):
            pltpu.make_async_copy(hbuf.at[s], xb_hbm.at[pl.ds(0, TM_ROW * PACK_TILE), :],
                                  row_sem.at[s]).wait()

    @pl.when(i == 0)
    def _():
        idx_copy(0, 0).start()

    idx_copy(i, slot).wait()

    @pl.when(i + 1 < N_ROW_TILES)
    def _():
        idx_copy(i + 1, 1 - slot).start()

    h = _rms(x_ref[...], nw_ref[...])
    half = D_MODEL // 2
    lo = pltpu.bitcast(h[:, 0:half].astype(bf16).astype(f32), jnp.uint32)
    hi = pltpu.bitcast(h[:, half:D_MODEL].astype(bf16).astype(f32), jnp.uint32)
    packed = (lo >> 16) | hi
    for s in range(2):
        @pl.when(slot == s)
        def _(s=s):
            @pl.when(i >= 2)
            def _():
                drain(s)
            for c in range(PACK_TILE):
                hbuf[s, pl.ds(c, TM_ROW, stride=PACK_TILE), :] = packed[:, c * LANES:(c + 1) * LANES]
            _issue_rows(idx_smem, s, PACK_TILE, lambda k, r, d: pltpu.make_async_copy(
                hbuf.at[s, pl.ds(r, PACK_TILE), :], xb_hbm.at[pl.ds(d, PACK_TILE), :], row_sem.at[s]))

    @pl.when(i == N_ROW_TILES - 1)
    def _():
        drain(0)
        drain(1)


def _dispatch(x, nw, dest_tiles, zero_plan):
    grid_spec = pltpu.PrefetchScalarGridSpec(
        num_scalar_prefetch=1,
        grid=(N_ROW_TILES,),
        in_specs=[pl.BlockSpec((TM_ROW, D_MODEL), lambda i, lb: (i, 0)),
                  pl.BlockSpec((1, D_MODEL), lambda i, lb: (0, 0)),
                  pl.BlockSpec(memory_space=pl.ANY)],
        out_specs=pl.BlockSpec(memory_space=pl.ANY),
        scratch_shapes=[pltpu.VMEM((2, TM_ROW * PACK_TILE, LANES), jnp.uint32),
                        pltpu.VMEM((MOE_ROWS * PACK_TILE, LANES), jnp.uint32),
                        pltpu.SMEM((2, IDX_TRIPS, IDX_PER_TRIP), jnp.int32),
                        pltpu.SemaphoreType.DMA((2,)),
                        pltpu.SemaphoreType.DMA((2,)),
                        pltpu.SemaphoreType.DMA])
    return pl.pallas_call(
        _dispatch_kernel,
        grid_spec=grid_spec,
        out_shape=jax.ShapeDtypeStruct((N_MOE_ROWS * PACK_TILE, LANES), jnp.uint32),
        compiler_params=_cparams(("arbitrary",)),
        name="moe_dispatch",
    )(zero_plan, x, nw, dest_tiles)


MOE_BLOCK_SUBLANES = MOE_ROWS * ROW_TILE
MOE_PACKED_SUBLANES = MOE_ROWS * PACK_TILE


def _expert_kernel(first_ref, nblk_ref, nused_ref, xb_hbm, wg_ref, wu_ref, wd_ref, yb_hbm,
                   xbuf, ybuf, wg_b, wu_b, wd_b, in_sem, out_sem, zero_sem):
    e = pl.program_id(0)
    n_used = nused_ref[0]

    def block_rows(g):
        return pl.ds(pl.multiple_of(g * MOE_BLOCK_SUBLANES, MOE_BLOCK_SUBLANES), MOE_BLOCK_SUBLANES)

    def x_copy(g, s):
        rows = pl.ds(pl.multiple_of(g * MOE_PACKED_SUBLANES, MOE_PACKED_SUBLANES), MOE_PACKED_SUBLANES)
        return pltpu.make_async_copy(xb_hbm.at[rows, :], xbuf.at[s], in_sem.at[s])

    def y_copy(g, s):
        return pltpu.make_async_copy(ybuf.at[s], yb_hbm.at[block_rows(g), :], out_sem.at[s])

    @pl.when(e == 0)
    def _():
        x_copy(0, 0).start()

    @pl.when(nblk_ref[e] > 0)
    def _():
        wg_b[...] = wg_ref[...].astype(bf16)
        wu_b[...] = wu_ref[...].astype(bf16)
        wd_b[...] = wd_ref[...].astype(bf16)

    def block(i, c):
        g = first_ref[e] + i
        s = g % 2
        x_copy(g, s).wait()

        @pl.when(g + 1 < n_used)
        def _():
            x_copy(g + 1, 1 - s).start()

        @pl.when(g >= 2)
        def _():
            y_copy(g - 2, s).wait()

        words = [xbuf[s, pl.ds(k, MOE_ROWS, stride=PACK_TILE), :] for k in range(PACK_TILE)]
        lo = [pltpu.bitcast(w << 16, f32) for w in words]
        hi = [pltpu.bitcast(w & jnp.uint32(0xFFFF0000), f32) for w in words]
        xb = jnp.concatenate(lo + hi, axis=1).astype(bf16)
        hid = jax.nn.silu(_dot(xb, wg_b[...])) * _dot(xb, wu_b[...])
        y = _dot(hid.astype(bf16), wd_b[...])
        for k in range(ROW_TILE):
            ybuf[s, pl.ds(k, MOE_ROWS, stride=ROW_TILE), :] = y[:, k * LANES:(k + 1) * LANES]
        y_copy(g, s).start()
        return c

    lax.fori_loop(0, nblk_ref[e], block, 0)

    @pl.when(e == N_EXPERTS - 1)
    def _():
        @pl.when(n_used >= 2)
        def _():
            y_copy(n_used - 2, n_used % 2).wait()
        y_copy(n_used - 1, (n_used + 1) % 2).wait()
        ybuf[0] = jnp.zeros((MOE_BLOCK_SUBLANES, LANES), f32)

        def zero_copy(g):
            return pltpu.make_async_copy(ybuf.at[0], yb_hbm.at[block_rows(g), :], zero_sem)

        def start_tail(g, c):
            zero_copy(g).start()
            return c

        def wait_tail(g, c):
            zero_copy(g).wait()
            return c

        lax.fori_loop(n_used, N_MOE_BLOCKS, start_tail, 0)
        lax.fori_loop(n_used, N_MOE_BLOCKS, wait_tail, 0)


def _experts(layer, first_block, n_blocks, n_used, xb, w_gate, w_up, w_down):
    grid_spec = pltpu.PrefetchScalarGridSpec(
        num_scalar_prefetch=3,
        grid=(N_EXPERTS,),
        in_specs=[pl.BlockSpec(memory_space=pl.ANY),
                  pl.BlockSpec((None, None, D_MODEL, D_FF_EXPERT), lambda e, fb, nb, nu: (layer, e, 0, 0)),
                  pl.BlockSpec((None, None, D_MODEL, D_FF_EXPERT), lambda e, fb, nb, nu: (layer, e, 0, 0)),
                  pl.BlockSpec((None, None, D_FF_EXPERT, D_MODEL), lambda e, fb, nb, nu: (layer, e, 0, 0))],
        out_specs=pl.BlockSpec(memory_space=pl.ANY),
        scratch_shapes=[pltpu.VMEM((2, MOE_PACKED_SUBLANES, LANES), jnp.uint32),
                        pltpu.VMEM((2, MOE_BLOCK_SUBLANES, LANES), f32),
                        pltpu.VMEM((D_MODEL, D_FF_EXPERT), bf16),
                        pltpu.VMEM((D_MODEL, D_FF_EXPERT), bf16),
                        pltpu.VMEM((D_FF_EXPERT, D_MODEL), bf16),
                        pltpu.SemaphoreType.DMA((2,)),
                        pltpu.SemaphoreType.DMA((2,)),
                        pltpu.SemaphoreType.DMA])
    return pl.pallas_call(
        _expert_kernel,
        grid_spec=grid_spec,
        out_shape=jax.ShapeDtypeStruct((N_MOE_ROWS * ROW_TILE, LANES), f32),
        compiler_params=_cparams(("arbitrary",)),
        name="moe_experts",
    )(first_block, n_blocks, n_used, xb, w_gate, w_up, w_down)


def _combine_kernel(x_ref, gate_ref, dest_hbm, yb_hbm, o_ref, ybuf, idx_smem, idx_sem, row_sem):
    i = pl.program_id(0)
    slot = i % 2

    def idx_copy(tile, s):
        return pltpu.make_async_copy(dest_hbm.at[tile], idx_smem.at[s], idx_sem.at[s])

    def gather(s):
        _issue_rows(idx_smem, s, ROW_TILE, lambda k, r, d: pltpu.make_async_copy(
            yb_hbm.at[pl.ds(d, ROW_TILE), :], ybuf.at[s, k, pl.ds(r, ROW_TILE), :], row_sem.at[s]))

    @pl.when(i == 0)
    def _():
        idx_copy(0, 0).start()
        idx_copy(0, 0).wait()
        gather(0)
        idx_copy(1, 1).start()

    for s in range(2):
        @pl.when(jnp.logical_and(i + 1 < N_ROW_TILES, slot == 1 - s))
        def _(s=s):
            idx_copy(i + 1, s).wait()
            gather(s)

    @pl.when(i + 2 < N_ROW_TILES)
    def _():
        idx_copy(i + 2, slot).start()

    g0 = gate_ref[:, 0:1]
    g1 = gate_ref[:, 1:2]
    for s in range(2):
        @pl.when(slot == s)
        def _(s=s):
            for k in range(TOP_K):
                pltpu.make_async_copy(yb_hbm.at[pl.ds(0, TM_ROW * ROW_TILE), :], ybuf.at[s, k],
                                      row_sem.at[s]).wait()
            for c in range(ROW_TILE):
                y0 = ybuf[s, 0, pl.ds(c, TM_ROW, stride=ROW_TILE), :]
                y1 = ybuf[s, 1, pl.ds(c, TM_ROW, stride=ROW_TILE), :]
                lanes = slice(c * LANES, (c + 1) * LANES)
                o_ref[:, lanes] = x_ref[:, lanes] + (y0 * g0 + y1 * g1)


def _combine(x, gate_cols, dest_tiles, yb):
    return pl.pallas_call(
        _combine_kernel,
        grid=(N_ROW_TILES,),
        in_specs=[pl.BlockSpec((TM_ROW, D_MODEL), lambda i: (i, 0)),
                  pl.BlockSpec((TM_ROW, TOP_K), lambda i: (i, 0)),
                  pl.BlockSpec(memory_space=pl.ANY),
                  pl.BlockSpec(memory_space=pl.ANY)],
        out_specs=pl.BlockSpec((TM_ROW, D_MODEL), lambda i: (i, 0)),
        out_shape=jax.ShapeDtypeStruct((N_TOK, D_MODEL), f32),
        scratch_shapes=[pltpu.VMEM((2, TOP_K, TM_ROW * ROW_TILE, LANES), f32),
                        pltpu.SMEM((2, IDX_TRIPS, IDX_PER_TRIP), jnp.int32),
                        pltpu.SemaphoreType.DMA((2,)),
                        pltpu.SemaphoreType.DMA((2,))],
        compiler_params=_cparams(("arbitrary",)),
        name="moe_combine",
    )(x, gate_cols, dest_tiles, yb)


def _hier_moe(layer, x, nw, idx, gates, cnt, w_gate, w_up, w_down):
    counts = cnt[:, 0].astype(jnp.int32)
    padded = (counts + MOE_ROWS - 1) // MOE_ROWS * MOE_ROWS
    pends = jnp.cumsum(padded)
    pstarts = pends - padded
    eids = jnp.arange(N_EXPERTS, dtype=jnp.int32)[:, None, None]
    seg_start = jnp.sum(jnp.where(idx[None, 0:2] == eids, pstarts[:, None, None], 0), axis=0)
    dest = seg_start + idx[2:4]
    dest_tiles = _dest_tiles(dest)
    n_used = (pends[-1:] // MOE_ROWS).astype(jnp.int32)

    zero_plan = jnp.concatenate([jnp.where(padded > 0, pends - MOE_ROWS, -1), n_used]).astype(jnp.int32)
    xb = _dispatch(x, nw, dest_tiles, zero_plan)
    yb = _experts(layer, (pstarts // MOE_ROWS).astype(jnp.int32), (padded // MOE_ROWS).astype(jnp.int32),
                  n_used, xb, w_gate, w_up, w_down)
    return _combine(x, gates[0:2].T, dest_tiles, yb)


def kernel(x, positions, conv_norm_w, conv_w_in, conv_k, conv_w_out, mla_norm_w, mla_w_down, mla_q_lat_norm, mla_kv_lat_norm, mla_w_q_up, mla_w_kv_up, mla_q_head_norm, mla_k_head_norm, mla_w_out, ffn_norm_w, router_w_group, router_b_group, router_w_expert, router_b_expert, exp_w_gate, exp_w_up, exp_w_down):
    half = QK_ROPE_DIM // 2
    inv_freq = ROPE_THETA ** (-jnp.arange(0, QK_ROPE_DIM, 2, dtype=f32) / QK_ROPE_DIM)
    ang_t = inv_freq[None, :, None] * positions.astype(f32)[:, None, :]
    cos_t = jnp.cos(ang_t)
    sin_t = jnp.sin(ang_t)
    cos = cos_t.transpose(0, 2, 1)
    sin = sin_t.transpose(0, 2, 1)
    cs = jnp.concatenate([cos, cos, -sin, sin], axis=-1).reshape(N_TOK, LANES)
    row = lax.broadcasted_iota(jnp.int32, (TM, TM), 0)
    col = lax.broadcasted_iota(jnp.int32, (TM, TM), 1)
    tri = (row < col).astype(bf16)

    xt = x.reshape(N_TOK, D_MODEL)
    for i in range(DEPTH):
        j = i // N_MIXERS
        route_ops = _router_operands(ffn_norm_w[i][None, :], router_w_group[i], router_b_group[i],
                                     router_w_expert[i], router_b_expert[i], tri)
        if i % N_MIXERS == 0:
            z, gb = _conv_in(xt, conv_norm_w[j][None, :], conv_w_in[j].astype(bf16))
            xt, idx, gates, cnt = _conv_out(xt, z, gb, conv_k[j], conv_w_out[j].astype(bf16), route_ops)
        else:
            wd = mla_w_down[j]
            ro = Q_LORA_RANK + KV_LORA_RANK
            wd_aug = jnp.concatenate([wd, wd[:, ro + half:ro + 2 * half], wd[:, ro:ro + half]],
                                     axis=1).astype(bf16)
            wkv = mla_w_kv_up[j].reshape(KV_LORA_RANK, N_HEADS, QK_NOPE_DIM + V_HEAD_DIM)
            wk = wkv[:, :, :QK_NOPE_DIM].reshape(KV_LORA_RANK, N_HEADS * QK_NOPE_DIM).astype(bf16)
            wvt = wkv[:, :, QK_NOPE_DIM:].reshape(KV_LORA_RANK, N_HEADS * V_HEAD_DIM).T.astype(bf16)
            wqt = mla_w_q_up[j].T.astype(bf16)
            khn = mla_k_head_norm[j]
            kw1 = khn[QK_NOPE_DIM:QK_NOPE_DIM + half]
            kw2 = khn[QK_NOPE_DIM + half:]
            g128 = jnp.concatenate([kw1, kw2, kw2, kw1])[None, :]
            qt, k, vt = _mla_proj(xt, mla_norm_w[j][None, :], wd_aug, mla_q_lat_norm[j][None, :],
                                  mla_kv_lat_norm[j][None, :], wqt, wk, wvt,
                                  mla_q_head_norm[j][:, None], khn[None, :], g128, cs, cos_t, sin_t)
            ot = _attention(qt, k, vt)
            xt, idx, gates, cnt = _mla_out(xt, ot.reshape(BATCH, N_HEADS * V_HEAD_DIM, SEQ),
                                           mla_w_out[j].T.astype(bf16), route_ops)
        xt = _hier_moe(i, xt, ffn_norm_w[i][None, :], idx, gates, cnt, exp_w_gate, exp_w_up, exp_w_down)
    return xt.reshape(BATCH, SEQ, D_MODEL)
```

```python
import math

import jax
import jax.numpy as jnp
from jax import lax
from jax.experimental import pallas as pl
from jax.experimental.pallas import tpu as pltpu

D_MODEL = 1024
BATCH = 2
SEQ = 16384
DEPTH = 4
N_TOK = BATCH * SEQ
N_MIXERS = 2
N_HEADS = 8
Q_LORA_RANK = 384
KV_LORA_RANK = 256
QK_NOPE_DIM = 128
QK_ROPE_DIM = 64
QK_HEAD_DIM = QK_NOPE_DIM + QK_ROPE_DIM
V_HEAD_DIM = 128
V_ROWS = V_HEAD_DIM + 16
LATENT_DIM = Q_LORA_RANK + KV_LORA_RANK + QK_ROPE_DIM
LATENT_AUG = LATENT_DIM + QK_ROPE_DIM
ROPE_THETA = 10000.0
N_GROUPS = 8
EXPERTS_PER_GROUP = 8
N_EXPERTS = N_GROUPS * EXPERTS_PER_GROUP
TOP_K = 2
D_FF_EXPERT = 384
RMS_EPS = 1e-6

LANES = 128
VMEM_LIMIT = 56 * 1024 * 1024

TM = 1024
TM_ROW = 512
ROW_UNROLL = 8
MOE_ROWS = 256
N_MOE_BLOCKS = N_TOK * TOP_K // MOE_ROWS + N_EXPERTS
N_MOE_ROWS = N_MOE_BLOCKS * MOE_ROWS
TQ = 2048
TK = 512
QK_SCALE_LOG2E = (QK_HEAD_DIM ** -0.5) * math.log2(math.e)

bf16 = jnp.bfloat16
f32 = jnp.float32


def _cparams(sem):
    return pltpu.CompilerParams(dimension_semantics=sem, vmem_limit_bytes=VMEM_LIMIT)


def _rms(x, w):
    return x * lax.rsqrt(jnp.mean(x * x, axis=-1, keepdims=True) + RMS_EPS) * w


def _dot(a, b):
    return jnp.dot(a, b, preferred_element_type=f32)


def _dot_nt(a, b):
    return lax.dot_general(a, b, (((1,), (1,)), ((), ())), preferred_element_type=f32)


def _conv_in_kernel(x_ref, nw_ref, w_ref, z_ref, gb_ref):
    h = _rms(x_ref[...], nw_ref[...]).astype(bf16)
    gb = _dot(h, w_ref[:, 0:D_MODEL])
    gc = _dot(h, w_ref[:, D_MODEL:2 * D_MODEL])
    u = _dot(h, w_ref[:, 2 * D_MODEL:3 * D_MODEL])
    z_ref[...] = (gc * u).astype(bf16)
    gb_ref[...] = gb.astype(bf16)


def _conv_in(x, nw, w_in):
    return pl.pallas_call(
        _conv_in_kernel,
        grid=(N_TOK // TM,),
        in_specs=[pl.BlockSpec((TM, D_MODEL), lambda i: (i, 0)),
                  pl.BlockSpec((1, D_MODEL), lambda i: (0, 0)),
                  pl.BlockSpec((D_MODEL, 3 * D_MODEL), lambda i: (0, 0))],
        out_specs=[pl.BlockSpec((TM, D_MODEL), lambda i: (i, 0)),
                   pl.BlockSpec((TM, D_MODEL), lambda i: (i, 0))],
        out_shape=[jax.ShapeDtypeStruct((N_TOK, D_MODEL), bf16),
                   jax.ShapeDtypeStruct((N_TOK, D_MODEL), bf16)],
        compiler_params=_cparams(("arbitrary",)),
        name="conv_in",
    )(x, nw, w_in)


def _conv_out_kernel(x_ref, z_ref, zp_ref, zn_ref, gb_ref, k_ref, w_ref, *route_refs):
    o_ref = route_refs[len(ROUTE_IN_SPECS)]
    i = pl.program_id(0)
    tiles_per_seq = SEQ // TM
    first = (i % tiles_per_seq) == 0
    last = (i % tiles_per_seq) == tiles_per_seq - 1
    z = z_ref[...].astype(f32)
    prev_row = jnp.where(first, 0.0, zp_ref[7:8, :].astype(f32))
    next_row = jnp.where(last, 0.0, zn_ref[0:1, :].astype(f32))
    row = lax.broadcasted_iota(jnp.int32, z.shape, 0)
    z_m1 = jnp.where(row == 0, prev_row, pltpu.roll(z, 1, 0))
    z_p1 = jnp.where(row == TM - 1, next_row, pltpu.roll(z, TM - 1, 0))
    conv = z_m1 * k_ref[0:1, :] + z * k_ref[1:2, :] + z_p1 * k_ref[2:3, :]
    y = _dot((gb_ref[...].astype(f32) * conv).astype(bf16), w_ref[...])
    x_new = x_ref[...] + y
    o_ref[...] = x_new
    _route_tile(x_new, *route_refs[:len(ROUTE_IN_SPECS)], *route_refs[len(ROUTE_IN_SPECS) + 1:])


def _conv_out(x, z, gb, conv_k, w_out, route_ops):
    n_halo = N_TOK // 8
    per = TM // 8
    return pl.pallas_call(
        _conv_out_kernel,
        grid=(N_TOK // TM,),
        in_specs=[pl.BlockSpec((TM, D_MODEL), lambda i: (i, 0)),
                  pl.BlockSpec((TM, D_MODEL), lambda i: (i, 0)),
                  pl.BlockSpec((8, D_MODEL), lambda i: (jnp.maximum(i * per - 1, 0), 0)),
                  pl.BlockSpec((8, D_MODEL), lambda i: (jnp.minimum((i + 1) * per, n_halo - 1), 0)),
                  pl.BlockSpec((TM, D_MODEL), lambda i: (i, 0)),
                  pl.BlockSpec((3, D_MODEL), lambda i: (0, 0)),
                  pl.BlockSpec((D_MODEL, D_MODEL), lambda i: (0, 0))] + ROUTE_IN_SPECS,
        out_specs=[pl.BlockSpec((TM, D_MODEL), lambda i: (i, 0))] + ROUTE_OUT_SPECS,
        out_shape=[jax.ShapeDtypeStruct((N_TOK, D_MODEL), f32)] + ROUTE_OUT_SHAPES,
        scratch_shapes=ROUTE_SCRATCH,
        compiler_params=_cparams(("arbitrary",)),
        name="conv_out",
    )(x, z, z, z, gb, conv_k, w_out, *route_ops)


def _mla_proj_kernel(x_ref, nw_ref, wd_ref, qln_ref, kvln_ref, wqt_ref, wk_ref, wvt_ref,
                     qhn_ref, khn_ref, g128_ref, cs_ref, cost_ref, sint_ref,
                     qt_ref, k_ref, vt_ref):
    h = _rms(x_ref[...], nw_ref[...]).astype(bf16)
    lat = _dot(h, wd_ref[...])
    q_lat = _rms(lat[:, 0:Q_LORA_RANK], qln_ref[...]).astype(bf16)
    kv_lat = _rms(lat[:, Q_LORA_RANK:Q_LORA_RANK + KV_LORA_RANK], kvln_ref[...]).astype(bf16)
    kr_blk = lat[:, Q_LORA_RANK + KV_LORA_RANK:LATENT_AUG]
    lane = lax.broadcasted_iota(jnp.int32, kr_blk.shape, 1)
    kr_ss = jnp.sum(jnp.where(lane < QK_ROPE_DIM, kr_blk * kr_blk, 0.0), axis=-1, keepdims=True)
    t = kr_blk * g128_ref[...] * cs_ref[...]
    kr_roped = t + pltpu.roll(t, QK_ROPE_DIM, 1)

    k_nope = _dot(kv_lat, wk_ref[...])
    cos_t = cost_ref[0]
    sin_t = sint_ref[0]
    for hd in range(N_HEADS):
        kn = k_nope[:, hd * QK_NOPE_DIM:(hd + 1) * QK_NOPE_DIM]
        ss = jnp.sum(kn * kn, axis=-1, keepdims=True) + kr_ss
        r = lax.rsqrt(ss * (1.0 / QK_HEAD_DIM) + RMS_EPS)
        k_ref[0, hd, :, 0:QK_NOPE_DIM] = (kn * r * khn_ref[:, 0:QK_NOPE_DIM]).astype(bf16)
        k_ref[0, hd, :, QK_NOPE_DIM:QK_HEAD_DIM] = (kr_roped[:, 0:QK_ROPE_DIM] * r).astype(bf16)

        qt = _dot_nt(wqt_ref[hd * QK_HEAD_DIM:(hd + 1) * QK_HEAD_DIM, :], q_lat)
        rq = lax.rsqrt(jnp.mean(qt * qt, axis=0, keepdims=True) + RMS_EPS)
        qn = qt * rq * qhn_ref[...] * QK_SCALE_LOG2E
        half = QK_ROPE_DIM // 2
        x1 = qn[QK_NOPE_DIM:QK_NOPE_DIM + half, :]
        x2 = qn[QK_NOPE_DIM + half:QK_HEAD_DIM, :]
        qt_ref[0, hd, 0:QK_NOPE_DIM, :] = qn[0:QK_NOPE_DIM, :].astype(bf16)
        qt_ref[0, hd, QK_NOPE_DIM:QK_NOPE_DIM + half, :] = (x1 * cos_t - x2 * sin_t).astype(bf16)
        qt_ref[0, hd, QK_NOPE_DIM + half:QK_HEAD_DIM, :] = (x2 * cos_t + x1 * sin_t).astype(bf16)

        vt = _dot_nt(wvt_ref[hd * V_HEAD_DIM:(hd + 1) * V_HEAD_DIM, :], kv_lat)
        vt_ref[0, hd, 0:V_HEAD_DIM, :] = vt.astype(bf16)
        vt_ref[0, hd, V_HEAD_DIM:V_ROWS, :] = jnp.ones((V_ROWS - V_HEAD_DIM, TM), bf16)


def _mla_proj(x, nw, wd_aug, qln, kvln, wqt, wk, wvt, qhn_col, khn_row, g128, cs, cos_t, sin_t):
    tps = SEQ // TM
    const = lambda i: (0, 0)
    return pl.pallas_call(
        _mla_proj_kernel,
        grid=(N_TOK // TM,),
        in_specs=[pl.BlockSpec((TM, D_MODEL), lambda i: (i, 0)),
                  pl.BlockSpec((1, D_MODEL), const),
                  pl.BlockSpec((D_MODEL, LATENT_AUG), const),
                  pl.BlockSpec((1, Q_LORA_RANK), const),
                  pl.BlockSpec((1, KV_LORA_RANK), const),
                  pl.BlockSpec((N_HEADS * QK_HEAD_DIM, Q_LORA_RANK), const),
                  pl.BlockSpec((KV_LORA_RANK, N_HEADS * QK_NOPE_DIM), const),
                  pl.BlockSpec((N_HEADS * V_HEAD_DIM, KV_LORA_RANK), const),
                  pl.BlockSpec((QK_HEAD_DIM, 1), const),
                  pl.BlockSpec((1, QK_HEAD_DIM), const),
                  pl.BlockSpec((1, LANES), const),
                  pl.BlockSpec((TM, LANES), lambda i: (i, 0)),
                  pl.BlockSpec((1, QK_ROPE_DIM // 2, TM), lambda i: (i // tps, 0, i % tps)),
                  pl.BlockSpec((1, QK_ROPE_DIM // 2, TM), lambda i: (i // tps, 0, i % tps))],
        out_specs=[pl.BlockSpec((1, N_HEADS, QK_HEAD_DIM, TM), lambda i: (i // tps, 0, 0, i % tps)),
                   pl.BlockSpec((1, N_HEADS, TM, QK_HEAD_DIM), lambda i: (i // tps, 0, i % tps, 0)),
                   pl.BlockSpec((1, N_HEADS, V_ROWS, TM), lambda i: (i // tps, 0, 0, i % tps))],
        out_shape=[jax.ShapeDtypeStruct((BATCH, N_HEADS, QK_HEAD_DIM, SEQ), bf16),
                   jax.ShapeDtypeStruct((BATCH, N_HEADS, SEQ, QK_HEAD_DIM), bf16),
                   jax.ShapeDtypeStruct((BATCH, N_HEADS, V_ROWS, SEQ), bf16)],
        compiler_params=_cparams(("arbitrary",)),
        name="mla_proj",
    )(x, nw, wd_aug, qln, kvln, wqt, wk, wvt, qhn_col, khn_row, g128, cs, cos_t, sin_t)


def _attn_kernel(qt_ref, k_ref, vt_ref, o_ref, acc_ref, s0_ref, s1_ref, p0_ref, p1_ref):
    qt = qt_ref[0, 0]
    n = SEQ // TK
    acc_ref[...] = jnp.zeros_like(acc_ref)

    def scores(j, s_ref):
        off = pl.multiple_of(j * TK, TK)
        s = _dot(k_ref[0, 0, pl.ds(off, TK), :], qt)
        s_ref[...] = s
        return jnp.max(s, axis=0, keepdims=True)

    def softmax(s_ref, p_ref, m, cmax):
        m_new = jnp.maximum(m, cmax)
        alpha = jnp.exp2(m - m_new)
        p_ref[...] = jnp.exp2((s_ref[...] - m_new).astype(bf16))
        return m_new, alpha

    def values(j, p_ref, alpha):
        off = pl.multiple_of(j * TK, TK)
        pv = _dot(vt_ref[0, 0, :, pl.ds(off, TK)], p_ref[...])
        acc_ref[...] = alpha * acc_ref[...] + pv

    m = jnp.full((1, TQ), -jnp.inf, f32)
    c0 = scores(0, s0_ref)
    c1 = scores(1, s1_ref)
    m, a0 = softmax(s0_ref, p0_ref, m, c0)
    c0 = scores(2, s0_ref)
    m, a1 = softmax(s1_ref, p1_ref, m, c1)
    values(0, p0_ref, a0)

    def pair(jj, carry):
        m, a_prev, c0 = carry
        j = 2 * jj
        c1 = scores(j + 1, s1_ref)
        m, a0 = softmax(s0_ref, p0_ref, m, c0)
        values(j - 1, p1_ref, a_prev)
        c0 = scores(j + 2, s0_ref)
        m, a1 = softmax(s1_ref, p1_ref, m, c1)
        values(j, p0_ref, a0)
        return m, a1, c0

    m, a_prev, c0 = lax.fori_loop(1, n // 2 - 1, pair, (m, a1, c0))
    c1 = scores(n - 1, s1_ref)
    m, a0 = softmax(s0_ref, p0_ref, m, c0)
    values(n - 3, p1_ref, a_prev)
    m, a1 = softmax(s1_ref, p1_ref, m, c1)
    values(n - 2, p0_ref, a0)
    values(n - 1, p1_ref, a1)
    o_ref[0, 0] = (acc_ref[0:V_HEAD_DIM, :] / acc_ref[V_HEAD_DIM:V_HEAD_DIM + 1, :]).astype(bf16)


def _attention(qt, k, vt):
    return pl.pallas_call(
        _attn_kernel,
        grid=(BATCH, N_HEADS, SEQ // TQ),
        in_specs=[pl.BlockSpec((1, 1, QK_HEAD_DIM, TQ), lambda b, h, i: (b, h, 0, i)),
                  pl.BlockSpec((1, 1, SEQ, QK_HEAD_DIM), lambda b, h, i: (b, h, 0, 0)),
                  pl.BlockSpec((1, 1, V_ROWS, SEQ), lambda b, h, i: (b, h, 0, 0))],
        out_specs=pl.BlockSpec((1, 1, V_HEAD_DIM, TQ), lambda b, h, i: (b, h, 0, i)),
        out_shape=jax.ShapeDtypeStruct((BATCH, N_HEADS, V_HEAD_DIM, SEQ), bf16),
        scratch_shapes=[pltpu.VMEM((V_ROWS, TQ), f32),
                        pltpu.VMEM((TK, TQ), f32), pltpu.VMEM((TK, TQ), f32),
                        pltpu.VMEM((TK, TQ), bf16), pltpu.VMEM((TK, TQ), bf16)],
        compiler_params=_cparams(("arbitrary", "arbitrary", "arbitrary")),
        name="mla_attention",
    )(qt, k, vt)


def _mla_out_kernel(x_ref, ot_ref, w_ref, *route_refs):
    o_ref = route_refs[len(ROUTE_IN_SPECS)]
    yt = _dot(w_ref[...], ot_ref[0])
    x_new = x_ref[...] + yt.T
    o_ref[...] = x_new
    _route_tile(x_new, *route_refs[:len(ROUTE_IN_SPECS)], *route_refs[len(ROUTE_IN_SPECS) + 1:])


def _mla_out(x, ot, w_out_t, route_ops):
    tps = SEQ // TM
    return pl.pallas_call(
        _mla_out_kernel,
        grid=(N_TOK // TM,),
        in_specs=[pl.BlockSpec((TM, D_MODEL), lambda i: (i, 0)),
                  pl.BlockSpec((1, N_HEADS * V_HEAD_DIM, TM), lambda i: (i // tps, 0, i % tps)),
                  pl.BlockSpec((D_MODEL, N_HEADS * V_HEAD_DIM), lambda i: (0, 0))] + ROUTE_IN_SPECS,
        out_specs=[pl.BlockSpec((TM, D_MODEL), lambda i: (i, 0))] + ROUTE_OUT_SPECS,
        out_shape=[jax.ShapeDtypeStruct((N_TOK, D_MODEL), f32)] + ROUTE_OUT_SHAPES,
        scratch_shapes=ROUTE_SCRATCH,
        compiler_params=_cparams(("arbitrary",)),
        name="mla_out",
    )(x, ot, w_out_t, *route_ops)


def _first_argmax(v, n):
    idx = lax.broadcasted_iota(jnp.int32, v.shape, 0)
    mx = jnp.max(v, axis=0, keepdims=True)
    return jnp.min(jnp.where(v == mx, idx, n), axis=0, keepdims=True)


def _route_tile(x, nw_ref, wt_hi_ref, wt_lo_ref, bias_ref, tri_ref,
                idx_ref, gate_ref, cnt_ref, base_ref):
    i = pl.program_id(0)

    @pl.when(i == 0)
    def _():
        base_ref[...] = jnp.zeros_like(base_ref)

    h = _rms(x, nw_ref[...])
    h_hi = h.astype(bf16)
    h_lo = (h - h_hi.astype(f32)).astype(bf16)
    logits = (_dot_nt(wt_hi_ref[...], h_hi) + _dot_nt(wt_lo_ref[...], h_hi)
              + _dot_nt(wt_hi_ref[...], h_lo))
    g_logits = logits[0:N_GROUPS, :]
    g_exp = jnp.exp(g_logits - jnp.max(g_logits, axis=0, keepdims=True))
    g_prob = g_exp / jnp.sum(g_exp, axis=0, keepdims=True)
    g_sel = _first_argmax(g_logits + bias_ref[0:N_GROUPS, :], N_GROUPS)
    gidx = lax.broadcasted_iota(jnp.int32, g_logits.shape, 0)
    g_gate = jnp.sum(jnp.where(gidx == g_sel, g_prob, 0.0), axis=0, keepdims=True)

    e_logits = jnp.zeros((EXPERTS_PER_GROUP, TM), f32)
    e_bias = jnp.zeros((EXPERTS_PER_GROUP, TM), f32)
    for g in range(N_GROUPS):
        lo = N_GROUPS + g * EXPERTS_PER_GROUP
        e_logits = jnp.where(g_sel == g, logits[lo:lo + EXPERTS_PER_GROUP, :], e_logits)
        e_bias = jnp.where(g_sel == g, bias_ref[lo:lo + EXPERTS_PER_GROUP, :], e_bias)
    scored = e_logits + e_bias
    eidx = lax.broadcasted_iota(jnp.int32, scored.shape, 0)
    top0 = _first_argmax(scored, EXPERTS_PER_GROUP)
    top1 = _first_argmax(jnp.where(eidx == top0, -jnp.inf, scored), EXPERTS_PER_GROUP)
    e_exp = jnp.exp(e_logits - jnp.max(e_logits, axis=0, keepdims=True))
    e_prob = e_exp / jnp.sum(e_exp, axis=0, keepdims=True)
    p0 = jnp.sum(jnp.where(eidx == top0, e_prob, 0.0), axis=0, keepdims=True)
    p1 = jnp.sum(jnp.where(eidx == top1, e_prob, 0.0), axis=0, keepdims=True)
    psum = p0 + p1
    gate0 = g_gate * (p0 / psum)
    gate1 = g_gate * (p1 / psum)
    eid0 = g_sel * EXPERTS_PER_GROUP + top0
    eid1 = g_sel * EXPERTS_PER_GROUP + top1

    xidx = lax.broadcasted_iota(jnp.int32, (N_EXPERTS, TM), 0)
    oh0 = xidx == eid0
    oh1 = xidx == eid1
    pre0 = _dot(jnp.where(oh0, 1.0, 0.0).astype(bf16), tri_ref[...])
    pre1 = _dot(jnp.where(oh1, 1.0, 0.0).astype(bf16), tri_ref[...])
    tot0 = jnp.sum(jnp.where(oh0, 1.0, 0.0), axis=1, keepdims=True)
    tot1 = jnp.sum(jnp.where(oh1, 1.0, 0.0), axis=1, keepdims=True)
    base = base_ref[:, 0:1]
    rank0 = jnp.sum(jnp.where(oh0, base + pre0, 0.0), axis=0, keepdims=True)
    rank1 = jnp.sum(jnp.where(oh1, base + tot0 + pre1, 0.0), axis=0, keepdims=True)
    new_base = base + tot0 + tot1
    base_ref[...] = jnp.broadcast_to(new_base, base_ref.shape)
    cnt_ref[...] = jnp.broadcast_to(new_base, cnt_ref.shape)

    zi = jnp.zeros((4, TM), jnp.int32)
    idx_ref[...] = jnp.concatenate(
        [eid0, eid1, rank0.astype(jnp.int32), rank1.astype(jnp.int32), zi], axis=0)
    gate_ref[...] = jnp.concatenate([gate0, gate1, jnp.zeros((6, TM), f32)], axis=0)


def _const2(i):
    return (0, 0)


ROUTE_IN_SPECS = [pl.BlockSpec((1, D_MODEL), _const2),
                  pl.BlockSpec((LANES, D_MODEL), _const2),
                  pl.BlockSpec((LANES, D_MODEL), _const2),
                  pl.BlockSpec((LANES, 1), _const2),
                  pl.BlockSpec((TM, TM), _const2)]
ROUTE_OUT_SPECS = [pl.BlockSpec((8, TM), lambda i: (0, i)),
                   pl.BlockSpec((8, TM), lambda i: (0, i)),
                   pl.BlockSpec((N_EXPERTS, LANES), _const2)]
ROUTE_OUT_SHAPES = [jax.ShapeDtypeStruct((8, N_TOK), jnp.int32),
                    jax.ShapeDtypeStruct((8, N_TOK), f32),
                    jax.ShapeDtypeStruct((N_EXPERTS, LANES), f32)]
ROUTE_SCRATCH = [pltpu.VMEM((N_EXPERTS, LANES), f32)]


def _router_operands(nw, w_group, b_group, w_expert, b_expert, tri):
    wt = jnp.concatenate([w_group, w_expert], axis=1).T
    wt = jnp.pad(wt, ((0, LANES - wt.shape[0]), (0, 0)))
    wt_hi = wt.astype(bf16)
    wt_lo = (wt - wt_hi.astype(f32)).astype(bf16)
    bias_col = jnp.pad(jnp.concatenate([b_group, b_expert]), (0, LANES - N_GROUPS - N_EXPERTS))[:, None]
    return nw, wt_hi, wt_lo, bias_col, tri


N_ROW_TILES = N_TOK // TM_ROW
ROW_TILE = 8
PACK_TILE = 4
IDX_GROUPS = TM_ROW // LANES
IDX_TRIPS = LANES // ROW_UNROLL
IDX_PER_TRIP = IDX_GROUPS * ROW_UNROLL * TOP_K


def _issue_rows(idx_smem, s, row_tile, make_copy):
    for r8 in range(IDX_TRIPS):
        for g in range(IDX_GROUPS):
            for u in range(ROW_UNROLL):
                for k in range(TOP_K):
                    d = pl.multiple_of(idx_smem[s, r8, (g * ROW_UNROLL + u) * TOP_K + k] * row_tile,
                                       row_tile)
                    r = (g * LANES + r8 * ROW_UNROLL + u) * row_tile
                    make_copy(k, r, d).start(priority=k)


def _dest_tiles(dest):
    d = dest.reshape(TOP_K, N_ROW_TILES, IDX_GROUPS, IDX_TRIPS, ROW_UNROLL)
    return d.transpose(1, 3, 2, 4, 0).reshape(N_ROW_TILES, IDX_TRIPS, IDX_PER_TRIP)


def _dispatch_kernel(last_ref, x_ref, nw_ref, dest_hbm, xb_hbm, hbuf, zbuf, idx_smem, idx_sem,
                     row_sem, zero_sem):
    i = pl.program_id(0)
    slot = i % 2

    def idx_copy(tile, s):
        return pltpu.make_async_copy(dest_hbm.at[tile], idx_smem.at[s], idx_sem.at[s])

    @pl.when(i == 0)
    def _():
        zbuf[...] = jnp.zeros_like(zbuf)

        def zero_copy(row):
            row0 = pl.multiple_of(row * PACK_TILE, PACK_TILE)
            return pltpu.make_async_copy(zbuf, xb_hbm.at[pl.ds(row0, MOE_ROWS * PACK_TILE), :], zero_sem)

        def start(e, c):
            @pl.when(last_ref[e] >= 0)
            def _():
                zero_copy(last_ref[e]).start()
            return c

        def wait(e, c):
            @pl.when(last_ref[e] >= 0)
            def _():
                zero_copy(last_ref[e]).wait()
            return c

        def start_tail(b, c):
            zero_copy(b * MOE_ROWS).start()
            return c

        def wait_tail(b, c):
            zero_copy(b * MOE_ROWS).wait()
            return c

        n_used = last_ref[N_EXPERTS]
        lax.fori_loop(0, N_EXPERTS, start, 0)
        lax.fori_loop(n_used, N_MOE_BLOCKS, start_tail, 0)
        lax.fori_loop(0, N_EXPERTS, wait, 0)
        lax.fori_loop(n_used, N_MOE_BLOCKS, wait_tail, 0)

    def drain(s):
        for _ in range(TOP_K):
            pltpu.make_async_copy(hbuf.at[s], xb_hbm.at[pl.ds(0, TM_ROW * PACK_TILE), :],
                                  row_sem.at[s]).wait()

    @pl.when(i == 0)
    def _():
        idx_copy(0, 0).start()

    idx_copy(i, slot).wait()

    @pl.when(i + 1 < N_ROW_TILES)
    def _():
        idx_copy(i + 1, 1 - slot).start()

    h = _rms(x_ref[...], nw_ref[...])
    half = D_MODEL // 2
    lo = pltpu.bitcast(h[:, 0:half].astype(bf16).astype(f32), jnp.uint32)
    hi = pltpu.bitcast(h[:, half:D_MODEL].astype(bf16).astype(f32), jnp.uint32)
    packed = (lo >> 16) | hi
    for s in range(2):
        @pl.when(slot == s)
        def _(s=s):
            @pl.when(i >= 2)
            def _():
                drain(s)
            for c in range(PACK_TILE):
                hbuf[s, pl.ds(c, TM_ROW, stride=PACK_TILE), :] = packed[:, c * LANES:(c + 1) * LANES]
            _issue_rows(idx_smem, s, PACK_TILE, lambda k, r, d: pltpu.make_async_copy(
                hbuf.at[s, pl.ds(r, PACK_TILE), :], xb_hbm.at[pl.ds(d, PACK_TILE), :], row_sem.at[s]))

    @pl.when(i == N_ROW_TILES - 1)
    def _():
        drain(0)
        drain(1)


def _dispatch(x, nw, dest_tiles, zero_plan):
    grid_spec = pltpu.PrefetchScalarGridSpec(
        num_scalar_prefetch=1,
        grid=(N_ROW_TILES,),
        in_specs=[pl.BlockSpec((TM_ROW, D_MODEL), lambda i, lb: (i, 0)),
                  pl.BlockSpec((1, D_MODEL), lambda i, lb: (0, 0)),
                  pl.BlockSpec(memory_space=pl.ANY)],
        out_specs=pl.BlockSpec(memory_space=pl.ANY),
        scratch_shapes=[pltpu.VMEM((2, TM_ROW * PACK_TILE, LANES), jnp.uint32),
                        pltpu.VMEM((MOE_ROWS * PACK_TILE, LANES), jnp.uint32),
                        pltpu.SMEM((2, IDX_TRIPS, IDX_PER_TRIP), jnp.int32),
                        pltpu.SemaphoreType.DMA((2,)),
                        pltpu.SemaphoreType.DMA((2,)),
                        pltpu.SemaphoreType.DMA])
    return pl.pallas_call(
        _dispatch_kernel,
        grid_spec=grid_spec,
        out_shape=jax.ShapeDtypeStruct((N_MOE_ROWS * PACK_TILE, LANES), jnp.uint32),
        compiler_params=_cparams(("arbitrary",)),
        name="moe_dispatch",
    )(zero_plan, x, nw, dest_tiles)


MOE_BLOCK_SUBLANES = MOE_ROWS * ROW_TILE
MOE_PACKED_SUBLANES = MOE_ROWS * PACK_TILE
X_RING = 3


def _expert_kernel(first_ref, nblk_ref, nused_ref, xb_hbm, wg_ref, wu_ref, wd_ref, yb_hbm,
                   xbuf, ybuf, wg_b, wu_b, wd_b, in_sem, out_sem, zero_sem):
    e = pl.program_id(0)
    n_used = nused_ref[0]

    def block_rows(g):
        return pl.ds(pl.multiple_of(g * MOE_BLOCK_SUBLANES, MOE_BLOCK_SUBLANES), MOE_BLOCK_SUBLANES)

    def x_copy(g, s):
        rows = pl.ds(pl.multiple_of(g * MOE_PACKED_SUBLANES, MOE_PACKED_SUBLANES), MOE_PACKED_SUBLANES)
        return pltpu.make_async_copy(xb_hbm.at[rows, :], xbuf.at[s], in_sem.at[s])

    def y_copy(g, s):
        return pltpu.make_async_copy(ybuf.at[s], yb_hbm.at[block_rows(g), :], out_sem.at[s])

    @pl.when(e == 0)
    def _():
        for g0 in range(X_RING - 1):
            @pl.when(g0 < n_used)
            def _(g0=g0):
                x_copy(g0, g0).start()

    @pl.when(nblk_ref[e] > 0)
    def _():
        wg_b[...] = wg_ref[...].astype(bf16)
        wu_b[...] = wu_ref[...].astype(bf16)
        wd_b[...] = wd_ref[...].astype(bf16)

    def block(i, c):
        g = first_ref[e] + i
        s = g % 2
        sx = lax.rem(g, X_RING)
        x_copy(g, sx).wait()

        @pl.when(g + X_RING - 1 < n_used)
        def _():
            x_copy(g + X_RING - 1, lax.rem(g + X_RING - 1, X_RING)).start()

        @pl.when(g >= 2)
        def _():
            y_copy(g - 2, s).wait()

        words = [xbuf[sx, pl.ds(k, MOE_ROWS, stride=PACK_TILE), :] for k in range(PACK_TILE)]
        lo = [pltpu.bitcast(w << 16, f32) for w in words]
        hi = [pltpu.bitcast(w & jnp.uint32(0xFFFF0000), f32) for w in words]
        xb = jnp.concatenate(lo + hi, axis=1).astype(bf16)
        hid = jax.nn.silu(_dot(xb, wg_b[...])) * _dot(xb, wu_b[...])
        y = _dot(hid.astype(bf16), wd_b[...])
        for k in range(ROW_TILE):
            ybuf[s, pl.ds(k, MOE_ROWS, stride=ROW_TILE), :] = y[:, k * LANES:(k + 1) * LANES]
        y_copy(g, s).start()
        return c

    lax.fori_loop(0, nblk_ref[e], block, 0)

    @pl.when(e == N_EXPERTS - 1)
    def _():
        @pl.when(n_used >= 2)
        def _():
            y_copy(n_used - 2, n_used % 2).wait()
        y_copy(n_used - 1, (n_used + 1) % 2).wait()
        ybuf[0] = jnp.zeros((MOE_BLOCK_SUBLANES, LANES), f32)

        def zero_copy(g):
            return pltpu.make_async_copy(ybuf.at[0], yb_hbm.at[block_rows(g), :], zero_sem)

        def start_tail(g, c):
            zero_copy(g).start()
            return c

        def wait_tail(g, c):
            zero_copy(g).wait()
            return c

        lax.fori_loop(n_used, N_MOE_BLOCKS, start_tail, 0)
        lax.fori_loop(n_used, N_MOE_BLOCKS, wait_tail, 0)


def _experts(layer, first_block, n_blocks, n_used, xb, w_gate, w_up, w_down):
    grid_spec = pltpu.PrefetchScalarGridSpec(
        num_scalar_prefetch=3,
        grid=(N_EXPERTS,),
        in_specs=[pl.BlockSpec(memory_space=pl.ANY),
                  pl.BlockSpec((None, None, D_MODEL, D_FF_EXPERT), lambda e, fb, nb, nu: (layer, e, 0, 0)),
                  pl.BlockSpec((None, None, D_MODEL, D_FF_EXPERT), lambda e, fb, nb, nu: (layer, e, 0, 0)),
                  pl.BlockSpec((None, None, D_FF_EXPERT, D_MODEL), lambda e, fb, nb, nu: (layer, e, 0, 0))],
        out_specs=pl.BlockSpec(memory_space=pl.ANY),
        scratch_shapes=[pltpu.VMEM((X_RING, MOE_PACKED_SUBLANES, LANES), jnp.uint32),
                        pltpu.VMEM((2, MOE_BLOCK_SUBLANES, LANES), f32),
                        pltpu.VMEM((D_MODEL, D_FF_EXPERT), bf16),
                        pltpu.VMEM((D_MODEL, D_FF_EXPERT), bf16),
                        pltpu.VMEM((D_FF_EXPERT, D_MODEL), bf16),
                        pltpu.SemaphoreType.DMA((X_RING,)),
                        pltpu.SemaphoreType.DMA((2,)),
                        pltpu.SemaphoreType.DMA])
    return pl.pallas_call(
        _expert_kernel,
        grid_spec=grid_spec,
        out_shape=jax.ShapeDtypeStruct((N_MOE_ROWS * ROW_TILE, LANES), f32),
        compiler_params=_cparams(("arbitrary",)),
        name="moe_experts",
    )(first_block, n_blocks, n_used, xb, w_gate, w_up, w_down)


def _combine_kernel(x_ref, gate_ref, dest_hbm, yb_hbm, o_ref, ybuf, idx_smem, idx_sem, row_sem):
    i = pl.program_id(0)
    slot = i % 2

    def idx_copy(tile, s):
        return pltpu.make_async_copy(dest_hbm.at[tile], idx_smem.at[s], idx_sem.at[s])

    def gather(s):
        _issue_rows(idx_smem, s, ROW_TILE, lambda k, r, d: pltpu.make_async_copy(
            yb_hbm.at[pl.ds(d, ROW_TILE), :], ybuf.at[s, k, pl.ds(r, ROW_TILE), :], row_sem.at[s]))

    @pl.when(i == 0)
    def _():
        idx_copy(0, 0).start()
        idx_copy(0, 0).wait()
        gather(0)
        idx_copy(1, 1).start()

    for s in range(2):
        @pl.when(jnp.logical_and(i + 1 < N_ROW_TILES, slot == 1 - s))
        def _(s=s):
            idx_copy(i + 1, s).wait()
            gather(s)

    @pl.when(i + 2 < N_ROW_TILES)
    def _():
        idx_copy(i + 2, slot).start()

    g0 = gate_ref[:, 0:1]
    g1 = gate_ref[:, 1:2]
    for s in range(2):
        @pl.when(slot == s)
        def _(s=s):
            for k in range(TOP_K):
                pltpu.make_async_copy(yb_hbm.at[pl.ds(0, TM_ROW * ROW_TILE), :], ybuf.at[s, k],
                                      row_sem.at[s]).wait()
            for c in range(ROW_TILE):
                y0 = ybuf[s, 0, pl.ds(c, TM_ROW, stride=ROW_TILE), :]
                y1 = ybuf[s, 1, pl.ds(c, TM_ROW, stride=ROW_TILE), :]
                lanes = slice(c * LANES, (c + 1) * LANES)
                o_ref[:, lanes] = x_ref[:, lanes] + (y0 * g0 + y1 * g1)


def _combine(x, gate_cols, dest_tiles, yb):
    return pl.pallas_call(
        _combine_kernel,
        grid=(N_ROW_TILES,),
        in_specs=[pl.BlockSpec((TM_ROW, D_MODEL), lambda i: (i, 0)),
                  pl.BlockSpec((TM_ROW, TOP_K), lambda i: (i, 0)),
                  pl.BlockSpec(memory_space=pl.ANY),
                  pl.BlockSpec(memory_space=pl.ANY)],
        out_specs=pl.BlockSpec((TM_ROW, D_MODEL), lambda i: (i, 0)),
        out_shape=jax.ShapeDtypeStruct((N_TOK, D_MODEL), f32),
        scratch_shapes=[pltpu.VMEM((2, TOP_K, TM_ROW * ROW_TILE, LANES), f32),
                        pltpu.SMEM((2, IDX_TRIPS, IDX_PER_TRIP), jnp.int32),
                        pltpu.SemaphoreType.DMA((2,)),
                        pltpu.SemaphoreType.DMA((2,))],
        compiler_params=_cparams(("arbitrary",)),
        name="moe_combine",
    )(x, gate_cols, dest_tiles, yb)


def _hier_moe(layer, x, nw, idx, gates, cnt, w_gate, w_up, w_down):
    counts = cnt[:, 0].astype(jnp.int32)
    padded = (counts + MOE_ROWS - 1) // MOE_ROWS * MOE_ROWS
    pends = jnp.cumsum(padded)
    pstarts = pends - padded
    eids = jnp.arange(N_EXPERTS, dtype=jnp.int32)[:, None, None]
    seg_start = jnp.sum(jnp.where(idx[None, 0:2] == eids, pstarts[:, None, None], 0), axis=0)
    dest = seg_start + idx[2:4]
    dest_tiles = _dest_tiles(dest)
    n_used = (pends[-1:] // MOE_ROWS).astype(jnp.int32)

    zero_plan = jnp.concatenate([jnp.where(padded > 0, pends - MOE_ROWS, -1), n_used]).astype(jnp.int32)
    xb = _dispatch(x, nw, dest_tiles, zero_plan)
    yb = _experts(layer, (pstarts // MOE_ROWS).astype(jnp.int32), (padded // MOE_ROWS).astype(jnp.int32),
                  n_used, xb, w_gate, w_up, w_down)
    return _combine(x, gates[0:2].T, dest_tiles, yb)


def kernel(x, positions, conv_norm_w, conv_w_in, conv_k, conv_w_out, mla_norm_w, mla_w_down, mla_q_lat_norm, mla_kv_lat_norm, mla_w_q_up, mla_w_kv_up, mla_q_head_norm, mla_k_head_norm, mla_w_out, ffn_norm_w, router_w_group, router_b_group, router_w_expert, router_b_expert, exp_w_gate, exp_w_up, exp_w_down):
    half = QK_ROPE_DIM // 2
    inv_freq = ROPE_THETA ** (-jnp.arange(0, QK_ROPE_DIM, 2, dtype=f32) / QK_ROPE_DIM)
    ang_t = inv_freq[None, :, None] * positions.astype(f32)[:, None, :]
    cos_t = jnp.cos(ang_t)
    sin_t = jnp.sin(ang_t)
    cos = cos_t.transpose(0, 2, 1)
    sin = sin_t.transpose(0, 2, 1)
    cs = jnp.concatenate([cos, cos, -sin, sin], axis=-1).reshape(N_TOK, LANES)
    row = lax.broadcasted_iota(jnp.int32, (TM, TM), 0)
    col = lax.broadcasted_iota(jnp.int32, (TM, TM), 1)
    tri = (row < col).astype(bf16)

    xt = x.reshape(N_TOK, D_MODEL)
    for i in range(DEPTH):
        j = i // N_MIXERS
        route_ops = _router_operands(ffn_norm_w[i][None, :], router_w_group[i], router_b_group[i],
                                     router_w_expert[i], router_b_expert[i], tri)
        if i % N_MIXERS == 0:
            z, gb = _conv_in(xt, conv_norm_w[j][None, :], conv_w_in[j].astype(bf16))
            xt, idx, gates, cnt = _conv_out(xt, z, gb, conv_k[j], conv_w_out[j].astype(bf16), route_ops)
        else:
            wd = mla_w_down[j]
            ro = Q_LORA_RANK + KV_LORA_RANK
            wd_aug = jnp.concatenate([wd, wd[:, ro + half:ro + 2 * half], wd[:, ro:ro + half]],
                                     axis=1).astype(bf16)
            wkv = mla_w_kv_up[j].reshape(KV_LORA_RANK, N_HEADS, QK_NOPE_DIM + V_HEAD_DIM)
            wk = wkv[:, :, :QK_NOPE_DIM].reshape(KV_LORA_RANK, N_HEADS * QK_NOPE_DIM).astype(bf16)
            wvt = wkv[:, :, QK_NOPE_DIM:].reshape(KV_LORA_RANK, N_HEADS * V_HEAD_DIM).T.astype(bf16)
            wqt = mla_w_q_up[j].T.astype(bf16)
            khn = mla_k_head_norm[j]
            kw1 = khn[QK_NOPE_DIM:QK_NOPE_DIM + half]
            kw2 = khn[QK_NOPE_DIM + half:]
            g128 = jnp.concatenate([kw1, kw2, kw2, kw1])[None, :]
            qt, k, vt = _mla_proj(xt, mla_norm_w[j][None, :], wd_aug, mla_q_lat_norm[j][None, :],
                                  mla_kv_lat_norm[j][None, :], wqt, wk, wvt,
                                  mla_q_head_norm[j][:, None], khn[None, :], g128, cs, cos_t, sin_t)
            ot = _attention(qt, k, vt)
            xt, idx, gates, cnt = _mla_out(xt, ot.reshape(BATCH, N_HEADS * V_HEAD_DIM, SEQ),
                                           mla_w_out[j].T.astype(bf16), route_ops)
        xt = _hier_moe(i, xt, ffn_norm_w[i][None, :], idx, gates, cnt, exp_w_gate, exp_w_up, exp_w_down)
    return xt.reshape(BATCH, SEQ, D_MODEL)
```
